```python
import jax, jax.numpy as jnp
from jax import lax
import numpy as np

D_MODEL = 1024
BATCH = 4
SEQ = 4096
DEPTH = 4

N_MIXERS = 3
PLE_DIM = 256
D_FF = 4 * D_MODEL
RMS_EPS = 1e-6
SC_WIDTH = 3
ATTN_HEAD_DIM = 128
ATTN_HEADS_PER_GROUP = D_MODEL // ATTN_HEAD_DIM
DILATED_PATTERNS = ((128, 1), (512, 4), (2048, 16))
N_ATTN_GROUPS = len(DILATED_PATTERNS)
ROPE_THETA = 500000.0
ROPE_DIM = ATTN_HEAD_DIM // 4
D_RNN = 1280
N_LRU_BLOCKS = 10
LRU_BLOCK = D_RNN // N_LRU_BLOCKS
LRU_CONV_WIDTH = 4
LRU_C = 8.0

N_LAYERS_A = len(range(0, DEPTH, N_MIXERS))
N_LAYERS_B = len(range(1, DEPTH, N_MIXERS))
N_LAYERS_C = len(range(2, DEPTH, N_MIXERS))

kernel_name = 'hybrid_conv_dilattn_rglru_trunk'


def rmsnorm(x, g):
    xf = x.astype(jnp.float32)
    y = xf * lax.rsqrt(jnp.mean(xf * xf, axis=-1, keepdims=True) + RMS_EPS)
    return (y * g.astype(jnp.float32)).astype(x.dtype)


def causal_depthwise_conv(x, w):
    k_width, ch = w.shape
    return lax.conv_general_dilated(
        x, w[:, None, :].astype(x.dtype), window_strides=(1,),
        padding=[(k_width - 1, 0)], dimension_numbers=('NWC', 'WIO', 'NWC'),
        feature_group_count=ch)


def partial_rotary(x, positions):
    half = ROPE_DIM // 2
    inv_freq = ROPE_THETA ** (-2.0 * jnp.arange(half, dtype=jnp.float32) / ROPE_DIM)
    ang = positions.astype(jnp.float32)[..., None] * inv_freq
    cos = jnp.cos(ang)[:, :, None, :]
    sin = jnp.sin(ang)[:, :, None, :]
    xf = x.astype(jnp.float32)
    x1 = xf[..., :half]
    x2 = xf[..., half:ROPE_DIM]
    out = jnp.concatenate([x1 * cos - x2 * sin, x2 * cos + x1 * sin, xf[..., ROPE_DIM:]], axis=-1)
    return out.astype(x.dtype)


def dilated_window_attention(q, k, v, window, dilation):
    b_, s_, h_, dh = q.shape
    n_back = window // dilation
    blk = n_back
    sub_len = s_ // dilation
    n_blk = -(-sub_len // blk)
    sub_pad = n_blk * blk

    def to_sub(t):
        t = t.reshape(b_, sub_len, dilation, h_, dh).transpose(0, 2, 1, 3, 4)
        t = t.reshape(b_ * dilation, sub_len, h_, dh)
        t = jnp.pad(t, ((0, 0), (0, sub_pad - sub_len), (0, 0), (0, 0)))
        return t.reshape(b_ * dilation, n_blk, blk, h_, dh)

    def with_prev(t):
        prev = jnp.pad(t[:, :-1], ((0, 0), (1, 0), (0, 0), (0, 0), (0, 0)))
        return jnp.concatenate([prev, t], axis=2)

    qb = to_sub(q)
    kk = with_prev(to_sub(k))
    vv = with_prev(to_sub(v))
    s = jnp.einsum('nbqhd,nbkhd->nbhqk', qb, kk,
                   preferred_element_type=jnp.float32) * (dh ** -0.5)
    qi = jnp.arange(blk)[:, None]
    kj = jnp.arange(2 * blk)[None, :]
    rel = blk + qi - kj
    band = (rel >= 0) & (rel <= n_back)
    k_pos = jnp.arange(n_blk)[:, None, None] * blk - blk + kj[None]
    valid = band[None] & (k_pos >= 0)
    s = jnp.where(valid[None, :, None], s, -jnp.inf)
    lse = jax.nn.logsumexp(s, axis=-1)
    probs = jnp.exp(s - lse[..., None])
    o = jnp.einsum('nbhqk,nbkhd->nbqhd', probs, vv.astype(jnp.float32))

    def from_sub(t):
        t = t.reshape((b_, dilation, sub_pad) + t.shape[3:])[:, :, :sub_len]
        t = jnp.moveaxis(t, 1, 2)
        return t.reshape((b_, s_) + t.shape[3:])

    return from_sub(o), from_sub(jnp.swapaxes(lse, 2, 3))


def short_conv_mixer(h, w_in, w_conv, w_out):
    gate_b, gate_c, xin = jnp.split(h @ w_in, 3, axis=-1)
    y = gate_b * causal_depthwise_conv(gate_c * xin, w_conv)
    return y @ w_out


def dilated_attention_mixer(h, positions, w_qkv, w_o):
    b_, s_, _ = h.shape
    g_, hh, dh = N_ATTN_GROUPS, ATTN_HEADS_PER_GROUP, ATTN_HEAD_DIM
    qkv = (h @ w_qkv).reshape(b_, s_, 3, g_ * hh, dh)
    q = partial_rotary(qkv[:, :, 0], positions).reshape(b_, s_, g_, hh, dh)
    k = partial_rotary(qkv[:, :, 1], positions).reshape(b_, s_, g_, hh, dh)
    v = qkv[:, :, 2].reshape(b_, s_, g_, hh, dh)
    outs, lses = [], []
    for g, (window, dilation) in enumerate(DILATED_PATTERNS):
        o_g, lse_g = dilated_window_attention(q[:, :, g], k[:, :, g], v[:, :, g], window, dilation)
        outs.append(o_g)
        lses.append(lse_g)
    weights = jax.nn.softmax(jnp.stack(lses, axis=0), axis=0)
    o = jnp.sum(weights[..., None] * jnp.stack(outs, axis=0), axis=0)
    return o.reshape(b_, s_, hh * dh).astype(h.dtype) @ w_o


def rglru_mixer(h, w_in, conv_w, conv_b, w_a, b_a, w_x, b_x, lam, w_out):
    b_, s_, _ = h.shape
    gate, xr = jnp.split(h @ w_in, 2, axis=-1)
    xr = causal_depthwise_conv(xr, conv_w) + conv_b
    xb = xr.reshape(b_, s_, N_LRU_BLOCKS, LRU_BLOCK)
    r_gate = jax.nn.sigmoid(jnp.einsum('bsnj,njk->bsnk', xb, w_a).reshape(b_, s_, D_RNN) + b_a)
    i_gate = jax.nn.sigmoid(jnp.einsum('bsnj,njk->bsnk', xb, w_x).reshape(b_, s_, D_RNN) + b_x)
    log_a = -LRU_C * r_gate.astype(jnp.float32) * jax.nn.softplus(-lam.astype(jnp.float32))
    a = jnp.exp(log_a)
    mult = jnp.sqrt(-jnp.expm1(2.0 * log_a))
    u = mult * (i_gate * xr).astype(jnp.float32)

    def combine(left, right):
        a_l, u_l = left
        a_r, u_r = right
        return a_l * a_r, a_r * u_l + u_r

    _, hs = lax.associative_scan(combine, (a, u), axis=1)
    y = hs.astype(h.dtype) * jax.nn.gelu(gate)
    return y @ w_out


def squared_relu_mlp(h, w_up, w_down):
    return jnp.square(jax.nn.relu(h @ w_up)) @ w_down


def setup_inputs(seed: int = 0) -> dict:
    key = jax.random.key(seed)
    ks = iter(jax.random.split(key, 40))

    def nrm(shape, fan_in):
        return jax.random.normal(next(ks), shape, jnp.float32) * (fan_in ** -0.5)

    def gain(shape):
        return 1.0 + 0.05 * jax.random.normal(next(ks), shape, jnp.float32)

    def bias(shape):
        return 0.1 * jax.random.normal(next(ks), shape, jnp.float32)

    attn_width = N_ATTN_GROUPS * ATTN_HEADS_PER_GROUP * ATTN_HEAD_DIM
    u = jax.random.uniform(next(ks), (N_LAYERS_C, D_RNN), jnp.float32, 0.9, 0.999)
    sig = u ** (1.0 / LRU_C)
    return {
        'x': jax.random.normal(next(ks), (BATCH, SEQ, D_MODEL), jnp.float32),
        'p': jax.random.normal(next(ks), (DEPTH, BATCH, SEQ, PLE_DIM), jnp.float32),
        'positions': (jnp.arange(SEQ, dtype=jnp.int32)[None, :]
                      + jax.random.randint(next(ks), (BATCH, 1), 0, 1024, jnp.int32)),
        'norm_mix': gain((DEPTH, D_MODEL)),
        'norm_mlp': gain((DEPTH, D_MODEL)),
        'norm_ple': gain((DEPTH, D_MODEL)),
        'norm_final': gain((D_MODEL,)),
        'sc_w_in': nrm((N_LAYERS_A, D_MODEL, 3 * D_MODEL), D_MODEL),
        'sc_w_conv': nrm((N_LAYERS_A, SC_WIDTH, D_MODEL), SC_WIDTH),
        'sc_w_out': nrm((N_LAYERS_A, D_MODEL, D_MODEL), D_MODEL),
        'attn_w_qkv': nrm((N_LAYERS_B, D_MODEL, 3 * attn_width), D_MODEL),
        'attn_w_o': nrm((N_LAYERS_B, ATTN_HEADS_PER_GROUP * ATTN_HEAD_DIM, D_MODEL),
                        ATTN_HEADS_PER_GROUP * ATTN_HEAD_DIM),
        'lru_w_in': nrm((N_LAYERS_C, D_MODEL, 2 * D_RNN), D_MODEL),
        'lru_conv_w': nrm((N_LAYERS_C, LRU_CONV_WIDTH, D_RNN), LRU_CONV_WIDTH),
        'lru_conv_b': bias((N_LAYERS_C, D_RNN)),
        'lru_w_a': nrm((N_LAYERS_C, N_LRU_BLOCKS, LRU_BLOCK, LRU_BLOCK), LRU_BLOCK),
        'lru_b_a': bias((N_LAYERS_C, D_RNN)),
        'lru_w_x': nrm((N_LAYERS_C, N_LRU_BLOCKS, LRU_BLOCK, LRU_BLOCK), LRU_BLOCK),
        'lru_b_x': bias((N_LAYERS_C, D_RNN)),
        'lru_lambda': jnp.log(sig) - jnp.log1p(-sig),
        'lru_w_out': nrm((N_LAYERS_C, D_RNN, D_MODEL), D_RNN),
        'mlp_w_up': nrm((DEPTH, D_MODEL, D_FF), D_MODEL),
        'mlp_w_down': nrm((DEPTH, D_FF, D_MODEL), D_FF),
        'ple_w_gate': nrm((DEPTH, D_MODEL, D_MODEL), D_MODEL),
        'ple_w_proj': nrm((DEPTH, PLE_DIM, D_MODEL), PLE_DIM),
    }


def reference(x, p, positions, norm_mix, norm_mlp, norm_ple, norm_final,
              sc_w_in, sc_w_conv, sc_w_out, attn_w_qkv, attn_w_o,
              lru_w_in, lru_conv_w, lru_conv_b, lru_w_a, lru_b_a, lru_w_x, lru_b_x,
              lru_lambda, lru_w_out, mlp_w_up, mlp_w_down, ple_w_gate, ple_w_proj):
    h = x
    for i in range(DEPTH):
        kind, j = i % N_MIXERS, i // N_MIXERS
        hn = rmsnorm(h, norm_mix[i])
        if kind == 0:
            mixed = short_conv_mixer(hn, sc_w_in[j], sc_w_conv[j], sc_w_out[j])
        elif kind == 1:
            mixed = dilated_attention_mixer(hn, positions, attn_w_qkv[j], attn_w_o[j])
        else:
            mixed = rglru_mixer(hn, lru_w_in[j], lru_conv_w[j], lru_conv_b[j], lru_w_a[j],
                                lru_b_a[j], lru_w_x[j], lru_b_x[j], lru_lambda[j], lru_w_out[j])
        h = h + mixed
        h = h + squared_relu_mlp(rmsnorm(h, norm_mlp[i]), mlp_w_up[i], mlp_w_down[i])
        ple_gate = jax.nn.sigmoid(rmsnorm(h, norm_ple[i]) @ ple_w_gate[i])
        h = h + ple_gate * (p[i].astype(h.dtype) @ ple_w_proj[i])
    return rmsnorm(h, norm_final)
```

```python
import functools

import jax
import jax.numpy as jnp
from jax import lax
from jax.experimental import pallas as pl
from jax.experimental.pallas import tpu as pltpu

D_MODEL = 1024
DEPTH = 4
N_MIXERS = 3
PLE_DIM = 256
D_FF = 4 * D_MODEL
RMS_EPS = 1e-6
SC_WIDTH = 3
HEAD_DIM = 128
N_HEADS = D_MODEL // HEAD_DIM
DILATED_PATTERNS = ((128, 1), (512, 4), (2048, 16))
ROPE_THETA = 500000.0
ROPE_DIM = HEAD_DIM // 4
ROPE_HALF = ROPE_DIM // 2
D_RNN = 1280
N_LRU_BLOCKS = 10
LRU_BLOCK = D_RNN // N_LRU_BLOCKS
LRU_CONV_WIDTH = 4
LRU_C = 8.0

ATTN_BLOCK = 128
MAX_DILATION = 16
LSE_LANES_PER_HEAD = HEAD_DIM // N_HEADS
SUBLANES = 8
HALO = SUBLANES
MASK_VALUE = -1e30
VMEM_LIMIT_BYTES = 56 * 1024 * 1024

F32 = jnp.float32
BF16 = jnp.bfloat16


def _rms(x, g):
    return x * lax.rsqrt(jnp.mean(x * x, axis=-1, keepdims=True) + RMS_EPS) * g


def _dot(a, b):
    return jnp.dot(a, b, preferred_element_type=F32)


def _resident(shape):
    return pl.BlockSpec(shape, lambda *_: (0,) * len(shape), pipeline_mode=pl.Buffered(1))


def _params(semantics):
    return pltpu.CompilerParams(dimension_semantics=semantics, vmem_limit_bytes=VMEM_LIMIT_BYTES)


def _post_kernel(h_ref, p_ref, gm_ref, gp_ref, gf_ref, wup_ref, wdn_ref, wg_ref, wp_ref, o_ref,
                 *, final, tf):
    h = h_ref[...]
    xn = _rms(h, gm_ref[...]).astype(BF16)
    acc = h
    for f in range(D_FF // tf):
        u = _dot(xn, wup_ref[:, f * tf:(f + 1) * tf])
        a = jnp.square(jnp.maximum(u, 0.0)).astype(BF16)
        acc = acc + _dot(a, wdn_ref[f * tf:(f + 1) * tf, :])
    xg = _rms(acc, gp_ref[...]).astype(BF16)
    gate = jax.nn.sigmoid(_dot(xg, wg_ref[...]))
    proj = _dot(p_ref[...].astype(BF16), wp_ref[...])
    out = acc + gate * proj
    if final:
        out = _rms(out, gf_ref[...])
    o_ref[...] = out


def _post(h, p_i, g_mlp, g_ple, g_final, w_up, w_down, w_gate, w_proj, *, final, tm=512, tf=512):
    b, s, d = h.shape
    tok = lambda width: pl.BlockSpec((None, tm, width), lambda bi, si: (bi, si, 0))
    return pl.pallas_call(
        functools.partial(_post_kernel, final=final, tf=tf),
        grid=(b, s // tm),
        in_specs=[tok(d), tok(PLE_DIM), _resident((1, d)), _resident((1, d)), _resident((1, d)),
                  _resident((d, D_FF)), _resident((D_FF, d)), _resident((d, d)),
                  _resident((PLE_DIM, d))],
        out_specs=tok(d),
        out_shape=jax.ShapeDtypeStruct(h.shape, F32),
        compiler_params=_params(("parallel", "parallel")),
        name="mlp_ple",
    )(h, p_i, g_mlp[None], g_ple[None], g_final[None], w_up.astype(BF16), w_down.astype(BF16),
      w_gate.astype(BF16), w_proj.astype(BF16))


def _causal_taps(buf_ref, x, w, tm):
    k_width = w.shape[0]
    buf_ref[HALO:HALO + tm, :] = x
    out = w[k_width - 1:k_width] * x
    for back in range(1, k_width):
        out = out + w[k_width - 1 - back:k_width - back] * buf_ref[HALO - back:HALO - back + tm, :]
    buf_ref[0:HALO, :] = buf_ref[tm:tm + HALO, :]
    return out


def _conv_mixer_kernel(h_ref, gn_ref, win_ref, wconv_ref, wout_ref, o_ref, buf_ref, *, tm):
    @pl.when(pl.program_id(1) == 0)
    def _():
        buf_ref[0:HALO, :] = jnp.zeros((HALO, D_MODEL), F32)

    h = h_ref[...]
    xn = _rms(h, gn_ref[...]).astype(BF16)
    gate_b = _dot(xn, win_ref[:, 0:D_MODEL])
    gate_c = _dot(xn, win_ref[:, D_MODEL:2 * D_MODEL])
    xin = _dot(xn, win_ref[:, 2 * D_MODEL:3 * D_MODEL])
    conv = _causal_taps(buf_ref, gate_c * xin, wconv_ref[...], tm)
    y = (gate_b * conv).astype(BF16)
    o_ref[...] = h + _dot(y, wout_ref[...])


def _conv_mixer(h, g, w_in, w_conv, w_out, *, tm=512):
    b, s, d = h.shape
    tok = pl.BlockSpec((None, tm, d), lambda bi, si: (bi, si, 0))
    return pl.pallas_call(
        functools.partial(_conv_mixer_kernel, tm=tm),
        grid=(b, s // tm),
        in_specs=[tok, _resident((1, d)), _resident((d, 3 * d)), _resident((SC_WIDTH, d)),
                  _resident((d, d))],
        out_specs=tok,
        out_shape=jax.ShapeDtypeStruct(h.shape, F32),
        scratch_shapes=[pltpu.VMEM((HALO + tm, d), F32)],
        compiler_params=_params(("parallel", "arbitrary")),
        name="conv_mixer",
    )(h, g[None], w_in.astype(BF16), w_conv, w_out.astype(BF16))


def _lru_kernel(h_ref, gn_ref, win_ref, cw_ref, cb_ref, wax_ref, ba_ref, bx_ref, lam_ref, wout_ref,
                o_ref, buf_ref, a_ref, u_ref, carry_ref, *, tm):
    @pl.when(pl.program_id(1) == 0)
    def _():
        buf_ref[0:HALO, :] = jnp.zeros((HALO, D_RNN), F32)
        carry_ref[...] = jnp.zeros((1, D_RNN), F32)

    h = h_ref[...]
    xn = _rms(h, gn_ref[...]).astype(BF16)
    gate = _dot(xn, win_ref[:, 0:D_RNN])
    xr = _causal_taps(buf_ref, _dot(xn, win_ref[:, D_RNN:2 * D_RNN]), cw_ref[...], tm) + cb_ref[...]

    neg_c_softplus = -LRU_C * jax.nn.softplus(-lam_ref[...])
    for n in range(N_LRU_BLOCKS):
        cols = slice(n * LRU_BLOCK, (n + 1) * LRU_BLOCK)
        xb = xr[:, cols]
        ri = _dot(xb.astype(BF16), wax_ref[n])
        r_gate = jax.nn.sigmoid(ri[:, 0:LRU_BLOCK] + ba_ref[:, cols])
        i_gate = jax.nn.sigmoid(ri[:, LRU_BLOCK:2 * LRU_BLOCK] + bx_ref[:, cols])
        log_a = r_gate * neg_c_softplus[:, cols]
        a = jnp.exp(log_a)
        a_ref[:, cols] = a
        u_ref[:, cols] = jnp.sqrt(-jnp.tanh(log_a) * (a * a + 1.0)) * (i_gate * xb)

    row = lax.broadcasted_iota(jnp.int32, (SUBLANES, D_RNN), 0)

    def group(gi, h_prev):
        rows = pl.ds(pl.multiple_of(gi * SUBLANES, SUBLANES), SUBLANES)
        a = a_ref[rows, :]
        u = u_ref[rows, :]
        for k in (1, 2, 4):
            keep = row >= k
            a_back = jnp.where(keep, pltpu.roll(a, k, 0), 1.0)
            u_back = jnp.where(keep, pltpu.roll(u, k, 0), 0.0)
            u = a * u_back + u
            a = a * a_back
        hs = a * h_prev + u
        u_ref[rows, :] = hs
        return hs[SUBLANES - 1:SUBLANES, :]

    carry_ref[...] = lax.fori_loop(0, tm // SUBLANES, group, carry_ref[...])

    y = (u_ref[...] * jax.nn.gelu(gate)).astype(BF16)
    o_ref[...] = h + _dot(y, wout_ref[...])


def _lru_mixer(h, g, w_in, conv_w, conv_b, w_a, b_a, w_x, b_x, lam, w_out, *, tm=512):
    b, s, d = h.shape
    tok = pl.BlockSpec((None, tm, d), lambda bi, si: (bi, si, 0))
    w_ax = jnp.concatenate([w_a, w_x], axis=-1).astype(BF16)
    return pl.pallas_call(
        functools.partial(_lru_kernel, tm=tm),
        grid=(b, s // tm),
        in_specs=[tok, _resident((1, d)), _resident((d, 2 * D_RNN)),
                  _resident((LRU_CONV_WIDTH, D_RNN)), _resident((1, D_RNN)),
                  _resident((N_LRU_BLOCKS, LRU_BLOCK, 2 * LRU_BLOCK)), _resident((1, D_RNN)),
                  _resident((1, D_RNN)), _resident((1, D_RNN)), _resident((D_RNN, d))],
        out_specs=tok,
        out_shape=jax.ShapeDtypeStruct(h.shape, F32),
        scratch_shapes=[pltpu.VMEM((HALO + tm, D_RNN), F32), pltpu.VMEM((tm, D_RNN), F32),
                        pltpu.VMEM((tm, D_RNN), F32), pltpu.VMEM((1, D_RNN), F32)],
        compiler_params=_params(("parallel", "arbitrary")),
        name="rglru_mixer",
    )(h, g[None], w_in.astype(BF16), conv_w, conv_b[None], w_ax, b_a[None], b_x[None], lam[None],
      w_out.astype(BF16))


def _rope_table_kernel(pos_ref, invf_ref, cos_ref, sin_ref):
    ang = pos_ref[...].astype(F32) * invf_ref[...]
    lane = lax.broadcasted_iota(jnp.int32, ang.shape, 1)
    cos_ref[...] = jnp.cos(ang)
    sin_ref[...] = jnp.where(lane < ROPE_HALF, -1.0, 1.0) * jnp.sin(ang)


def _rope_tables(positions, *, ts=1024):
    b, s = positions.shape
    inv_freq = ROPE_THETA ** (-2.0 * jnp.arange(ROPE_HALF, dtype=F32) / ROPE_DIM)
    invf = jnp.concatenate([inv_freq, inv_freq, jnp.zeros((HEAD_DIM - ROPE_DIM,), F32)])[None]
    out = jax.ShapeDtypeStruct((b, s, HEAD_DIM), F32)
    tab = pl.BlockSpec((None, ts, HEAD_DIM), lambda bi, si: (bi, si, 0))
    return pl.pallas_call(
        _rope_table_kernel,
        grid=(b, s // ts),
        in_specs=[pl.BlockSpec((None, ts, 1), lambda bi, si: (bi, si, 0)), _resident((1, HEAD_DIM))],
        out_specs=[tab, tab],
        out_shape=[out, out],
        compiler_params=_params(("parallel", "parallel")),
        name="rope_tables",
    )(positions[..., None], invf)


def _qkv_kernel(h_ref, cos_ref, sin_ref, gn_ref, w_ref, q_ref, k_ref, v_ref):
    xn = _rms(h_ref[...], gn_ref[...]).astype(BF16)
    cos = cos_ref[...]
    sin = sin_ref[...]
    low_half = lax.broadcasted_iota(jnp.int32, cos.shape, 1) < ROPE_HALF
    for which, out_ref in ((0, q_ref), (1, k_ref)):
        z = _dot(xn, w_ref[:, which * D_MODEL:(which + 1) * D_MODEL])
        for hd in range(N_HEADS):
            cols = slice(hd * HEAD_DIM, (hd + 1) * HEAD_DIM)
            x = z[:, cols]
            partner = jnp.where(low_half, pltpu.roll(x, HEAD_DIM - ROPE_HALF, 1),
                                pltpu.roll(x, ROPE_HALF, 1))
            out_ref[:, cols] = (x * cos + partner * sin).astype(BF16)
    v_ref[...] = _dot(xn, w_ref[:, 2 * D_MODEL:3 * D_MODEL]).astype(BF16)


def _qkv(h, cos_t, sin_t, g, w, dil):
    b, s, d = h.shape
    sub = s // dil
    tm = min(512, sub)
    strided = lambda width: pl.BlockSpec((None, tm, width), lambda bi, ri, ti: (bi, ti, ri))
    out = jax.ShapeDtypeStruct((b, dil, sub, d), BF16)
    out_spec = pl.BlockSpec((None, None, tm, d), lambda bi, ri, ti: (bi, ri, ti, 0))
    return pl.pallas_call(
        _qkv_kernel,
        grid=(b, dil, sub // tm),
        in_specs=[strided(d), strided(HEAD_DIM), strided(HEAD_DIM), _resident((1, d)),
                  _resident((d, 3 * d))],
        out_specs=[out_spec] * 3,
        out_shape=[out] * 3,
        compiler_params=_params(("parallel", "parallel", "parallel")),
        name=f"qkv_dil{dil}",
    )(h.reshape(b, sub, dil * d), cos_t.reshape(b, sub, dil * HEAD_DIM),
      sin_t.reshape(b, sub, dil * HEAD_DIM), g[None], w)


def _attn_kernel(q_ref, k_ref, v_ref, o_ref, lse_ref, kprev_ref, vprev_ref, *, n_qblocks):
    step = pl.program_id(1)

    @pl.when(step == 0)
    def _():
        kprev_ref[...] = jnp.zeros(kprev_ref.shape, BF16)
        vprev_ref[...] = jnp.zeros(vprev_ref.shape, BF16)

    blk = ATTN_BLOCK
    qi = lax.broadcasted_iota(jnp.int32, (blk, 2 * blk), 0)
    kj = lax.broadcasted_iota(jnp.int32, (blk, 2 * blk), 1)
    rel = blk + qi - kj
    band = (rel >= 0) & (rel <= blk)
    band_first = band & ((kj >= blk) | (step > 0))
    head_of_lane = lax.broadcasted_iota(jnp.int32, (blk, HEAD_DIM), 1) // LSE_LANES_PER_HEAD
    scale = HEAD_DIM ** -0.5

    for qb in range(n_qblocks):
        rows = slice(qb * blk, (qb + 1) * blk)
        prev_rows = slice((qb - 1) * blk, qb * blk)
        lse_slab = jnp.zeros((blk, HEAD_DIM), F32)
        for hd in range(N_HEADS):
            cols = slice(hd * HEAD_DIM, (hd + 1) * HEAD_DIM)
            if qb == 0:
                k_prev, v_prev, valid = kprev_ref[:, cols], vprev_ref[:, cols], band_first
            else:
                k_prev, v_prev, valid = k_ref[prev_rows, cols], v_ref[prev_rows, cols], band
            kk = jnp.concatenate([k_prev, k_ref[rows, cols]], axis=0)
            vv = jnp.concatenate([v_prev, v_ref[rows, cols]], axis=0)
            s = lax.dot_general(q_ref[rows, cols], kk, (((1,), (1,)), ((), ())),
                                preferred_element_type=F32) * scale
            s = jnp.where(valid, s, MASK_VALUE)
            m = jnp.max(s, axis=-1, keepdims=True)
            p = jnp.exp(s - m)
            l = jnp.sum(p, axis=-1, keepdims=True)
            o_ref[rows, cols] = _dot(p.astype(BF16), vv) / l
            lse_slab = jnp.where(head_of_lane == hd, m + jnp.log(l), lse_slab)
        lse_ref[rows, :] = lse_slab

    last = slice((n_qblocks - 1) * blk, n_qblocks * blk)
    kprev_ref[...] = k_ref[last, :]
    vprev_ref[...] = v_ref[last, :]


def _band_attention(q, k, v, *, n_qblocks=2):
    n, sub, d = q.shape
    tq = n_qblocks * ATTN_BLOCK
    tok = lambda width: pl.BlockSpec((None, tq, width), lambda ni, ti: (ni, ti, 0))
    return pl.pallas_call(
        functools.partial(_attn_kernel, n_qblocks=n_qblocks),
        grid=(n, sub // tq),
        in_specs=[tok(d)] * 3,
        out_specs=[tok(d), tok(HEAD_DIM)],
        out_shape=[jax.ShapeDtypeStruct((n, sub, d), F32),
                   jax.ShapeDtypeStruct((n, sub, HEAD_DIM), F32)],
        scratch_shapes=[pltpu.VMEM((ATTN_BLOCK, d), BF16), pltpu.VMEM((ATTN_BLOCK, d), BF16)],
        compiler_params=_params(("parallel", "arbitrary")),
        name=f"band_attn_len{sub}",
    )(q, k, v)


def _combine_kernel(h_ref, o0_ref, l0_ref, o1_ref, l1_ref, o2_ref, l2_ref, wo_ref, out_ref, mix_ref):
    lses = (l0_ref[...], l1_ref[...], l2_ref[...])
    m = jnp.maximum(jnp.maximum(lses[0], lses[1]), lses[2])
    es = [jnp.exp(l - m) for l in lses]
    inv = 1.0 / (es[0] + es[1] + es[2])
    ws = [e * inv for e in es]
    o_refs = (o0_ref, o1_ref, o2_ref)
    rows = h_ref.shape[0]
    for hd in range(N_HEADS):
        cols = slice(hd * HEAD_DIM, (hd + 1) * HEAD_DIM)
        lane = hd * LSE_LANES_PER_HEAD
        mixed = None
        for w, o_ref in zip(ws, o_refs):
            term = jnp.broadcast_to(w[:, lane:lane + 1], (rows, HEAD_DIM)) * o_ref[:, cols]
            mixed = term if mixed is None else mixed + term
        mix_ref[:, cols] = mixed.astype(BF16)
    out_ref[...] = h_ref[...] + _dot(mix_ref[...], wo_ref[...])


def _combine(h, outs, lses, w_o):
    b, s, d = h.shape
    rows = s // MAX_DILATION
    views, specs = [], []
    for (_, dil), o_g, l_g in zip(DILATED_PATTERNS, outs, lses):
        inner = MAX_DILATION // dil
        for arr, width in ((o_g, d), (l_g, HEAD_DIM)):
            views.append(arr.reshape(b, dil, rows, inner * width))
            specs.append(pl.BlockSpec(
                (None, None, rows, width),
                lambda bi, ri, dil=dil: (bi, ri % dil, 0, ri // dil)))
    h_spec = pl.BlockSpec((None, rows, d), lambda bi, ri: (bi, 0, ri))
    out = pl.pallas_call(
        _combine_kernel,
        grid=(b, MAX_DILATION),
        in_specs=[h_spec] + specs + [_resident((d, d))],
        out_specs=h_spec,
        out_shape=jax.ShapeDtypeStruct((b, rows, MAX_DILATION * d), F32),
        scratch_shapes=[pltpu.VMEM((rows, d), BF16)],
        compiler_params=_params(("parallel", "parallel")),
        name="attn_combine",
    )(h.reshape(b, rows, MAX_DILATION * d), *views, w_o.astype(BF16))
    return out.reshape(b, s, d)


def _attention_mixer(h, positions, g, w_qkv, w_o):
    b, s, d = h.shape
    n_groups = len(DILATED_PATTERNS)
    cos_t, sin_t = _rope_tables(positions)
    w4 = w_qkv.reshape(d, 3, n_groups, d)
    outs, lses = [], []
    for gi, (window, dil) in enumerate(DILATED_PATTERNS):
        assert window // dil == ATTN_BLOCK and MAX_DILATION % dil == 0
        w_g = w4[:, :, gi, :].reshape(d, 3 * d).astype(BF16)
        q, k, v = _qkv(h, cos_t, sin_t, g, w_g, dil)
        sub = s // dil
        o_g, l_g = _band_attention(q.reshape(b * dil, sub, d), k.reshape(b * dil, sub, d),
                                   v.reshape(b * dil, sub, d))
        outs.append(o_g.reshape(b, dil, sub, d))
        lses.append(l_g.reshape(b, dil, sub, HEAD_DIM))
    return _combine(h, outs, lses, w_o)


def kernel(x, p, positions, norm_mix, norm_mlp, norm_ple, norm_final, sc_w_in, sc_w_conv, sc_w_out,
           attn_w_qkv, attn_w_o, lru_w_in, lru_conv_w, lru_conv_b, lru_w_a, lru_b_a, lru_w_x,
           lru_b_x, lru_lambda, lru_w_out, mlp_w_up, mlp_w_down, ple_w_gate, ple_w_proj):
    h = x
    for i in range(DEPTH):
        kind, j = i % N_MIXERS, i // N_MIXERS
        if kind == 0:
            h = _conv_mixer(h, norm_mix[i], sc_w_in[j], sc_w_conv[j], sc_w_out[j])
        elif kind == 1:
            h = _attention_mixer(h, positions, norm_mix[i], attn_w_qkv[j], attn_w_o[j])
        else:
            h = _lru_mixer(h, norm_mix[i], lru_w_in[j], lru_conv_w[j], lru_conv_b[j], lru_w_a[j],
                           lru_b_a[j], lru_w_x[j], lru_b_x[j], lru_lambda[j], lru_w_out[j])
        h = _post(h, p[i], norm_mlp[i], norm_ple[i], norm_final, mlp_w_up[i], mlp_w_down[i],
                  ple_w_gate[i], ple_w_proj[i], final=(i == DEPTH - 1))
    return h
```

```python
import functools

import jax
import jax.numpy as jnp
from jax import lax
from jax.experimental import pallas as pl
from jax.experimental.pallas import tpu as pltpu

D_MODEL = 1024
DEPTH = 4
N_MIXERS = 3
PLE_DIM = 256
D_FF = 4 * D_MODEL
RMS_EPS = 1e-6
SC_WIDTH = 3
HEAD_DIM = 128
N_HEADS = D_MODEL // HEAD_DIM
DILATED_PATTERNS = ((128, 1), (512, 4), (2048, 16))
ROPE_THETA = 500000.0
ROPE_DIM = HEAD_DIM // 4
ROPE_HALF = ROPE_DIM // 2
D_RNN = 1280
N_LRU_BLOCKS = 10
LRU_BLOCK = D_RNN // N_LRU_BLOCKS
LRU_CONV_WIDTH = 4
LRU_C = 8.0

ATTN_BLOCK = 128
LSE_LANES_PER_HEAD = HEAD_DIM // N_HEADS
SUBLANES = 8
HALO = SUBLANES
MASK_VALUE = -1e30
VMEM_LIMIT_BYTES = 56 * 1024 * 1024

F32 = jnp.float32
BF16 = jnp.bfloat16


def _rms(x, g):
    return x * lax.rsqrt(jnp.mean(x * x, axis=-1, keepdims=True) + RMS_EPS) * g


def _dot(a, b):
    return jnp.dot(a, b, preferred_element_type=F32)


def _resident(shape):
    return pl.BlockSpec(shape, lambda *_: (0,) * len(shape), pipeline_mode=pl.Buffered(1))


def _params(semantics):
    return pltpu.CompilerParams(dimension_semantics=semantics, vmem_limit_bytes=VMEM_LIMIT_BYTES)


def _post_kernel(h_ref, p_ref, gm_ref, gp_ref, gf_ref, wup_ref, wdn_ref, wg_ref, wp_ref, o_ref,
                 *, final, tf):
    h = h_ref[...]
    xn = _rms(h, gm_ref[...]).astype(BF16)
    acc = h
    for f in range(D_FF // tf):
        u = _dot(xn, wup_ref[:, f * tf:(f + 1) * tf])
        a = jnp.square(jnp.maximum(u, 0.0)).astype(BF16)
        acc = acc + _dot(a, wdn_ref[f * tf:(f + 1) * tf, :])
    xg = _rms(acc, gp_ref[...]).astype(BF16)
    gate = jax.nn.sigmoid(_dot(xg, wg_ref[...]))
    proj = _dot(p_ref[...].astype(BF16), wp_ref[...])
    out = acc + gate * proj
    if final:
        out = _rms(out, gf_ref[...])
    o_ref[...] = out


def _post(h, p_i, g_mlp, g_ple, g_final, w_up, w_down, w_gate, w_proj, *, final, tm=512, tf=512):
    b, s, d = h.shape
    tok = lambda width: pl.BlockSpec((None, tm, width), lambda bi, si: (bi, si, 0))
    return pl.pallas_call(
        functools.partial(_post_kernel, final=final, tf=tf),
        grid=(b, s // tm),
        in_specs=[tok(d), tok(PLE_DIM), _resident((1, d)), _resident((1, d)), _resident((1, d)),
                  _resident((d, D_FF)), _resident((D_FF, d)), _resident((d, d)),
                  _resident((PLE_DIM, d))],
        out_specs=tok(d),
        out_shape=jax.ShapeDtypeStruct(h.shape, F32),
        compiler_params=_params(("parallel", "parallel")),
        name="mlp_ple",
    )(h, p_i, g_mlp[None], g_ple[None], g_final[None], w_up.astype(BF16), w_down.astype(BF16),
      w_gate.astype(BF16), w_proj.astype(BF16))


def _causal_taps(buf_ref, x, w, tm):
    k_width = w.shape[0]
    buf_ref[HALO:HALO + tm, :] = x
    out = w[k_width - 1:k_width] * x
    for back in range(1, k_width):
        out = out + w[k_width - 1 - back:k_width - back] * buf_ref[HALO - back:HALO - back + tm, :]
    buf_ref[0:HALO, :] = buf_ref[tm:tm + HALO, :]
    return out


def _conv_mixer_kernel(h_ref, gn_ref, win_ref, wconv_ref, wout_ref, o_ref, buf_ref, *, tm):
    @pl.when(pl.program_id(1) == 0)
    def _():
        buf_ref[0:HALO, :] = jnp.zeros((HALO, D_MODEL), F32)

    h = h_ref[...]
    xn = _rms(h, gn_ref[...]).astype(BF16)
    gate_b = _dot(xn, win_ref[:, 0:D_MODEL])
    gate_c = _dot(xn, win_ref[:, D_MODEL:2 * D_MODEL])
    xin = _dot(xn, win_ref[:, 2 * D_MODEL:3 * D_MODEL])
    conv = _causal_taps(buf_ref, gate_c * xin, wconv_ref[...], tm)
    y = (gate_b * conv).astype(BF16)
    o_ref[...] = h + _dot(y, wout_ref[...])


def _conv_mixer(h, g, w_in, w_conv, w_out, *, tm=512):
    b, s, d = h.shape
    tok = pl.BlockSpec((None, tm, d), lambda bi, si: (bi, si, 0))
    return pl.pallas_call(
        functools.partial(_conv_mixer_kernel, tm=tm),
        grid=(b, s // tm),
        in_specs=[tok, _resident((1, d)), _resident((d, 3 * d)), _resident((SC_WIDTH, d)),
                  _resident((d, d))],
        out_specs=tok,
        out_shape=jax.ShapeDtypeStruct(h.shape, F32),
        scratch_shapes=[pltpu.VMEM((HALO + tm, d), F32)],
        compiler_params=_params(("parallel", "arbitrary")),
        name="conv_mixer",
    )(h, g[None], w_in.astype(BF16), w_conv, w_out.astype(BF16))


def _lru_kernel(h_ref, gn_ref, win_ref, cw_ref, cb_ref, wax_ref, ba_ref, bx_ref, lam_ref, wout_ref,
                o_ref, buf_ref, a_ref, u_ref, carry_ref, *, tm):
    @pl.when(pl.program_id(1) == 0)
    def _():
        buf_ref[0:HALO, :] = jnp.zeros((HALO, D_RNN), F32)
        carry_ref[...] = jnp.zeros((1, D_RNN), F32)

    h = h_ref[...]
    xn = _rms(h, gn_ref[...]).astype(BF16)
    gate = _dot(xn, win_ref[:, 0:D_RNN])
    xr = _causal_taps(buf_ref, _dot(xn, win_ref[:, D_RNN:2 * D_RNN]), cw_ref[...], tm) + cb_ref[...]

    neg_c_softplus = -LRU_C * jax.nn.softplus(-lam_ref[...])
    for n in range(N_LRU_BLOCKS):
        cols = slice(n * LRU_BLOCK, (n + 1) * LRU_BLOCK)
        xb = xr[:, cols]
        ri = _dot(xb.astype(BF16), wax_ref[n])
        r_gate = jax.nn.sigmoid(ri[:, 0:LRU_BLOCK] + ba_ref[:, cols])
        i_gate = jax.nn.sigmoid(ri[:, LRU_BLOCK:2 * LRU_BLOCK] + bx_ref[:, cols])
        log_a = r_gate * neg_c_softplus[:, cols]
        a = jnp.exp(log_a)
        a_ref[:, cols] = a
        u_ref[:, cols] = jnp.sqrt(-jnp.tanh(log_a) * (a * a + 1.0)) * (i_gate * xb)

    row = lax.broadcasted_iota(jnp.int32, (SUBLANES, D_RNN), 0)

    def group(gi, h_prev):
        rows = pl.ds(pl.multiple_of(gi * SUBLANES, SUBLANES), SUBLANES)
        a = a_ref[rows, :]
        u = u_ref[rows, :]
        for k in (1, 2, 4):
            keep = row >= k
            a_back = jnp.where(keep, pltpu.roll(a, k, 0), 1.0)
            u_back = jnp.where(keep, pltpu.roll(u, k, 0), 0.0)
            u = a * u_back + u
            a = a * a_back
        hs = a * h_prev + u
        u_ref[rows, :] = hs
        return hs[SUBLANES - 1:SUBLANES, :]

    carry_ref[...] = lax.fori_loop(0, tm // SUBLANES, group, carry_ref[...])

    y = (u_ref[...] * jax.nn.gelu(gate)).astype(BF16)
    o_ref[...] = h + _dot(y, wout_ref[...])


def _lru_mixer(h, g, w_in, conv_w, conv_b, w_a, b_a, w_x, b_x, lam, w_out, *, tm=512):
    b, s, d = h.shape
    tok = pl.BlockSpec((None, tm, d), lambda bi, si: (bi, si, 0))
    w_ax = jnp.concatenate([w_a, w_x], axis=-1).astype(BF16)
    return pl.pallas_call(
        functools.partial(_lru_kernel, tm=tm),
        grid=(b, s // tm),
        in_specs=[tok, _resident((1, d)), _resident((d, 2 * D_RNN)),
                  _resident((LRU_CONV_WIDTH, D_RNN)), _resident((1, D_RNN)),
                  _resident((N_LRU_BLOCKS, LRU_BLOCK, 2 * LRU_BLOCK)), _resident((1, D_RNN)),
                  _resident((1, D_RNN)), _resident((1, D_RNN)), _resident((D_RNN, d))],
        out_specs=tok,
        out_shape=jax.ShapeDtypeStruct(h.shape, F32),
        scratch_shapes=[pltpu.VMEM((HALO + tm, D_RNN), F32), pltpu.VMEM((tm, D_RNN), F32),
                        pltpu.VMEM((tm, D_RNN), F32), pltpu.VMEM((1, D_RNN), F32)],
        compiler_params=_params(("parallel", "arbitrary")),
        name="rglru_mixer",
    )(h, g[None], w_in.astype(BF16), conv_w, conv_b[None], w_ax, b_a[None], b_x[None], lam[None],
      w_out.astype(BF16))


def _rope_table_kernel(pos_ref, invf_ref, cos_ref, sin_ref):
    ang = pos_ref[...].astype(F32) * invf_ref[...]
    lane = lax.broadcasted_iota(jnp.int32, ang.shape, 1)
    cos_ref[...] = jnp.cos(ang)
    sin_ref[...] = jnp.where(lane < ROPE_HALF, -1.0, 1.0) * jnp.sin(ang)


def _rope_tables(positions, *, ts=1024):
    b, s = positions.shape
    inv_freq = ROPE_THETA ** (-2.0 * jnp.arange(ROPE_HALF, dtype=F32) / ROPE_DIM)
    invf = jnp.concatenate([inv_freq, inv_freq, jnp.zeros((HEAD_DIM - ROPE_DIM,), F32)])[None]
    out = jax.ShapeDtypeStruct((b, s, HEAD_DIM), F32)
    tab = pl.BlockSpec((None, ts, HEAD_DIM), lambda bi, si: (bi, si, 0))
    return pl.pallas_call(
        _rope_table_kernel,
        grid=(b, s // ts),
        in_specs=[pl.BlockSpec((None, ts, 1), lambda bi, si: (bi, si, 0)), _resident((1, HEAD_DIM))],
        out_specs=[tab, tab],
        out_shape=[out, out],
        compiler_params=_params(("parallel", "parallel")),
        name="rope_tables",
    )(positions[..., None], invf)


def _dilation_perm(tm, dil):
    p = jnp.arange(tm)
    src = (p % (tm // dil)) * dil + p // (tm // dil)
    return (src[:, None] == jnp.arange(tm)[None, :]).astype(BF16)


def _qkv_kernel(h_ref, cos_ref, sin_ref, gn_ref, w_ref, perm_ref, q_ref, k_ref, v_ref, *, dil, tm):
    sub = tm // dil
    xn = _rms(h_ref[...], gn_ref[...]).astype(BF16)
    if dil == 1:
        cos, sin = cos_ref[...], sin_ref[...]
    else:
        xn = _dot(perm_ref[...], xn).astype(BF16)
        cos = jnp.concatenate([cos_ref[pl.ds(r, sub, stride=dil), :] for r in range(dil)], axis=0)
        sin = jnp.concatenate([sin_ref[pl.ds(r, sub, stride=dil), :] for r in range(dil)], axis=0)

    def store(out_ref, cols, val):
        for r in range(dil):
            out_ref[r, :, cols] = val[r * sub:(r + 1) * sub, :]

    low_half = lax.broadcasted_iota(jnp.int32, cos.shape, 1) < ROPE_HALF
    for which, out_ref in ((0, q_ref), (1, k_ref)):
        z = _dot(xn, w_ref[:, which * D_MODEL:(which + 1) * D_MODEL])
        for hd in range(N_HEADS):
            cols = slice(hd * HEAD_DIM, (hd + 1) * HEAD_DIM)
            x = z[:, cols]
            partner = jnp.where(low_half, pltpu.roll(x, HEAD_DIM - ROPE_HALF, 1),
                                pltpu.roll(x, ROPE_HALF, 1))
            store(out_ref, cols, (x * cos + partner * sin).astype(BF16))
    store(v_ref, slice(None), _dot(xn, w_ref[:, 2 * D_MODEL:3 * D_MODEL]).astype(BF16))


def _qkv(h, cos_t, sin_t, g, w, dil, *, tm=512):
    b, s, d = h.shape
    tok = lambda width: pl.BlockSpec((None, tm, width), lambda bi, ti: (bi, ti, 0))
    out = jax.ShapeDtypeStruct((b, dil, s // dil, d), BF16)
    out_spec = pl.BlockSpec((None, dil, tm // dil, d), lambda bi, ti: (bi, 0, ti, 0))
    return pl.pallas_call(
        functools.partial(_qkv_kernel, dil=dil, tm=tm),
        grid=(b, s // tm),
        in_specs=[tok(d), tok(HEAD_DIM), tok(HEAD_DIM), _resident((1, d)), _resident((d, 3 * d)),
                  _resident((tm, tm))],
        out_specs=[out_spec] * 3,
        out_shape=[out] * 3,
        compiler_params=_params(("parallel", "parallel")),
        name=f"qkv_dil{dil}",
    )(h, cos_t, sin_t, g[None], w, _dilation_perm(tm, dil))


def _attn_kernel(q_ref, k_ref, v_ref, o_ref, lse_ref, kprev_ref, vprev_ref, *, n_qblocks):
    step = pl.program_id(1)

    @pl.when(step == 0)
    def _():
        kprev_ref[...] = jnp.zeros(kprev_ref.shape, BF16)
        vprev_ref[...] = jnp.zeros(vprev_ref.shape, BF16)

    blk = ATTN_BLOCK
    qi = lax.broadcasted_iota(jnp.int32, (blk, 2 * blk), 0)
    kj = lax.broadcasted_iota(jnp.int32, (blk, 2 * blk), 1)
    rel = blk + qi - kj
    band = (rel >= 0) & (rel <= blk)
    band_first = band & ((kj >= blk) | (step > 0))
    head_of_lane = lax.broadcasted_iota(jnp.int32, (blk, HEAD_DIM), 1) // LSE_LANES_PER_HEAD
    scale = HEAD_DIM ** -0.5

    for qb in range(n_qblocks):
        rows = slice(qb * blk, (qb + 1) * blk)
        prev_rows = slice((qb - 1) * blk, qb * blk)
        lse_slab = jnp.zeros((blk, HEAD_DIM), F32)
        for hd in range(N_HEADS):
            cols = slice(hd * HEAD_DIM, (hd + 1) * HEAD_DIM)
            if qb == 0:
                k_prev, v_prev, valid = kprev_ref[:, cols], vprev_ref[:, cols], band_first
            else:
                k_prev, v_prev, valid = k_ref[prev_rows, cols], v_ref[prev_rows, cols], band
            kk = jnp.concatenate([k_prev, k_ref[rows, cols]], axis=0)
            vv = jnp.concatenate([v_prev, v_ref[rows, cols]], axis=0)
            s = lax.dot_general(q_ref[rows, cols], kk, (((1,), (1,)), ((), ())),
                                preferred_element_type=F32) * scale
            s = jnp.where(valid, s, MASK_VALUE)
            m = jnp.max(s, axis=-1, keepdims=True)
            p = jnp.exp(s - m)
            l = jnp.sum(p, axis=-1, keepdims=True)
            o_ref[rows, cols] = (_dot(p.astype(BF16), vv) / l).astype(BF16)
            lse_slab = jnp.where(head_of_lane == hd, m + jnp.log(l), lse_slab)
        lse_ref[rows, :] = lse_slab

    last = slice((n_qblocks - 1) * blk, n_qblocks * blk)
    kprev_ref[...] = k_ref[last, :]
    vprev_ref[...] = v_ref[last, :]


def _band_attention(q, k, v, *, n_qblocks=2):
    n, sub, d = q.shape
    tq = n_qblocks * ATTN_BLOCK
    tok = lambda width: pl.BlockSpec((None, tq, width), lambda ni, ti: (ni, ti, 0))
    return pl.pallas_call(
        functools.partial(_attn_kernel, n_qblocks=n_qblocks),
        grid=(n, sub // tq),
        in_specs=[tok(d)] * 3,
        out_specs=[tok(d), tok(HEAD_DIM)],
        out_shape=[jax.ShapeDtypeStruct((n, sub, d), BF16),
                   jax.ShapeDtypeStruct((n, sub, HEAD_DIM), F32)],
        scratch_shapes=[pltpu.VMEM((ATTN_BLOCK, d), BF16), pltpu.VMEM((ATTN_BLOCK, d), BF16)],
        compiler_params=_params(("parallel", "arbitrary")),
        name=f"band_attn_len{sub}",
    )(q, k, v)


def _combine_kernel(h_ref, o0_ref, l0_ref, o1_ref, l1_ref, o2_ref, l2_ref, unperm1_ref, unperm2_ref,
                    wo_ref, out_ref, lse1_ref, lse2_ref, mix_ref, *, tm):
    def natural_lse(l_ref, nat_ref, dil):
        for r in range(dil):
            nat_ref[pl.ds(r, tm // dil, stride=dil), :] = l_ref[r]
        return nat_ref[...]

    def natural_out(o_ref, unperm_ref):
        return _dot(unperm_ref[...], o_ref[...].reshape(tm, D_MODEL))

    dil1, dil2 = DILATED_PATTERNS[1][1], DILATED_PATTERNS[2][1]
    lses = (l0_ref[0], natural_lse(l1_ref, lse1_ref, dil1), natural_lse(l2_ref, lse2_ref, dil2))
    outs = (o0_ref[0].astype(F32), natural_out(o1_ref, unperm1_ref), natural_out(o2_ref, unperm2_ref))
    m = jnp.maximum(jnp.maximum(lses[0], lses[1]), lses[2])
    es = [jnp.exp(l - m) for l in lses]
    inv = 1.0 / (es[0] + es[1] + es[2])
    ws = [e * inv for e in es]
    for hd in range(N_HEADS):
        cols = slice(hd * HEAD_DIM, (hd + 1) * HEAD_DIM)
        lane = hd * LSE_LANES_PER_HEAD
        mixed = None
        for w, o in zip(ws, outs):
            term = jnp.broadcast_to(w[:, lane:lane + 1], (tm, HEAD_DIM)) * o[:, cols]
            mixed = term if mixed is None else mixed + term
        mix_ref[:, cols] = mixed.astype(BF16)
    out_ref[...] = h_ref[...] + _dot(mix_ref[...], wo_ref[...])


def _combine(h, outs, lses, w_o, *, tm=512):
    b, s, d = h.shape
    tok = pl.BlockSpec((None, tm, d), lambda bi, ti: (bi, ti, 0))
    specs, unperms = [], []
    for (_, dil) in DILATED_PATTERNS:
        for width in (d, HEAD_DIM):
            specs.append(pl.BlockSpec((None, dil, tm // dil, width), lambda bi, ti: (bi, 0, ti, 0)))
        if dil > 1:
            unperms.append(_dilation_perm(tm, dil).T)
    args = [a for pair in zip(outs, lses) for a in pair]
    return pl.pallas_call(
        functools.partial(_combine_kernel, tm=tm),
        grid=(b, s // tm),
        in_specs=[tok] + specs + [_resident((tm, tm))] * len(unperms) + [_resident((d, d))],
        out_specs=tok,
        out_shape=jax.ShapeDtypeStruct(h.shape, F32),
        scratch_shapes=[pltpu.VMEM((tm, HEAD_DIM), F32), pltpu.VMEM((tm, HEAD_DIM), F32),
                        pltpu.VMEM((tm, d), BF16)],
        compiler_params=_params(("parallel", "parallel")),
        name="attn_combine",
    )(h, *args, *unperms, w_o.astype(BF16))


def _attention_mixer(h, positions, g, w_qkv, w_o):
    b, s, d = h.shape
    n_groups = len(DILATED_PATTERNS)
    cos_t, sin_t = _rope_tables(positions)
    w4 = w_qkv.reshape(d, 3, n_groups, d)
    outs, lses = [], []
    for gi, (window, dil) in enumerate(DILATED_PATTERNS):
        assert window // dil == ATTN_BLOCK
        w_g = w4[:, :, gi, :].reshape(d, 3 * d).astype(BF16)
        q, k, v = _qkv(h, cos_t, sin_t, g, w_g, dil)
        sub = s // dil
        o_g, l_g = _band_attention(q.reshape(b * dil, sub, d), k.reshape(b * dil, sub, d),
                                   v.reshape(b * dil, sub, d))
        outs.append(o_g.reshape(b, dil, sub, d))
        lses.append(l_g.reshape(b, dil, sub, HEAD_DIM))
    return _combine(h, outs, lses, w_o)


def kernel(x, p, positions, norm_mix, norm_mlp, norm_ple, norm_final, sc_w_in, sc_w_conv, sc_w_out,
           attn_w_qkv, attn_w_o, lru_w_in, lru_conv_w, lru_conv_b, lru_w_a, lru_b_a, lru_w_x,
           lru_b_x, lru_lambda, lru_w_out, mlp_w_up, mlp_w_down, ple_w_gate, ple_w_proj):
    h = x
    for i in range(DEPTH):
        kind, j = i % N_MIXERS, i // N_MIXERS
        if kind == 0:
            h = _conv_mixer(h, norm_mix[i], sc_w_in[j], sc_w_conv[j], sc_w_out[j])
        elif kind == 1:
            h = _attention_mixer(h, positions, norm_mix[i], attn_w_qkv[j], attn_w_o[j])
        else:
            h = _lru_mixer(h, norm_mix[i], lru_w_in[j], lru_conv_w[j], lru_conv_b[j], lru_w_a[j],
                           lru_b_a[j], lru_w_x[j], lru_b_x[j], lru_lambda[j], lru_w_out[j])
        h = _post(h, p[i], norm_mlp[i], norm_ple[i], norm_final, mlp_w_up[i], mlp_w_down[i],
                  ple_w_gate[i], ple_w_proj[i], final=(i == DEPTH - 1))
    return h
```

```python
import functools
import math

import jax
import jax.numpy as jnp
from jax import lax
from jax.experimental import pallas as pl
from jax.experimental.pallas import tpu as pltpu

D_MODEL = 1024
DEPTH = 4
N_MIXERS = 3
PLE_DIM = 256
D_FF = 4 * D_MODEL
RMS_EPS = 1e-6
SC_WIDTH = 3
HEAD_DIM = 128
N_HEADS = D_MODEL // HEAD_DIM
DILATED_PATTERNS = ((128, 1), (512, 4), (2048, 16))
ROPE_THETA = 500000.0
ROPE_DIM = HEAD_DIM // 4
ROPE_HALF = ROPE_DIM // 2
D_RNN = 1280
N_LRU_BLOCKS = 10
LRU_BLOCK = D_RNN // N_LRU_BLOCKS
LRU_CONV_WIDTH = 4
LRU_C = 8.0

ATTN_BLOCK = 128
LSE_LANES_PER_HEAD = HEAD_DIM // N_HEADS
SUBLANES = 8
HALO = SUBLANES
MASK_VALUE = -1e30
VMEM_LIMIT_BYTES = 56 * 1024 * 1024

F32 = jnp.float32
BF16 = jnp.bfloat16


def _rms(x, g):
    return x * lax.rsqrt(jnp.mean(x * x, axis=-1, keepdims=True) + RMS_EPS) * g


def _dot(a, b):
    return jnp.dot(a, b, preferred_element_type=F32)


def _resident(shape, index=None):
    index = (0,) * len(shape) if index is None else index
    return pl.BlockSpec(shape, lambda *_: index, pipeline_mode=pl.Buffered(1))


def _layer(stacked, layer):
    return _resident((None,) + stacked.shape[1:], (layer, 0, 0))


def _rows(stacked):
    return stacked[:, None, :]


def _params(semantics):
    return pltpu.CompilerParams(dimension_semantics=semantics, vmem_limit_bytes=VMEM_LIMIT_BYTES)


def _post_kernel(h_ref, p_ref, gm_ref, gp_ref, gf_ref, wup_ref, wdn_ref, wg_ref, wp_ref, o_ref,
                 *, final, tf):
    h = h_ref[...]
    xn = _rms(h, gm_ref[...]).astype(BF16)
    acc = h
    for f in range(D_FF // tf):
        u = _dot(xn, wup_ref[:, f * tf:(f + 1) * tf])
        a = jnp.square(jnp.maximum(u, 0.0)).astype(BF16)
        acc = acc + _dot(a, wdn_ref[f * tf:(f + 1) * tf, :])
    xg = _rms(acc, gp_ref[...]).astype(BF16)
    gate = jax.nn.sigmoid(_dot(xg, wg_ref[...]))
    proj = _dot(p_ref[...].astype(BF16), wp_ref[...])
    out = acc + gate * proj
    if final:
        out = _rms(out, gf_ref[...])
    o_ref[...] = out


def _post(h, p, g_mlp, g_ple, g_final, w_up, w_down, w_gate, w_proj, layer, *, tm=512, tf=512):
    b, s, d = h.shape
    tok = pl.BlockSpec((None, tm, d), lambda bi, si: (bi, si, 0))
    p_spec = pl.BlockSpec((None, None, tm, PLE_DIM), lambda bi, si: (layer, bi, si, 0))
    return pl.pallas_call(
        functools.partial(_post_kernel, final=(layer == DEPTH - 1), tf=tf),
        grid=(b, s // tm),
        in_specs=[tok, p_spec, _layer(g_mlp, layer), _layer(g_ple, layer), _resident((1, d)),
                  _layer(w_up, layer), _layer(w_down, layer), _layer(w_gate, layer),
                  _layer(w_proj, layer)],
        out_specs=tok,
        out_shape=jax.ShapeDtypeStruct(h.shape, F32),
        compiler_params=_params(("parallel", "parallel")),
        name="mlp_ple",
    )(h, p, g_mlp, g_ple, g_final, w_up, w_down, w_gate, w_proj)


def _causal_taps(buf_ref, x, w, tm):
    k_width = w.shape[0]
    buf_ref[HALO:HALO + tm, :] = x
    out = w[k_width - 1:k_width] * x
    for back in range(1, k_width):
        out = out + w[k_width - 1 - back:k_width - back] * buf_ref[HALO - back:HALO - back + tm, :]
    buf_ref[0:HALO, :] = buf_ref[tm:tm + HALO, :]
    return out


def _conv_mixer_kernel(h_ref, gn_ref, win_ref, wconv_ref, wout_ref, o_ref, buf_ref, *, tm):
    @pl.when(pl.program_id(1) == 0)
    def _():
        buf_ref[0:HALO, :] = jnp.zeros((HALO, D_MODEL), F32)

    h = h_ref[...]
    xn = _rms(h, gn_ref[...]).astype(BF16)
    gate_b = _dot(xn, win_ref[:, 0:D_MODEL])
    gate_c = _dot(xn, win_ref[:, D_MODEL:2 * D_MODEL])
    xin = _dot(xn, win_ref[:, 2 * D_MODEL:3 * D_MODEL])
    conv = _causal_taps(buf_ref, gate_c * xin, wconv_ref[...], tm)
    y = (gate_b * conv).astype(BF16)
    o_ref[...] = h + _dot(y, wout_ref[...])


def _conv_mixer(h, g, layer, w_in, w_conv, w_out, j, *, tm=512):
    b, s, d = h.shape
    tok = pl.BlockSpec((None, tm, d), lambda bi, si: (bi, si, 0))
    return pl.pallas_call(
        functools.partial(_conv_mixer_kernel, tm=tm),
        grid=(b, s // tm),
        in_specs=[tok, _layer(g, layer), _layer(w_in, j), _layer(w_conv, j), _layer(w_out, j)],
        out_specs=tok,
        out_shape=jax.ShapeDtypeStruct(h.shape, F32),
        scratch_shapes=[pltpu.VMEM((HALO + tm, d), F32)],
        compiler_params=_params(("parallel", "arbitrary")),
        name="conv_mixer",
    )(h, g, w_in, w_conv, w_out)


def _lru_kernel(h_ref, gn_ref, win_ref, cw_ref, cb_ref, wax_ref, ba_ref, bx_ref, lam_ref, wout_ref,
                o_ref, buf_ref, a_ref, u_ref, carry_ref, *, tm):
    @pl.when(pl.program_id(1) == 0)
    def _():
        buf_ref[0:HALO, :] = jnp.zeros((HALO, D_RNN), F32)
        carry_ref[...] = jnp.zeros((1, D_RNN), F32)

    h = h_ref[...]
    xn = _rms(h, gn_ref[...]).astype(BF16)
    gate = _dot(xn, win_ref[:, 0:D_RNN])
    xr = _causal_taps(buf_ref, _dot(xn, win_ref[:, D_RNN:2 * D_RNN]), cw_ref[...], tm) + cb_ref[...]

    neg_c_softplus = -LRU_C * jax.nn.softplus(-lam_ref[...])
    for n in range(N_LRU_BLOCKS):
        cols = slice(n * LRU_BLOCK, (n + 1) * LRU_BLOCK)
        xb = xr[:, cols]
        ri = _dot(xb.astype(BF16), wax_ref[n])
        r_gate = jax.nn.sigmoid(ri[:, 0:LRU_BLOCK] + ba_ref[:, cols])
        i_gate = jax.nn.sigmoid(ri[:, LRU_BLOCK:2 * LRU_BLOCK] + bx_ref[:, cols])
        log_a = r_gate * neg_c_softplus[:, cols]
        a = jnp.exp(log_a)
        a_ref[:, cols] = a
        u_ref[:, cols] = jnp.sqrt(-jnp.tanh(log_a) * (a * a + 1.0)) * (i_gate * xb)

    row = lax.broadcasted_iota(jnp.int32, (SUBLANES, D_RNN), 0)

    def group(gi, h_prev):
        rows = pl.ds(pl.multiple_of(gi * SUBLANES, SUBLANES), SUBLANES)
        a = a_ref[rows, :]
        u = u_ref[rows, :]
        for k in (1, 2, 4):
            keep = row >= k
            a_back = jnp.where(keep, pltpu.roll(a, k, 0), 1.0)
            u_back = jnp.where(keep, pltpu.roll(u, k, 0), 0.0)
            u = a * u_back + u
            a = a * a_back
        hs = a * h_prev + u
        u_ref[rows, :] = hs
        return hs[SUBLANES - 1:SUBLANES, :]

    carry_ref[...] = lax.fori_loop(0, tm // SUBLANES, group, carry_ref[...])

    y = (u_ref[...] * jax.nn.gelu(gate)).astype(BF16)
    o_ref[...] = h + _dot(y, wout_ref[...])


def _lru_mixer(h, g, layer, w_in, conv_w, conv_b, w_ax, b_a, b_x, lam, w_out, j, *, tm=512):
    b, s, d = h.shape
    tok = pl.BlockSpec((None, tm, d), lambda bi, si: (bi, si, 0))
    gates = _resident((None, N_LRU_BLOCKS, LRU_BLOCK, 2 * LRU_BLOCK), (j, 0, 0, 0))
    return pl.pallas_call(
        functools.partial(_lru_kernel, tm=tm),
        grid=(b, s // tm),
        in_specs=[tok, _layer(g, layer), _layer(w_in, j), _layer(conv_w, j), _layer(conv_b, j), gates,
                  _layer(b_a, j), _layer(b_x, j), _layer(lam, j), _layer(w_out, j)],
        out_specs=tok,
        out_shape=jax.ShapeDtypeStruct(h.shape, F32),
        scratch_shapes=[pltpu.VMEM((HALO + tm, D_RNN), F32), pltpu.VMEM((tm, D_RNN), F32),
                        pltpu.VMEM((tm, D_RNN), F32), pltpu.VMEM((1, D_RNN), F32)],
        compiler_params=_params(("parallel", "arbitrary")),
        name="rglru_mixer",
    )(h, g, w_in, conv_w, conv_b, w_ax, b_a, b_x, lam, w_out)


def _rope_table_kernel(pos_ref, invf_ref, cos_ref, sin_ref):
    ang = pos_ref[...].astype(F32) * invf_ref[...]
    lane = lax.broadcasted_iota(jnp.int32, ang.shape, 1)
    cos_ref[...] = jnp.cos(ang)
    sin_ref[...] = jnp.where(lane < ROPE_HALF, -1.0, 1.0) * jnp.sin(ang)


def _rope_tables(positions, *, ts=1024):
    b, s = positions.shape
    inv_freq = ROPE_THETA ** (-2.0 * jnp.arange(ROPE_HALF, dtype=F32) / ROPE_DIM)
    invf = jnp.concatenate([inv_freq, inv_freq, jnp.zeros((HEAD_DIM - ROPE_DIM,), F32)])[None]
    out = jax.ShapeDtypeStruct((b, s, HEAD_DIM), F32)
    tab = pl.BlockSpec((None, ts, HEAD_DIM), lambda bi, si: (bi, si, 0))
    return pl.pallas_call(
        _rope_table_kernel,
        grid=(b, s // ts),
        in_specs=[pl.BlockSpec((None, ts, 1), lambda bi, si: (bi, si, 0)), _resident((1, HEAD_DIM))],
        out_specs=[tab, tab],
        out_shape=[out, out],
        compiler_params=_params(("parallel", "parallel")),
        name="rope_tables",
    )(positions[..., None], invf)


def _dilation_perm(tm, dil):
    p = jnp.arange(tm)
    src = (p % (tm // dil)) * dil + p // (tm // dil)
    return (src[:, None] == jnp.arange(tm)[None, :]).astype(BF16)


def _qkv_kernel(h_ref, cos_ref, sin_ref, gn_ref, wq_ref, wk_ref, wv_ref, perm_ref,
                q_ref, k_ref, v_ref, *, dil, tm):
    sub = tm // dil
    xn = _rms(h_ref[...], gn_ref[...]).astype(BF16)
    if dil == 1:
        cos, sin = cos_ref[...], sin_ref[...]
    else:
        xn = _dot(perm_ref[...], xn).astype(BF16)
        cos = jnp.concatenate([cos_ref[pl.ds(r, sub, stride=dil), :] for r in range(dil)], axis=0)
        sin = jnp.concatenate([sin_ref[pl.ds(r, sub, stride=dil), :] for r in range(dil)], axis=0)

    def store(out_ref, cols, val):
        for r in range(dil):
            out_ref[r, :, cols] = val[r * sub:(r + 1) * sub, :]

    low_half = lax.broadcasted_iota(jnp.int32, cos.shape, 1) < ROPE_HALF
    for w_ref, out_ref in ((wq_ref, q_ref), (wk_ref, k_ref)):
        z = _dot(xn, w_ref[...])
        for hd in range(N_HEADS):
            cols = slice(hd * HEAD_DIM, (hd + 1) * HEAD_DIM)
            x = z[:, cols]
            partner = jnp.where(low_half, pltpu.roll(x, HEAD_DIM - ROPE_HALF, 1),
                                pltpu.roll(x, ROPE_HALF, 1))
            store(out_ref, cols, (x * cos + partner * sin).astype(BF16))
    store(v_ref, slice(None), _dot(xn, wv_ref[...]).astype(BF16))


def _qkv(h, cos_t, sin_t, g, layer, w_qkv, j, group, dil, *, tm=512):
    b, s, d = h.shape
    n_groups = len(DILATED_PATTERNS)
    tok = lambda width: pl.BlockSpec((None, tm, width), lambda bi, ti: (bi, ti, 0))
    w_spec = lambda which: _resident((None, d, d), (j, 0, which * n_groups + group))
    out = jax.ShapeDtypeStruct((b, dil, s // dil, d), BF16)
    out_spec = pl.BlockSpec((None, dil, tm // dil, d), lambda bi, ti: (bi, 0, ti, 0))
    return pl.pallas_call(
        functools.partial(_qkv_kernel, dil=dil, tm=tm),
        grid=(b, s // tm),
        in_specs=[tok(d), tok(HEAD_DIM), tok(HEAD_DIM), _layer(g, layer), w_spec(0), w_spec(1),
                  w_spec(2), _resident((tm, tm))],
        out_specs=[out_spec] * 3,
        out_shape=[out] * 3,
        compiler_params=_params(("parallel", "parallel")),
        name=f"qkv_dil{dil}",
    )(h, cos_t, sin_t, g, w_qkv, w_qkv, w_qkv, _dilation_perm(tm, dil))


def _attn_kernel(q_ref, k_ref, v_ref, o_ref, lse_ref, kbuf_ref, vbuf_ref, *, n_qblocks):
    step = pl.program_id(1)
    blk = ATTN_BLOCK
    tq = n_qblocks * blk

    @pl.when(step == 0)
    def _():
        kbuf_ref[0:blk, :] = jnp.zeros((blk, D_MODEL), BF16)
        vbuf_ref[0:blk, :] = jnp.zeros((blk, D_MODEL), BF16)

    kbuf_ref[blk:blk + tq, :] = k_ref[...]
    vbuf_ref[blk:blk + tq, :] = v_ref[...]

    qi = lax.broadcasted_iota(jnp.int32, (blk, 2 * blk), 0)
    kj = lax.broadcasted_iota(jnp.int32, (blk, 2 * blk), 1)
    rel = blk + qi - kj
    band = (rel >= 0) & (rel <= blk)
    band_first = band & ((kj >= blk) | (step > 0))
    head_of_lane = lax.broadcasted_iota(jnp.int32, (blk, HEAD_DIM), 1) // LSE_LANES_PER_HEAD
    scale2 = HEAD_DIM ** -0.5 * math.log2(math.e)

    for qb in range(n_qblocks):
        rows = slice(qb * blk, (qb + 1) * blk)
        keys = slice(qb * blk, (qb + 2) * blk)
        valid = band_first if qb == 0 else band
        lse_slab = jnp.zeros((blk, HEAD_DIM), F32)
        for hd in range(N_HEADS):
            cols = slice(hd * HEAD_DIM, (hd + 1) * HEAD_DIM)
            s2 = lax.dot_general(q_ref[rows, cols], kbuf_ref[keys, cols], (((1,), (1,)), ((), ())),
                                 preferred_element_type=F32) * scale2
            s2 = jnp.where(valid, s2, MASK_VALUE)
            m2 = jnp.max(s2, axis=-1, keepdims=True)
            p = jnp.exp2(s2 - m2)
            l = jnp.sum(p, axis=-1, keepdims=True)
            o = _dot(p.astype(BF16), vbuf_ref[keys, cols]) * (1.0 / l)
            o_ref[rows, cols] = o.astype(BF16)
            lse = (m2 + jnp.log2(l)) * math.log(2.0)
            lse_slab = jnp.where(head_of_lane == hd, lse, lse_slab)
        lse_ref[rows, :] = lse_slab

    kbuf_ref[0:blk, :] = kbuf_ref[tq:tq + blk, :]
    vbuf_ref[0:blk, :] = vbuf_ref[tq:tq + blk, :]


def _band_attention(q, k, v, *, n_qblocks=2):
    n, sub, d = q.shape
    tq = n_qblocks * ATTN_BLOCK
    tok = lambda width: pl.BlockSpec((None, tq, width), lambda ni, ti: (ni, ti, 0))
    return pl.pallas_call(
        functools.partial(_attn_kernel, n_qblocks=n_qblocks),
        grid=(n, sub // tq),
        in_specs=[tok(d)] * 3,
        out_specs=[tok(d), tok(HEAD_DIM)],
        out_shape=[jax.ShapeDtypeStruct((n, sub, d), BF16),
                   jax.ShapeDtypeStruct((n, sub, HEAD_DIM), F32)],
        scratch_shapes=[pltpu.VMEM((ATTN_BLOCK + tq, d), BF16), pltpu.VMEM((ATTN_BLOCK + tq, d), BF16)],
        compiler_params=_params(("parallel", "arbitrary")),
        name=f"band_attn_len{sub}",
    )(q, k, v)


def _combine_kernel(h_ref, o0_ref, l0_ref, o1_ref, l1_ref, o2_ref, l2_ref, unperm1_ref, unperm2_ref,
                    wo_ref, out_ref, lse1_ref, lse2_ref, mix_ref, *, tm):
    def natural_lse(l_ref, nat_ref, dil):
        for r in range(dil):
            nat_ref[pl.ds(r, tm // dil, stride=dil), :] = l_ref[r]
        return nat_ref[...]

    def natural_out(o_ref, unperm_ref):
        return _dot(unperm_ref[...], o_ref[...].reshape(tm, D_MODEL))

    dil1, dil2 = DILATED_PATTERNS[1][1], DILATED_PATTERNS[2][1]
    lses = (l0_ref[0], natural_lse(l1_ref, lse1_ref, dil1), natural_lse(l2_ref, lse2_ref, dil2))
    outs = (o0_ref[0].astype(F32), natural_out(o1_ref, unperm1_ref), natural_out(o2_ref, unperm2_ref))
    m = jnp.maximum(jnp.maximum(lses[0], lses[1]), lses[2])
    es = [jnp.exp(l - m) for l in lses]
    inv = 1.0 / (es[0] + es[1] + es[2])
    ws = [e * inv for e in es]
    for hd in range(N_HEADS):
        cols = slice(hd * HEAD_DIM, (hd + 1) * HEAD_DIM)
        lane = hd * LSE_LANES_PER_HEAD
        mixed = None
        for w, o in zip(ws, outs):
            term = jnp.broadcast_to(w[:, lane:lane + 1], (tm, HEAD_DIM)) * o[:, cols]
            mixed = term if mixed is None else mixed + term
        mix_ref[:, cols] = mixed.astype(BF16)
    out_ref[...] = h_ref[...] + _dot(mix_ref[...], wo_ref[...])


def _combine(h, outs, lses, w_o, j, *, tm=512):
    b, s, d = h.shape
    tok = pl.BlockSpec((None, tm, d), lambda bi, ti: (bi, ti, 0))
    specs, unperms = [], []
    for (_, dil) in DILATED_PATTERNS:
        for width in (d, HEAD_DIM):
            specs.append(pl.BlockSpec((None, dil, tm // dil, width), lambda bi, ti: (bi, 0, ti, 0)))
        if dil > 1:
            unperms.append(_dilation_perm(tm, dil).T)
    args = [a for pair in zip(outs, lses) for a in pair]
    return pl.pallas_call(
        functools.partial(_combine_kernel, tm=tm),
        grid=(b, s // tm),
        in_specs=[tok] + specs + [_resident((tm, tm))] * len(unperms) + [_layer(w_o, j)],
        out_specs=tok,
        out_shape=jax.ShapeDtypeStruct(h.shape, F32),
        scratch_shapes=[pltpu.VMEM((tm, HEAD_DIM), F32), pltpu.VMEM((tm, HEAD_DIM), F32),
                        pltpu.VMEM((tm, d), BF16)],
        compiler_params=_params(("parallel", "parallel")),
        name="attn_combine",
    )(h, *args, *unperms, w_o)


def _attention_mixer(h, cos_t, sin_t, g, layer, w_qkv, w_o, j):
    b, s, d = h.shape
    outs, lses = [], []
    for group, (window, dil) in enumerate(DILATED_PATTERNS):
        assert window // dil == ATTN_BLOCK
        q, k, v = _qkv(h, cos_t, sin_t, g, layer, w_qkv, j, group, dil)
        sub = s // dil
        o_g, l_g = _band_attention(q.reshape(b * dil, sub, d), k.reshape(b * dil, sub, d),
                                   v.reshape(b * dil, sub, d))
        outs.append(o_g.reshape(b, dil, sub, d))
        lses.append(l_g.reshape(b, dil, sub, HEAD_DIM))
    return _combine(h, outs, lses, w_o, j)


def kernel(x, p, positions, norm_mix, norm_mlp, norm_ple, norm_final, sc_w_in, sc_w_conv, sc_w_out,
           attn_w_qkv, attn_w_o, lru_w_in, lru_conv_w, lru_conv_b, lru_w_a, lru_b_a, lru_w_x,
           lru_b_x, lru_lambda, lru_w_out, mlp_w_up, mlp_w_down, ple_w_gate, ple_w_proj):
    bf = lambda w: w.astype(BF16)
    norm_mix, norm_mlp, norm_ple = _rows(norm_mix), _rows(norm_mlp), _rows(norm_ple)
    sc_w_in, sc_w_out = bf(sc_w_in), bf(sc_w_out)
    attn_w_qkv, attn_w_o = bf(attn_w_qkv), bf(attn_w_o)
    lru_w_in, lru_w_out = bf(lru_w_in), bf(lru_w_out)
    lru_w_ax = bf(jnp.concatenate([lru_w_a, lru_w_x], axis=-1))
    lru_conv_b, lru_b_a, lru_b_x, lru_lambda = (_rows(lru_conv_b), _rows(lru_b_a), _rows(lru_b_x),
                                                _rows(lru_lambda))
    mlp_w_up, mlp_w_down = bf(mlp_w_up), bf(mlp_w_down)
    ple_w_gate, ple_w_proj = bf(ple_w_gate), bf(ple_w_proj)
    cos_t, sin_t = _rope_tables(positions)

    h = x
    for i in range(DEPTH):
        kind, j = i % N_MIXERS, i // N_MIXERS
        if kind == 0:
            h = _conv_mixer(h, norm_mix, i, sc_w_in, sc_w_conv, sc_w_out, j)
        elif kind == 1:
            h = _attention_mixer(h, cos_t, sin_t, norm_mix, i, attn_w_qkv, attn_w_o, j)
        else:
            h = _lru_mixer(h, norm_mix, i, lru_w_in, lru_conv_w, lru_conv_b, lru_w_ax, lru_b_a,
                           lru_b_x, lru_lambda, lru_w_out, j)
        h = _post(h, p, norm_mlp, norm_ple, norm_final[None], mlp_w_up, mlp_w_down, ple_w_gate,
                  ple_w_proj, i)
    return h
```

```python
import functools
import math

import jax
import jax.numpy as jnp
from jax import lax
from jax.experimental import pallas as pl
from jax.experimental.pallas import tpu as pltpu

D_MODEL = 1024
DEPTH = 4
N_MIXERS = 3
PLE_DIM = 256
D_FF = 4 * D_MODEL
RMS_EPS = 1e-6
SC_WIDTH = 3
HEAD_DIM = 128
N_HEADS = D_MODEL // HEAD_DIM
DILATED_PATTERNS = ((128, 1), (512, 4), (2048, 16))
ROPE_THETA = 500000.0
ROPE_DIM = HEAD_DIM // 4
ROPE_HALF = ROPE_DIM // 2
D_RNN = 1280
N_LRU_BLOCKS = 10
LRU_BLOCK = D_RNN // N_LRU_BLOCKS
LRU_CONV_WIDTH = 4
LRU_C = 8.0

ATTN_BLOCK = 128
LSE_LANES_PER_HEAD = HEAD_DIM // N_HEADS
SUBLANES = 8
HALO = SUBLANES
MASK_VALUE = -1e30
VMEM_LIMIT_BYTES = 56 * 1024 * 1024

F32 = jnp.float32
BF16 = jnp.bfloat16


def _rms(x, g):
    return x * lax.rsqrt(jnp.mean(x * x, axis=-1, keepdims=True) + RMS_EPS) * g


def _dot(a, b):
    return jnp.dot(a, b, preferred_element_type=F32)


def _resident(shape, index=None):
    index = (0,) * len(shape) if index is None else index
    return pl.BlockSpec(shape, lambda *_: index, pipeline_mode=pl.Buffered(1))


def _layer(stacked, layer):
    return _resident((None,) + stacked.shape[1:], (layer, 0, 0))


def _rows(stacked):
    return stacked[:, None, :]


def _params(semantics):
    return pltpu.CompilerParams(dimension_semantics=semantics, vmem_limit_bytes=VMEM_LIMIT_BYTES)


def _post_kernel(h_ref, p_ref, gm_ref, gp_ref, gf_ref, wup_ref, wdn_ref, wg_ref, wp_ref, o_ref,
                 *, final, tf):
    h = h_ref[...]
    xn = _rms(h, gm_ref[...]).astype(BF16)
    acc = h
    for f in range(D_FF // tf):
        u = _dot(xn, wup_ref[:, f * tf:(f + 1) * tf])
        a = jnp.square(jnp.maximum(u, 0.0)).astype(BF16)
        acc = acc + _dot(a, wdn_ref[f * tf:(f + 1) * tf, :])
    xg = _rms(acc, gp_ref[...]).astype(BF16)
    gate = jax.nn.sigmoid(_dot(xg, wg_ref[...]))
    proj = _dot(p_ref[...].astype(BF16), wp_ref[...])
    out = acc + gate * proj
    if final:
        out = _rms(out, gf_ref[...])
    o_ref[...] = out


def _post(h, p, g_mlp, g_ple, g_final, w_up, w_down, w_gate, w_proj, layer, *, tm=512, tf=512):
    b, s, d = h.shape
    tok = pl.BlockSpec((None, tm, d), lambda bi, si: (bi, si, 0))
    p_spec = pl.BlockSpec((None, None, tm, PLE_DIM), lambda bi, si: (layer, bi, si, 0))
    return pl.pallas_call(
        functools.partial(_post_kernel, final=(layer == DEPTH - 1), tf=tf),
        grid=(b, s // tm),
        in_specs=[tok, p_spec, _layer(g_mlp, layer), _layer(g_ple, layer), _resident((1, d)),
                  _layer(w_up, layer), _layer(w_down, layer), _layer(w_gate, layer),
                  _layer(w_proj, layer)],
        out_specs=tok,
        out_shape=jax.ShapeDtypeStruct(h.shape, F32),
        compiler_params=_params(("parallel", "parallel")),
        name="mlp_ple",
    )(h, p, g_mlp, g_ple, g_final, w_up, w_down, w_gate, w_proj)


def _causal_taps(buf_ref, x, w, tm):
    k_width = w.shape[0]
    buf_ref[HALO:HALO + tm, :] = x
    out = w[k_width - 1:k_width] * x
    for back in range(1, k_width):
        out = out + w[k_width - 1 - back:k_width - back] * buf_ref[HALO - back:HALO - back + tm, :]
    buf_ref[0:HALO, :] = buf_ref[tm:tm + HALO, :]
    return out


def _conv_mixer_kernel(h_ref, gn_ref, win_ref, wconv_ref, wout_ref, o_ref, buf_ref, *, tm):
    @pl.when(pl.program_id(1) == 0)
    def _():
        buf_ref[0:HALO, :] = jnp.zeros((HALO, D_MODEL), F32)

    h = h_ref[...]
    xn = _rms(h, gn_ref[...]).astype(BF16)
    gate_b = _dot(xn, win_ref[:, 0:D_MODEL])
    gate_c = _dot(xn, win_ref[:, D_MODEL:2 * D_MODEL])
    xin = _dot(xn, win_ref[:, 2 * D_MODEL:3 * D_MODEL])
    conv = _causal_taps(buf_ref, gate_c * xin, wconv_ref[...], tm)
    y = (gate_b * conv).astype(BF16)
    o_ref[...] = h + _dot(y, wout_ref[...])


def _conv_mixer(h, g, layer, w_in, w_conv, w_out, j, *, tm=512):
    b, s, d = h.shape
    tok = pl.BlockSpec((None, tm, d), lambda bi, si: (bi, si, 0))
    return pl.pallas_call(
        functools.partial(_conv_mixer_kernel, tm=tm),
        grid=(b, s // tm),
        in_specs=[tok, _layer(g, layer), _layer(w_in, j), _layer(w_conv, j), _layer(w_out, j)],
        out_specs=tok,
        out_shape=jax.ShapeDtypeStruct(h.shape, F32),
        scratch_shapes=[pltpu.VMEM((HALO + tm, d), F32)],
        compiler_params=_params(("parallel", "arbitrary")),
        name="conv_mixer",
    )(h, g, w_in, w_conv, w_out)


def _lru_kernel(h_ref, gn_ref, win_ref, cw_ref, cb_ref, wax_ref, ba_ref, bx_ref, lam_ref, wout_ref,
                o_ref, buf_ref, a_ref, u_ref, carry_ref, *, tm):
    @pl.when(pl.program_id(1) == 0)
    def _():
        buf_ref[0:HALO, :] = jnp.zeros((HALO, D_RNN), F32)
        carry_ref[...] = jnp.zeros((1, D_RNN), F32)

    h = h_ref[...]
    xn = _rms(h, gn_ref[...]).astype(BF16)
    gate = _dot(xn, win_ref[:, 0:D_RNN])
    xr = _causal_taps(buf_ref, _dot(xn, win_ref[:, D_RNN:2 * D_RNN]), cw_ref[...], tm) + cb_ref[...]

    neg_c_softplus = -LRU_C * jax.nn.softplus(-lam_ref[...])
    for n in range(N_LRU_BLOCKS):
        cols = slice(n * LRU_BLOCK, (n + 1) * LRU_BLOCK)
        xb = xr[:, cols]
        ri = _dot(xb.astype(BF16), wax_ref[n])
        r_gate = jax.nn.sigmoid(ri[:, 0:LRU_BLOCK] + ba_ref[:, cols])
        i_gate = jax.nn.sigmoid(ri[:, LRU_BLOCK:2 * LRU_BLOCK] + bx_ref[:, cols])
        log_a = r_gate * neg_c_softplus[:, cols]
        a = jnp.exp(log_a)
        a_ref[:, cols] = a
        u_ref[:, cols] = jnp.sqrt(-jnp.tanh(log_a) * (a * a + 1.0)) * (i_gate * xb)

    row = lax.broadcasted_iota(jnp.int32, (SUBLANES, D_RNN), 0)

    def group(gi, h_prev):
        rows = pl.ds(pl.multiple_of(gi * SUBLANES, SUBLANES), SUBLANES)
        a = a_ref[rows, :]
        u = u_ref[rows, :]
        for k in (1, 2, 4):
            keep = row >= k
            a_back = jnp.where(keep, pltpu.roll(a, k, 0), 1.0)
            u_back = jnp.where(keep, pltpu.roll(u, k, 0), 0.0)
            u = a * u_back + u
            a = a * a_back
        hs = a * h_prev + u
        u_ref[rows, :] = hs
        return hs[SUBLANES - 1:SUBLANES, :]

    carry_ref[...] = lax.fori_loop(0, tm // SUBLANES, group, carry_ref[...])

    y = (u_ref[...] * jax.nn.gelu(gate)).astype(BF16)
    o_ref[...] = h + _dot(y, wout_ref[...])


def _lru_mixer(h, g, layer, w_in, conv_w, conv_b, w_ax, b_a, b_x, lam, w_out, j, *, tm=512):
    b, s, d = h.shape
    tok = pl.BlockSpec((None, tm, d), lambda bi, si: (bi, si, 0))
    gates = _resident((None, N_LRU_BLOCKS, LRU_BLOCK, 2 * LRU_BLOCK), (j, 0, 0, 0))
    return pl.pallas_call(
        functools.partial(_lru_kernel, tm=tm),
        grid=(b, s // tm),
        in_specs=[tok, _layer(g, layer), _layer(w_in, j), _layer(conv_w, j), _layer(conv_b, j), gates,
                  _layer(b_a, j), _layer(b_x, j), _layer(lam, j), _layer(w_out, j)],
        out_specs=tok,
        out_shape=jax.ShapeDtypeStruct(h.shape, F32),
        scratch_shapes=[pltpu.VMEM((HALO + tm, D_RNN), F32), pltpu.VMEM((tm, D_RNN), F32),
                        pltpu.VMEM((tm, D_RNN), F32), pltpu.VMEM((1, D_RNN), F32)],
        compiler_params=_params(("parallel", "arbitrary")),
        name="rglru_mixer",
    )(h, g, w_in, conv_w, conv_b, w_ax, b_a, b_x, lam, w_out)


def _rope_table_kernel(pos_ref, invf_ref, cos_ref, sin_ref):
    ang = pos_ref[...].astype(F32) * invf_ref[...]
    lane = lax.broadcasted_iota(jnp.int32, ang.shape, 1)
    cos_ref[...] = jnp.cos(ang)
    sin_ref[...] = jnp.where(lane < ROPE_HALF, -1.0, 1.0) * jnp.sin(ang)


def _rope_tables(positions, *, ts=1024):
    b, s = positions.shape
    inv_freq = ROPE_THETA ** (-2.0 * jnp.arange(ROPE_HALF, dtype=F32) / ROPE_DIM)
    invf = jnp.concatenate([inv_freq, inv_freq, jnp.zeros((HEAD_DIM - ROPE_DIM,), F32)])[None]
    out = jax.ShapeDtypeStruct((b, s, HEAD_DIM), F32)
    tab = pl.BlockSpec((None, ts, HEAD_DIM), lambda bi, si: (bi, si, 0))
    return pl.pallas_call(
        _rope_table_kernel,
        grid=(b, s // ts),
        in_specs=[pl.BlockSpec((None, ts, 1), lambda bi, si: (bi, si, 0)), _resident((1, HEAD_DIM))],
        out_specs=[tab, tab],
        out_shape=[out, out],
        compiler_params=_params(("parallel", "parallel")),
        name="rope_tables",
    )(positions[..., None], invf)


def _dilation_perm(tm, dil):
    p = jnp.arange(tm)
    src = (p % (tm // dil)) * dil + p // (tm // dil)
    return (src[:, None] == jnp.arange(tm)[None, :]).astype(BF16)


def _qkv_kernel(h_ref, cos_ref, sin_ref, gn_ref, wq_ref, wk_ref, wv_ref, perm_ref,
                q_ref, k_ref, v_ref, *, dil, tm):
    sub = tm // dil
    xn = _rms(h_ref[...], gn_ref[...]).astype(BF16)
    if dil == 1:
        cos, sin = cos_ref[...], sin_ref[...]
    else:
        xn = _dot(perm_ref[...], xn).astype(BF16)
        cos = jnp.concatenate([cos_ref[pl.ds(r, sub, stride=dil), :] for r in range(dil)], axis=0)
        sin = jnp.concatenate([sin_ref[pl.ds(r, sub, stride=dil), :] for r in range(dil)], axis=0)

    def store(out_ref, cols, val):
        for r in range(dil):
            out_ref[r, :, cols] = val[r * sub:(r + 1) * sub, :]

    low_half = lax.broadcasted_iota(jnp.int32, cos.shape, 1) < ROPE_HALF
    for w_ref, out_ref in ((wq_ref, q_ref), (wk_ref, k_ref)):
        z = _dot(xn, w_ref[...])
        for hd in range(N_HEADS):
            cols = slice(hd * HEAD_DIM, (hd + 1) * HEAD_DIM)
            x = z[:, cols]
            partner = jnp.where(low_half, pltpu.roll(x, HEAD_DIM - ROPE_HALF, 1),
                                pltpu.roll(x, ROPE_HALF, 1))
            store(out_ref, cols, (x * cos + partner * sin).astype(BF16))
    store(v_ref, slice(None), _dot(xn, wv_ref[...]).astype(BF16))


def _qkv(h, cos_t, sin_t, g, layer, w_qkv, j, group, dil, *, tm=512):
    b, s, d = h.shape
    n_groups = len(DILATED_PATTERNS)
    tok = lambda width: pl.BlockSpec((None, tm, width), lambda bi, ti: (bi, ti, 0))
    w_spec = lambda which: _resident((None, d, d), (j, 0, which * n_groups + group))
    out = jax.ShapeDtypeStruct((b, dil, s // dil, d), BF16)
    out_spec = pl.BlockSpec((None, dil, tm // dil, d), lambda bi, ti: (bi, 0, ti, 0))
    return pl.pallas_call(
        functools.partial(_qkv_kernel, dil=dil, tm=tm),
        grid=(b, s // tm),
        in_specs=[tok(d), tok(HEAD_DIM), tok(HEAD_DIM), _layer(g, layer), w_spec(0), w_spec(1),
                  w_spec(2), _resident((tm, tm))],
        out_specs=[out_spec] * 3,
        out_shape=[out] * 3,
        compiler_params=_params(("parallel", "parallel")),
        name=f"qkv_dil{dil}",
    )(h, cos_t, sin_t, g, w_qkv, w_qkv, w_qkv, _dilation_perm(tm, dil))


def _attn_kernel(q_ref, k_ref, v_ref, o_ref, lse_ref, kbuf_ref, vbuf_ref, m_ref, l_ref,
                 *, n_seq, n_qblocks, whole_sequence):
    step = pl.program_id(1)
    blk = ATTN_BLOCK
    tq = n_qblocks * blk
    vcols = lambda hd: slice(2 * hd * HEAD_DIM, (2 * hd + 1) * HEAD_DIM)

    @pl.when((pl.program_id(0) == 0) & (step == 0))
    def _():
        kbuf_ref[...] = jnp.zeros(kbuf_ref.shape, BF16)
        vbuf_ref[...] = jnp.ones(vbuf_ref.shape, BF16)

    qi = lax.broadcasted_iota(jnp.int32, (blk, 2 * blk), 0)
    kj = lax.broadcasted_iota(jnp.int32, (blk, 2 * blk), 1)
    rel = blk + qi - kj
    band = (rel >= 0) & (rel <= blk)
    after_start = False if whole_sequence else step > 0
    band_first = band & ((kj >= blk) | after_start)
    scale2 = HEAD_DIM ** -0.5 * math.log2(math.e)

    for sq in range(n_seq):
        kbuf_ref[sq, blk:blk + tq, :] = k_ref[sq]
        for hd in range(N_HEADS):
            vbuf_ref[sq, blk:blk + tq, vcols(hd)] = v_ref[sq, :, hd * HEAD_DIM:(hd + 1) * HEAD_DIM]
        for qb in range(n_qblocks):
            rows = slice(qb * blk, (qb + 1) * blk)
            keys = slice(qb * blk, (qb + 2) * blk)
            valid = band_first if qb == 0 else band
            for hd in range(N_HEADS):
                cols = slice(hd * HEAD_DIM, (hd + 1) * HEAD_DIM)
                stat = slice(hd * LSE_LANES_PER_HEAD, (hd + 1) * LSE_LANES_PER_HEAD)
                s2 = lax.dot_general(q_ref[sq, rows, cols], kbuf_ref[sq, keys, cols],
                                     (((1,), (1,)), ((), ())), preferred_element_type=F32) * scale2
                s2 = jnp.where(valid, s2, MASK_VALUE)
                m2 = jnp.max(s2, axis=-1, keepdims=True)
                p = jnp.exp2(s2 - m2)
                ol = _dot(p.astype(BF16),
                          vbuf_ref[sq, keys, 2 * hd * HEAD_DIM:2 * (hd + 1) * HEAD_DIM])
                l = ol[:, HEAD_DIM:]
                o_ref[sq, rows, cols] = (ol[:, :HEAD_DIM] * (1.0 / l)).astype(BF16)
                m_ref[sq, rows, stat] = jnp.broadcast_to(m2, (blk, LSE_LANES_PER_HEAD))
                l_ref[sq, rows, stat] = l[:, stat]
        if not whole_sequence:
            kbuf_ref[sq, 0:blk, :] = kbuf_ref[sq, tq:tq + blk, :]
            vbuf_ref[sq, 0:blk, :] = vbuf_ref[sq, tq:tq + blk, :]
    lse_ref[...] = (m_ref[...] + jnp.log2(l_ref[...])) * math.log(2.0)


def _band_attention(q, k, v, *, units_per_step=64):
    n, sub, d = q.shape
    blocks_per_step = units_per_step // N_HEADS
    n_qblocks = min(blocks_per_step, sub // ATTN_BLOCK)
    n_seq = blocks_per_step // n_qblocks
    tq = n_qblocks * ATTN_BLOCK
    tok = lambda width: pl.BlockSpec((n_seq, tq, width), lambda ni, ti: (ni, ti, 0))
    return pl.pallas_call(
        functools.partial(_attn_kernel, n_seq=n_seq, n_qblocks=n_qblocks, whole_sequence=(tq == sub)),
        grid=(n // n_seq, sub // tq),
        in_specs=[tok(d)] * 3,
        out_specs=[tok(d), tok(HEAD_DIM)],
        out_shape=[jax.ShapeDtypeStruct((n, sub, d), BF16),
                   jax.ShapeDtypeStruct((n, sub, HEAD_DIM), F32)],
        scratch_shapes=[pltpu.VMEM((n_seq, ATTN_BLOCK + tq, d), BF16),
                        pltpu.VMEM((n_seq, ATTN_BLOCK + tq, 2 * d), BF16),
                        pltpu.VMEM((n_seq, tq, HEAD_DIM), F32), pltpu.VMEM((n_seq, tq, HEAD_DIM), F32)],
        compiler_params=_params(("arbitrary", "arbitrary")),
        name=f"band_attn_len{sub}",
    )(q, k, v)


def _combine_kernel(h_ref, o0_ref, l0_ref, o1_ref, l1_ref, o2_ref, l2_ref, unperm1_ref, unperm2_ref,
                    wo_ref, out_ref, lse1_ref, lse2_ref, mix_ref, *, tm):
    def natural_lse(l_ref, nat_ref, dil):
        for r in range(dil):
            nat_ref[pl.ds(r, tm // dil, stride=dil), :] = l_ref[r]
        return nat_ref[...]

    def natural_out(o_ref, unperm_ref):
        return _dot(unperm_ref[...], o_ref[...].reshape(tm, D_MODEL))

    dil1, dil2 = DILATED_PATTERNS[1][1], DILATED_PATTERNS[2][1]
    lses = (l0_ref[0], natural_lse(l1_ref, lse1_ref, dil1), natural_lse(l2_ref, lse2_ref, dil2))
    outs = (o0_ref[0].astype(F32), natural_out(o1_ref, unperm1_ref), natural_out(o2_ref, unperm2_ref))
    m = jnp.maximum(jnp.maximum(lses[0], lses[1]), lses[2])
    es = [jnp.exp(l - m) for l in lses]
    inv = 1.0 / (es[0] + es[1] + es[2])
    ws = [e * inv for e in es]
    for hd in range(N_HEADS):
        cols = slice(hd * HEAD_DIM, (hd + 1) * HEAD_DIM)
        lane = hd * LSE_LANES_PER_HEAD
        mixed = None
        for w, o in zip(ws, outs):
            term = jnp.broadcast_to(w[:, lane:lane + 1], (tm, HEAD_DIM)) * o[:, cols]
            mixed = term if mixed is None else mixed + term
        mix_ref[:, cols] = mixed.astype(BF16)
    out_ref[...] = h_ref[...] + _dot(mix_ref[...], wo_ref[...])


def _combine(h, outs, lses, w_o, j, *, tm=512):
    b, s, d = h.shape
    tok = pl.BlockSpec((None, tm, d), lambda bi, ti: (bi, ti, 0))
    specs, unperms = [], []
    for (_, dil) in DILATED_PATTERNS:
        for width in (d, HEAD_DIM):
            specs.append(pl.BlockSpec((None, dil, tm // dil, width), lambda bi, ti: (bi, 0, ti, 0)))
        if dil > 1:
            unperms.append(_dilation_perm(tm, dil).T)
    args = [a for pair in zip(outs, lses) for a in pair]
    return pl.pallas_call(
        functools.partial(_combine_kernel, tm=tm),
        grid=(b, s // tm),
        in_specs=[tok] + specs + [_resident((tm, tm))] * len(unperms) + [_layer(w_o, j)],
        out_specs=tok,
        out_shape=jax.ShapeDtypeStruct(h.shape, F32),
        scratch_shapes=[pltpu.VMEM((tm, HEAD_DIM), F32), pltpu.VMEM((tm, HEAD_DIM), F32),
                        pltpu.VMEM((tm, d), BF16)],
        compiler_params=_params(("parallel", "parallel")),
        name="attn_combine",
    )(h, *args, *unperms, w_o)


def _attention_mixer(h, cos_t, sin_t, g, layer, w_qkv, w_o, j):
    b, s, d = h.shape
    outs, lses = [], []
    for group, (window, dil) in enumerate(DILATED_PATTERNS):
        assert window // dil == ATTN_BLOCK
        q, k, v = _qkv(h, cos_t, sin_t, g, layer, w_qkv, j, group, dil)
        sub = s // dil
        o_g, l_g = _band_attention(q.reshape(b * dil, sub, d), k.reshape(b * dil, sub, d),
                                   v.reshape(b * dil, sub, d))
        outs.append(o_g.reshape(b, dil, sub, d))
        lses.append(l_g.reshape(b, dil, sub, HEAD_DIM))
    return _combine(h, outs, lses, w_o, j)


def kernel(x, p, positions, norm_mix, norm_mlp, norm_ple, norm_final, sc_w_in, sc_w_conv, sc_w_out,
           attn_w_qkv, attn_w_o, lru_w_in, lru_conv_w, lru_conv_b, lru_w_a, lru_b_a, lru_w_x,
           lru_b_x, lru_lambda, lru_w_out, mlp_w_up, mlp_w_down, ple_w_gate, ple_w_proj):
    bf = lambda w: w.astype(BF16)
    norm_mix, norm_mlp, norm_ple = _rows(norm_mix), _rows(norm_mlp), _rows(norm_ple)
    sc_w_in, sc_w_out = bf(sc_w_in), bf(sc_w_out)
    attn_w_qkv, attn_w_o = bf(attn_w_qkv), bf(attn_w_o)
    lru_w_in, lru_w_out = bf(lru_w_in), bf(lru_w_out)
    lru_w_ax = bf(jnp.concatenate([lru_w_a, lru_w_x], axis=-1))
    lru_conv_b, lru_b_a, lru_b_x, lru_lambda = (_rows(lru_conv_b), _rows(lru_b_a), _rows(lru_b_x),
                                                _rows(lru_lambda))
    mlp_w_up, mlp_w_down = bf(mlp_w_up), bf(mlp_w_down)
    ple_w_gate, ple_w_proj = bf(ple_w_gate), bf(ple_w_proj)
    cos_t, sin_t = _rope_tables(positions)

    h = x
    for i in range(DEPTH):
        kind, j = i % N_MIXERS, i // N_MIXERS
        if kind == 0:
            h = _conv_mixer(h, norm_mix, i, sc_w_in, sc_w_conv, sc_w_out, j)
        elif kind == 1:
            h = _attention_mixer(h, cos_t, sin_t, norm_mix, i, attn_w_qkv, attn_w_o, j)
        else:
            h = _lru_mixer(h, norm_mix, i, lru_w_in, lru_conv_w, lru_conv_b, lru_w_ax, lru_b_a,
                           lru_b_x, lru_lambda, lru_w_out, j)
        h = _post(h, p, norm_mlp, norm_ple, norm_final[None], mlp_w_up, mlp_w_down, ple_w_gate,
                  ple_w_proj, i)
    return h
```

```python
import functools
import math

import jax
import jax.numpy as jnp
from jax import lax
from jax.experimental import pallas as pl
from jax.experimental.pallas import tpu as pltpu

D_MODEL = 1024
DEPTH = 4
N_MIXERS = 3
PLE_DIM = 256
D_FF = 4 * D_MODEL
RMS_EPS = 1e-6
SC_WIDTH = 3
HEAD_DIM = 128
N_HEADS = D_MODEL // HEAD_DIM
DILATED_PATTERNS = ((128, 1), (512, 4), (2048, 16))
ROPE_THETA = 500000.0
ROPE_DIM = HEAD_DIM // 4
ROPE_HALF = ROPE_DIM // 2
D_RNN = 1280
N_LRU_BLOCKS = 10
LRU_BLOCK = D_RNN // N_LRU_BLOCKS
LRU_CONV_WIDTH = 4
LRU_C = 8.0

ATTN_BLOCK = 128
LSE_LANES_PER_HEAD = HEAD_DIM // N_HEADS
SUBLANES = 8
HALO = SUBLANES
MASK_VALUE = -1e30
VMEM_LIMIT_BYTES = 56 * 1024 * 1024

F32 = jnp.float32
BF16 = jnp.bfloat16


def _rms(x, g):
    return x * lax.rsqrt(jnp.mean(x * x, axis=-1, keepdims=True) + RMS_EPS) * g


def _dot(a, b):
    return jnp.dot(a, b, preferred_element_type=F32)


def _resident(shape, index=None):
    index = (0,) * len(shape) if index is None else index
    return pl.BlockSpec(shape, lambda *_: index, pipeline_mode=pl.Buffered(1))


def _layer(stacked, layer):
    return _resident((None,) + stacked.shape[1:], (layer, 0, 0))


def _rows(stacked):
    return stacked[:, None, :]


def _params(semantics):
    return pltpu.CompilerParams(dimension_semantics=semantics, vmem_limit_bytes=VMEM_LIMIT_BYTES)


def _post_kernel(h_ref, p_ref, gm_ref, gp_ref, gf_ref, wup_ref, wdn_ref, wg_ref, wp_ref, o_ref,
                 *, final, tf):
    h = h_ref[...]
    xn = _rms(h, gm_ref[...]).astype(BF16)
    acc = h
    for f in range(D_FF // tf):
        u = _dot(xn, wup_ref[:, f * tf:(f + 1) * tf])
        a = jnp.square(jnp.maximum(u, 0.0)).astype(BF16)
        acc = acc + _dot(a, wdn_ref[f * tf:(f + 1) * tf, :])
    xg = _rms(acc, gp_ref[...]).astype(BF16)
    gate = jax.nn.sigmoid(_dot(xg, wg_ref[...]))
    proj = _dot(p_ref[...].astype(BF16), wp_ref[...])
    out = acc + gate * proj
    if final:
        out = _rms(out, gf_ref[...])
    o_ref[...] = out


def _post(h, p, g_mlp, g_ple, g_final, w_up, w_down, w_gate, w_proj, layer, *, tm=512, tf=512):
    b, s, d = h.shape
    tok = pl.BlockSpec((None, tm, d), lambda bi, si: (bi, si, 0))
    p_spec = pl.BlockSpec((None, None, tm, PLE_DIM), lambda bi, si: (layer, bi, si, 0))
    return pl.pallas_call(
        functools.partial(_post_kernel, final=(layer == DEPTH - 1), tf=tf),
        grid=(b, s // tm),
        in_specs=[tok, p_spec, _layer(g_mlp, layer), _layer(g_ple, layer), _resident((1, d)),
                  _layer(w_up, layer), _layer(w_down, layer), _layer(w_gate, layer),
                  _layer(w_proj, layer)],
        out_specs=tok,
        out_shape=jax.ShapeDtypeStruct(h.shape, F32),
        compiler_params=_params(("parallel", "parallel")),
        name="mlp_ple",
    )(h, p, g_mlp, g_ple, g_final, w_up, w_down, w_gate, w_proj)


def _causal_taps(buf_ref, x, w, tm):
    k_width = w.shape[0]
    buf_ref[HALO:HALO + tm, :] = x
    out = w[k_width - 1:k_width] * x
    for back in range(1, k_width):
        out = out + w[k_width - 1 - back:k_width - back] * buf_ref[HALO - back:HALO - back + tm, :]
    buf_ref[0:HALO, :] = buf_ref[tm:tm + HALO, :]
    return out


def _conv_mixer_kernel(h_ref, gn_ref, win_ref, wconv_ref, wout_ref, o_ref, buf_ref, *, tm):
    @pl.when(pl.program_id(1) == 0)
    def _():
        buf_ref[0:HALO, :] = jnp.zeros((HALO, D_MODEL), F32)

    h = h_ref[...]
    xn = _rms(h, gn_ref[...]).astype(BF16)
    gate_b = _dot(xn, win_ref[:, 0:D_MODEL])
    gate_c = _dot(xn, win_ref[:, D_MODEL:2 * D_MODEL])
    xin = _dot(xn, win_ref[:, 2 * D_MODEL:3 * D_MODEL])
    conv = _causal_taps(buf_ref, gate_c * xin, wconv_ref[...], tm)
    y = (gate_b * conv).astype(BF16)
    o_ref[...] = h + _dot(y, wout_ref[...])


def _conv_mixer(h, g, layer, w_in, w_conv, w_out, j, *, tm=512):
    b, s, d = h.shape
    tok = pl.BlockSpec((None, tm, d), lambda bi, si: (bi, si, 0))
    return pl.pallas_call(
        functools.partial(_conv_mixer_kernel, tm=tm),
        grid=(b, s // tm),
        in_specs=[tok, _layer(g, layer), _layer(w_in, j), _layer(w_conv, j), _layer(w_out, j)],
        out_specs=tok,
        out_shape=jax.ShapeDtypeStruct(h.shape, F32),
        scratch_shapes=[pltpu.VMEM((HALO + tm, d), F32)],
        compiler_params=_params(("parallel", "arbitrary")),
        name="conv_mixer",
    )(h, g, w_in, w_conv, w_out)


def _sublane_scan(a, u, row):
    for k in (1, 2, 4):
        keep = row >= k
        a_back = jnp.where(keep, pltpu.roll(a, k, 0), 1.0)
        u_back = jnp.where(keep, pltpu.roll(u, k, 0), 0.0)
        u = a * u_back + u
        a = a * a_back
    return a, u


def _lru_kernel(h_ref, gn_ref, win_ref, cw_ref, cb_ref, wax_ref, ba_ref, bx_ref, lam_ref, wout_ref,
                o_ref, slab_ref, xs_ref, tail_ref, carry_ref, y_ref, *, tm):
    chunk = tm // SUBLANES
    pitch = chunk + SUBLANES
    n_back = LRU_CONV_WIDTH - 1
    lead = n_back * SUBLANES
    vreg = lambda j: slice(j * SUBLANES, (j + 1) * SUBLANES)

    @pl.when(pl.program_id(1) == 0)
    def _():
        tail_ref[...] = jnp.zeros(tail_ref.shape, F32)
        carry_ref[...] = jnp.zeros(carry_ref.shape, F32)

    h = h_ref[...]
    xn = _rms(h, gn_ref[...]).astype(BF16)
    taps = cw_ref[...]
    bias = cb_ref[...]
    half_log_decay = -0.5 * LRU_C * jax.nn.softplus(-lam_ref[...])
    h_prev = carry_ref[...]
    row = lax.broadcasted_iota(jnp.int32, (SUBLANES, LRU_BLOCK), 0)
    block_cols = lambda n: slice(n * LRU_BLOCK, (n + 1) * LRU_BLOCK)

    def conv_and_gate_matmul(n, x_nat):
        cols = block_cols(n)
        for s in range(SUBLANES):
            slab_ref[n, s * pitch:s * pitch + chunk, :] = x_nat[s * chunk:(s + 1) * chunk, :]
        for j in range(chunk):
            xs_ref[n, lead + j * SUBLANES:lead + (j + 1) * SUBLANES, :] = (
                slab_ref[n, pl.ds(j, SUBLANES, stride=pitch), :])
        for k in range(1, n_back + 1):
            cur = xs_ref[n, lead + (chunk - k) * SUBLANES:lead + (chunk - k + 1) * SUBLANES, :]
            prev = tail_ref[n, vreg(n_back - k), :]
            xs_ref[n, lead - k * SUBLANES:lead - (k - 1) * SUBLANES, :] = pltpu.roll(
                jnp.where(row == SUBLANES - 1, prev, cur), 1, 0)
        tail_ref[n] = xs_ref[n, lead + (chunk - n_back) * SUBLANES:lead + chunk * SUBLANES, :]
        xr = bias[:, cols]
        for k in range(LRU_CONV_WIDTH):
            xr = xr + taps[n_back - k:n_back - k + 1, cols] * xs_ref[
                n, lead - k * SUBLANES:lead - k * SUBLANES + tm, :]
        return xr, _dot(xr.astype(BF16), wax_ref[n])

    def recurrence(n, xr, ri, gate):
        cols = block_cols(n)
        t_r = jnp.tanh(ri[:, 0:LRU_BLOCK] + ba_ref[:, cols])
        t_i = jnp.tanh(ri[:, LRU_BLOCK:2 * LRU_BLOCK] + bx_ref[:, cols])
        log_a = (t_r + 1.0) * half_log_decay[:, cols]
        a = jnp.exp(log_a)
        z = -jnp.tanh(log_a) * (a * a + 1.0)
        u = (0.5 * jnp.exp2(0.5 * jnp.log2(z))) * ((t_i + 1.0) * xr)
        h_loc = jnp.zeros((SUBLANES, LRU_BLOCK), F32)
        a_tot = jnp.ones((SUBLANES, LRU_BLOCK), F32)
        for j in range(chunk):
            h_loc = a[vreg(j)] * h_loc + u[vreg(j)]
            a_tot = a_tot * a[vreg(j)]
        a_cum, h_cum = _sublane_scan(a_tot, h_loc, row)
        ends = a_cum * h_prev[:, cols] + h_cum
        h_cur = jnp.where(row == 0, h_prev[:, cols], pltpu.roll(ends, 1, 0))
        for j in range(chunk):
            h_cur = a[vreg(j)] * h_cur + u[vreg(j)]
            slab_ref[n, pl.ds(j, SUBLANES, stride=pitch), :] = h_cur
        for s in range(SUBLANES):
            rows = slice(s * chunk, (s + 1) * chunk)
            hs = slab_ref[n, s * pitch:s * pitch + chunk, :]
            y_ref[rows, cols] = (hs * jax.nn.gelu(gate[rows, :])).astype(BF16)
        return ends[SUBLANES - 1:SUBLANES, :]

    pair_width = 2 * LRU_BLOCK
    n_pairs = N_LRU_BLOCKS // 2
    in_proj = lambda lo: _dot(xn, win_ref[:, lo:lo + pair_width])
    out_proj = lambda p: _dot(y_ref[:, p * pair_width:(p + 1) * pair_width],
                              wout_ref[p * pair_width:(p + 1) * pair_width, :])
    x_pairs = [in_proj(D_RNN), in_proj(D_RNN + pair_width)]
    out = h
    carries = []
    for pair in range(n_pairs):
        blocks = (2 * pair, 2 * pair + 1)
        halves = (slice(0, LRU_BLOCK), slice(LRU_BLOCK, pair_width))
        fronts = [conv_and_gate_matmul(n, x_pairs[pair][:, half]) for n, half in zip(blocks, halves)]
        if pair > 0:
            out = out + out_proj(pair - 1)
        if pair + 2 < n_pairs:
            x_pairs.append(in_proj(D_RNN + (pair + 2) * pair_width))
        gate = in_proj(pair * pair_width)
        for n, half, (xr, ri) in zip(blocks, halves, fronts):
            carries.append(recurrence(n, xr, ri, gate[:, half]))
    out = out + out_proj(n_pairs - 1)

    carry_ref[...] = jnp.concatenate(carries, axis=1)
    o_ref[...] = out


def _lru_mixer(h, g, layer, w_in, conv_w, conv_b, w_ax, b_a, b_x, lam, w_out, j, *, tm=512):
    b, s, d = h.shape
    tok = pl.BlockSpec((None, tm, d), lambda bi, si: (bi, si, 0))
    gates = _resident((None, N_LRU_BLOCKS, LRU_BLOCK, 2 * LRU_BLOCK), (j, 0, 0, 0))
    lead = (LRU_CONV_WIDTH - 1) * SUBLANES
    slab_rows = SUBLANES * (tm // SUBLANES + SUBLANES)
    return pl.pallas_call(
        functools.partial(_lru_kernel, tm=tm),
        grid=(b, s // tm),
        in_specs=[tok, _layer(g, layer), _layer(w_in, j), _layer(conv_w, j), _layer(conv_b, j), gates,
                  _layer(b_a, j), _layer(b_x, j), _layer(lam, j), _layer(w_out, j)],
        out_specs=tok,
        out_shape=jax.ShapeDtypeStruct(h.shape, F32),
        scratch_shapes=[pltpu.VMEM((N_LRU_BLOCKS, slab_rows, LRU_BLOCK), F32),
                        pltpu.VMEM((N_LRU_BLOCKS, lead + tm, LRU_BLOCK), F32),
                        pltpu.VMEM((N_LRU_BLOCKS, lead, LRU_BLOCK), F32),
                        pltpu.VMEM((1, D_RNN), F32), pltpu.VMEM((tm, D_RNN), BF16)],
        compiler_params=_params(("parallel", "arbitrary")),
        name="rglru_mixer",
    )(h, g, w_in, conv_w, conv_b, w_ax, b_a, b_x, lam, w_out)


def _rope_table_kernel(pos_ref, invf_ref, cos_ref, sin_ref):
    ang = pos_ref[...].astype(F32) * invf_ref[...]
    lane = lax.broadcasted_iota(jnp.int32, ang.shape, 1)
    cos_ref[...] = jnp.cos(ang)
    sin_ref[...] = jnp.where(lane < ROPE_HALF, -1.0, 1.0) * jnp.sin(ang)


def _rope_tables(positions, *, ts=1024):
    b, s = positions.shape
    inv_freq = ROPE_THETA ** (-2.0 * jnp.arange(ROPE_HALF, dtype=F32) / ROPE_DIM)
    invf = jnp.concatenate([inv_freq, inv_freq, jnp.zeros((HEAD_DIM - ROPE_DIM,), F32)])[None]
    out = jax.ShapeDtypeStruct((b, s, HEAD_DIM), F32)
    tab = pl.BlockSpec((None, ts, HEAD_DIM), lambda bi, si: (bi, si, 0))
    return pl.pallas_call(
        _rope_table_kernel,
        grid=(b, s // ts),
        in_specs=[pl.BlockSpec((None, ts, 1), lambda bi, si: (bi, si, 0)), _resident((1, HEAD_DIM))],
        out_specs=[tab, tab],
        out_shape=[out, out],
        compiler_params=_params(("parallel", "parallel")),
        name="rope_tables",
    )(positions[..., None], invf)


def _dilation_perm(tm, dil):
    p = jnp.arange(tm)
    src = (p % (tm // dil)) * dil + p // (tm // dil)
    return (src[:, None] == jnp.arange(tm)[None, :]).astype(BF16)


def _qkv_kernel(h_ref, cos_ref, sin_ref, gn_ref, wq_ref, wk_ref, wv_ref, perm_ref,
                q_ref, k_ref, v_ref, *, dil, tm):
    sub = tm // dil
    xn = _rms(h_ref[...], gn_ref[...]).astype(BF16)
    if dil == 1:
        cos, sin = cos_ref[...], sin_ref[...]
    else:
        xn = _dot(perm_ref[...], xn).astype(BF16)
        cos = jnp.concatenate([cos_ref[pl.ds(r, sub, stride=dil), :] for r in range(dil)], axis=0)
        sin = jnp.concatenate([sin_ref[pl.ds(r, sub, stride=dil), :] for r in range(dil)], axis=0)

    def store(out_ref, cols, val):
        for r in range(dil):
            out_ref[r, :, cols] = val[r * sub:(r + 1) * sub, :]

    low_half = lax.broadcasted_iota(jnp.int32, cos.shape, 1) < ROPE_HALF
    for w_ref, out_ref in ((wq_ref, q_ref), (wk_ref, k_ref)):
        z = _dot(xn, w_ref[...])
        for hd in range(N_HEADS):
            cols = slice(hd * HEAD_DIM, (hd + 1) * HEAD_DIM)
            x = z[:, cols]
            partner = jnp.where(low_half, pltpu.roll(x, HEAD_DIM - ROPE_HALF, 1),
                                pltpu.roll(x, ROPE_HALF, 1))
            store(out_ref, cols, (x * cos + partner * sin).astype(BF16))
    store(v_ref, slice(None), _dot(xn, wv_ref[...]).astype(BF16))


def _qkv(h, cos_t, sin_t, g, layer, w_qkv, j, group, dil, *, tm=512):
    b, s, d = h.shape
    n_groups = len(DILATED_PATTERNS)
    tok = lambda width: pl.BlockSpec((None, tm, width), lambda bi, ti: (bi, ti, 0))
    w_spec = lambda which: _resident((None, d, d), (j, 0, which * n_groups + group))
    out = jax.ShapeDtypeStruct((b, dil, s // dil, d), BF16)
    out_spec = pl.BlockSpec((None, dil, tm // dil, d), lambda bi, ti: (bi, 0, ti, 0))
    return pl.pallas_call(
        functools.partial(_qkv_kernel, dil=dil, tm=tm),
        grid=(b, s // tm),
        in_specs=[tok(d), tok(HEAD_DIM), tok(HEAD_DIM), _layer(g, layer), w_spec(0), w_spec(1),
                  w_spec(2), _resident((tm, tm))],
        out_specs=[out_spec] * 3,
        out_shape=[out] * 3,
        compiler_params=_params(("parallel", "parallel")),
        name=f"qkv_dil{dil}",
    )(h, cos_t, sin_t, g, w_qkv, w_qkv, w_qkv, _dilation_perm(tm, dil))


def _attn_kernel(q_ref, k_ref, v_ref, o_ref, lse_ref, kbuf_ref, vbuf_ref, m_ref, l_ref,
                 *, n_seq, n_qblocks, whole_sequence):
    step = pl.program_id(1)
    blk = ATTN_BLOCK
    tq = n_qblocks * blk
    vcols = lambda hd: slice(2 * hd * HEAD_DIM, (2 * hd + 1) * HEAD_DIM)

    @pl.when((pl.program_id(0) == 0) & (step == 0))
    def _():
        kbuf_ref[...] = jnp.zeros(kbuf_ref.shape, BF16)
        vbuf_ref[...] = jnp.ones(vbuf_ref.shape, BF16)

    qi = lax.broadcasted_iota(jnp.int32, (blk, 2 * blk), 0)
    kj = lax.broadcasted_iota(jnp.int32, (blk, 2 * blk), 1)
    rel = blk + qi - kj
    band = (rel >= 0) & (rel <= blk)
    after_start = False if whole_sequence else step > 0
    band_first = band & ((kj >= blk) | after_start)
    scale2 = HEAD_DIM ** -0.5 * math.log2(math.e)

    for sq in range(n_seq):
        kbuf_ref[sq, blk:blk + tq, :] = k_ref[sq]
        for hd in range(N_HEADS):
            vbuf_ref[sq, blk:blk + tq, vcols(hd)] = v_ref[sq, :, hd * HEAD_DIM:(hd + 1) * HEAD_DIM]
        for qb in range(n_qblocks):
            rows = slice(qb * blk, (qb + 1) * blk)
            keys = slice(qb * blk, (qb + 2) * blk)
            valid = band_first if qb == 0 else band
            for hd in range(N_HEADS):
                cols = slice(hd * HEAD_DIM, (hd + 1) * HEAD_DIM)
                stat = slice(hd * LSE_LANES_PER_HEAD, (hd + 1) * LSE_LANES_PER_HEAD)
                s2 = lax.dot_general(q_ref[sq, rows, cols], kbuf_ref[sq, keys, cols],
                                     (((1,), (1,)), ((), ())), preferred_element_type=F32) * scale2
                s2 = jnp.where(valid, s2, MASK_VALUE)
                m2 = jnp.max(s2, axis=-1, keepdims=True)
                p = jnp.exp2(s2 - m2)
                ol = _dot(p.astype(BF16),
                          vbuf_ref[sq, keys, 2 * hd * HEAD_DIM:2 * (hd + 1) * HEAD_DIM])
                l = ol[:, HEAD_DIM:]
                o_ref[sq, rows, cols] = (ol[:, :HEAD_DIM] * (1.0 / l)).astype(BF16)
                m_ref[sq, rows, stat] = jnp.broadcast_to(m2, (blk, LSE_LANES_PER_HEAD))
                l_ref[sq, rows, stat] = l[:, stat]
        if not whole_sequence:
            kbuf_ref[sq, 0:blk, :] = kbuf_ref[sq, tq:tq + blk, :]
            vbuf_ref[sq, 0:blk, :] = vbuf_ref[sq, tq:tq + blk, :]
    lse_ref[...] = (m_ref[...] + jnp.log2(l_ref[...])) * math.log(2.0)


def _band_attention(q, k, v, *, units_per_step=64):
    n, sub, d = q.shape
    blocks_per_step = units_per_step // N_HEADS
    n_qblocks = min(blocks_per_step, sub // ATTN_BLOCK)
    n_seq = blocks_per_step // n_qblocks
    tq = n_qblocks * ATTN_BLOCK
    tok = lambda width: pl.BlockSpec((n_seq, tq, width), lambda ni, ti: (ni, ti, 0))
    return pl.pallas_call(
        functools.partial(_attn_kernel, n_seq=n_seq, n_qblocks=n_qblocks, whole_sequence=(tq == sub)),
        grid=(n // n_seq, sub // tq),
        in_specs=[tok(d)] * 3,
        out_specs=[tok(d), tok(HEAD_DIM)],
        out_shape=[jax.ShapeDtypeStruct((n, sub, d), BF16),
                   jax.ShapeDtypeStruct((n, sub, HEAD_DIM), F32)],
        scratch_shapes=[pltpu.VMEM((n_seq, ATTN_BLOCK + tq, d), BF16),
                        pltpu.VMEM((n_seq, ATTN_BLOCK + tq, 2 * d), BF16),
                        pltpu.VMEM((n_seq, tq, HEAD_DIM), F32), pltpu.VMEM((n_seq, tq, HEAD_DIM), F32)],
        compiler_params=_params(("arbitrary", "arbitrary")),
        name=f"band_attn_len{sub}",
    )(q, k, v)


def _combine_kernel(h_ref, o0_ref, l0_ref, o1_ref, l1_ref, o2_ref, l2_ref, unperm1_ref, unperm2_ref,
                    wo_ref, out_ref, lse1_ref, lse2_ref, mix_ref, *, tm):
    def natural_lse(l_ref, nat_ref, dil):
        for r in range(dil):
            nat_ref[pl.ds(r, tm // dil, stride=dil), :] = l_ref[r]
        return nat_ref[...]

    def natural_out(o_ref, unperm_ref):
        return _dot(unperm_ref[...], o_ref[...].reshape(tm, D_MODEL))

    dil1, dil2 = DILATED_PATTERNS[1][1], DILATED_PATTERNS[2][1]
    lses = (l0_ref[0], natural_lse(l1_ref, lse1_ref, dil1), natural_lse(l2_ref, lse2_ref, dil2))
    outs = (o0_ref[0].astype(F32), natural_out(o1_ref, unperm1_ref), natural_out(o2_ref, unperm2_ref))
    m = jnp.maximum(jnp.maximum(lses[0], lses[1]), lses[2])
    es = [jnp.exp(l - m) for l in lses]
    inv = 1.0 / (es[0] + es[1] + es[2])
    ws = [e * inv for e in es]
    for hd in range(N_HEADS):
        cols = slice(hd * HEAD_DIM, (hd + 1) * HEAD_DIM)
        lane = hd * LSE_LANES_PER_HEAD
        mixed = None
        for w, o in zip(ws, outs):
            term = jnp.broadcast_to(w[:, lane:lane + 1], (tm, HEAD_DIM)) * o[:, cols]
            mixed = term if mixed is None else mixed + term
        mix_ref[:, cols] = mixed.astype(BF16)
    out_ref[...] = h_ref[...] + _dot(mix_ref[...], wo_ref[...])


def _combine(h, outs, lses, w_o, j, *, tm=512):
    b, s, d = h.shape
    tok = pl.BlockSpec((None, tm, d), lambda bi, ti: (bi, ti, 0))
    specs, unperms = [], []
    for (_, dil) in DILATED_PATTERNS:
        for width in (d, HEAD_DIM):
            specs.append(pl.BlockSpec((None, dil, tm // dil, width), lambda bi, ti: (bi, 0, ti, 0)))
        if dil > 1:
            unperms.append(_dilation_perm(tm, dil).T)
    args = [a for pair in zip(outs, lses) for a in pair]
    return pl.pallas_call(
        functools.partial(_combine_kernel, tm=tm),
        grid=(b, s // tm),
        in_specs=[tok] + specs + [_resident((tm, tm))] * len(unperms) + [_layer(w_o, j)],
        out_specs=tok,
        out_shape=jax.ShapeDtypeStruct(h.shape, F32),
        scratch_shapes=[pltpu.VMEM((tm, HEAD_DIM), F32), pltpu.VMEM((tm, HEAD_DIM), F32),
                        pltpu.VMEM((tm, d), BF16)],
        compiler_params=_params(("parallel", "parallel")),
        name="attn_combine",
    )(h, *args, *unperms, w_o)


def _attention_mixer(h, cos_t, sin_t, g, layer, w_qkv, w_o, j):
    b, s, d = h.shape
    outs, lses = [], []
    for group, (window, dil) in enumerate(DILATED_PATTERNS):
        assert window // dil == ATTN_BLOCK
        q, k, v = _qkv(h, cos_t, sin_t, g, layer, w_qkv, j, group, dil)
        sub = s // dil
        o_g, l_g = _band_attention(q.reshape(b * dil, sub, d), k.reshape(b * dil, sub, d),
                                   v.reshape(b * dil, sub, d))
        outs.append(o_g.reshape(b, dil, sub, d))
        lses.append(l_g.reshape(b, dil, sub, HEAD_DIM))
    return _combine(h, outs, lses, w_o, j)


def kernel(x, p, positions, norm_mix, norm_mlp, norm_ple, norm_final, sc_w_in, sc_w_conv, sc_w_out,
           attn_w_qkv, attn_w_o, lru_w_in, lru_conv_w, lru_conv_b, lru_w_a, lru_b_a, lru_w_x,
           lru_b_x, lru_lambda, lru_w_out, mlp_w_up, mlp_w_down, ple_w_gate, ple_w_proj):
    bf = lambda w: w.astype(BF16)
    norm_mix, norm_mlp, norm_ple = _rows(norm_mix), _rows(norm_mlp), _rows(norm_ple)
    sc_w_in, sc_w_out = bf(sc_w_in), bf(sc_w_out)
    attn_w_qkv, attn_w_o = bf(attn_w_qkv), bf(attn_w_o)
    lru_w_in, lru_w_out = bf(lru_w_in), bf(lru_w_out)
    lru_w_ax = bf(0.5 * jnp.concatenate([lru_w_a, lru_w_x], axis=-1))
    lru_conv_b, lru_b_a, lru_b_x, lru_lambda = (_rows(lru_conv_b), _rows(0.5 * lru_b_a),
                                                _rows(0.5 * lru_b_x), _rows(lru_lambda))
    mlp_w_up, mlp_w_down = bf(mlp_w_up), bf(mlp_w_down)
    ple_w_gate, ple_w_proj = bf(ple_w_gate), bf(ple_w_proj)
    cos_t, sin_t = _rope_tables(positions)

    h = x
    for i in range(DEPTH):
        kind, j = i % N_MIXERS, i // N_MIXERS
        if kind == 0:
            h = _conv_mixer(h, norm_mix, i, sc_w_in, sc_w_conv, sc_w_out, j)
        elif kind == 1:
            h = _attention_mixer(h, cos_t, sin_t, norm_mix, i, attn_w_qkv, attn_w_o, j)
        else:
            h = _lru_mixer(h, norm_mix, i, lru_w_in, lru_conv_w, lru_conv_b, lru_w_ax, lru_b_a,
                           lru_b_x, lru_lambda, lru_w_out, j)
        h = _post(h, p, norm_mlp, norm_ple, norm_final[None], mlp_w_up, mlp_w_down, ple_w_gate,
                  ple_w_proj, i)
    return h
```

```python
import functools
import math

import jax
import jax.numpy as jnp
from jax import lax
from jax.experimental import pallas as pl
from jax.experimental.pallas import tpu as pltpu

D_MODEL = 1024
DEPTH = 4
N_MIXERS = 3
PLE_DIM = 256
D_FF = 4 * D_MODEL
RMS_EPS = 1e-6
SC_WIDTH = 3
HEAD_DIM = 128
N_HEADS = D_MODEL // HEAD_DIM
DILATED_PATTERNS = ((128, 1), (512, 4), (2048, 16))
ROPE_THETA = 500000.0
ROPE_DIM = HEAD_DIM // 4
ROPE_HALF = ROPE_DIM // 2
D_RNN = 1280
N_LRU_BLOCKS = 10
LRU_BLOCK = D_RNN // N_LRU_BLOCKS
LRU_CONV_WIDTH = 4
LRU_C = 8.0

ATTN_BLOCK = 128
LSE_LANES_PER_HEAD = HEAD_DIM // N_HEADS
SUBLANES = 8
HALO = SUBLANES
MASK_VALUE = -1e30
VMEM_LIMIT_BYTES = 56 * 1024 * 1024

F32 = jnp.float32
BF16 = jnp.bfloat16


def _rms(x, g):
    return x * lax.rsqrt(jnp.mean(x * x, axis=-1, keepdims=True) + RMS_EPS) * g


def _dot(a, b):
    return jnp.dot(a, b, preferred_element_type=F32)


def _resident(shape, index=None):
    index = (0,) * len(shape) if index is None else index
    return pl.BlockSpec(shape, lambda *_: index, pipeline_mode=pl.Buffered(1))


def _layer(stacked, layer):
    return _resident((None,) + stacked.shape[1:], (layer, 0, 0))


def _rows(stacked):
    return stacked[:, None, :]


def _params(semantics):
    return pltpu.CompilerParams(dimension_semantics=semantics, vmem_limit_bytes=VMEM_LIMIT_BYTES)


def _post_kernel(h_ref, p_ref, gm_ref, gp_ref, gf_ref, wup_ref, wdn_ref, wg_ref, wp_ref, o_ref,
                 *, final, tf):
    h = h_ref[...]
    xn = _rms(h, gm_ref[...]).astype(BF16)
    acc = h
    for f in range(D_FF // tf):
        u = _dot(xn, wup_ref[:, f * tf:(f + 1) * tf])
        a = jnp.square(jnp.maximum(u, 0.0)).astype(BF16)
        acc = acc + _dot(a, wdn_ref[f * tf:(f + 1) * tf, :])
    xg = _rms(acc, gp_ref[...]).astype(BF16)
    gate = jax.nn.sigmoid(_dot(xg, wg_ref[...]))
    proj = _dot(p_ref[...].astype(BF16), wp_ref[...])
    out = acc + gate * proj
    if final:
        out = _rms(out, gf_ref[...])
    o_ref[...] = out


def _post(h, p, g_mlp, g_ple, g_final, w_up, w_down, w_gate, w_proj, layer, *, tm=1024, tf=512):
    b, s, d = h.shape
    tok = pl.BlockSpec((None, tm, d), lambda bi, si: (bi, si, 0))
    p_spec = pl.BlockSpec((None, None, tm, PLE_DIM), lambda bi, si: (layer, bi, si, 0))
    return pl.pallas_call(
        functools.partial(_post_kernel, final=(layer == DEPTH - 1), tf=tf),
        grid=(b, s // tm),
        in_specs=[tok, p_spec, _layer(g_mlp, layer), _layer(g_ple, layer), _resident((1, d)),
                  _layer(w_up, layer), _layer(w_down, layer), _layer(w_gate, layer),
                  _layer(w_proj, layer)],
        out_specs=tok,
        out_shape=jax.ShapeDtypeStruct(h.shape, F32),
        compiler_params=_params(("parallel", "parallel")),
        name="mlp_ple",
    )(h, p, g_mlp, g_ple, g_final, w_up, w_down, w_gate, w_proj)


def _causal_taps(buf_ref, x, w, tm):
    k_width = w.shape[0]
    buf_ref[HALO:HALO + tm, :] = x
    out = w[k_width - 1:k_width] * x
    for back in range(1, k_width):
        out = out + w[k_width - 1 - back:k_width - back] * buf_ref[HALO - back:HALO - back + tm, :]
    buf_ref[0:HALO, :] = buf_ref[tm:tm + HALO, :]
    return out


def _conv_mixer_kernel(h_ref, gn_ref, win_ref, wconv_ref, wout_ref, o_ref, buf_ref, *, tm):
    @pl.when(pl.program_id(1) == 0)
    def _():
        buf_ref[0:HALO, :] = jnp.zeros((HALO, D_MODEL), F32)

    h = h_ref[...]
    xn = _rms(h, gn_ref[...]).astype(BF16)
    gate_b = _dot(xn, win_ref[:, 0:D_MODEL])
    gate_c = _dot(xn, win_ref[:, D_MODEL:2 * D_MODEL])
    xin = _dot(xn, win_ref[:, 2 * D_MODEL:3 * D_MODEL])
    conv = _causal_taps(buf_ref, gate_c * xin, wconv_ref[...], tm)
    y = (gate_b * conv).astype(BF16)
    o_ref[...] = h + _dot(y, wout_ref[...])


def _conv_mixer(h, g, layer, w_in, w_conv, w_out, j, *, tm=1024):
    b, s, d = h.shape
    tok = pl.BlockSpec((None, tm, d), lambda bi, si: (bi, si, 0))
    return pl.pallas_call(
        functools.partial(_conv_mixer_kernel, tm=tm),
        grid=(b, s // tm),
        in_specs=[tok, _layer(g, layer), _layer(w_in, j), _layer(w_conv, j), _layer(w_out, j)],
        out_specs=tok,
        out_shape=jax.ShapeDtypeStruct(h.shape, F32),
        scratch_shapes=[pltpu.VMEM((HALO + tm, d), F32)],
        compiler_params=_params(("parallel", "arbitrary")),
        name="conv_mixer",
    )(h, g, w_in, w_conv, w_out)


def _sublane_scan(a, u, row):
    for k in (1, 2, 4):
        keep = row >= k
        a_back = jnp.where(keep, pltpu.roll(a, k, 0), 1.0)
        u_back = jnp.where(keep, pltpu.roll(u, k, 0), 0.0)
        u = a * u_back + u
        a = a * a_back
    return a, u


def _lru_kernel(h_ref, gn_ref, win_ref, cw_ref, cb_ref, wax_ref, ba_ref, bx_ref, lam_ref, wout_ref,
                o_ref, slab_ref, xs_ref, tail_ref, carry_ref, y_ref, *, tm):
    chunk = tm // SUBLANES
    pitch = chunk + SUBLANES
    n_back = LRU_CONV_WIDTH - 1
    lead = n_back * SUBLANES
    vreg = lambda j: slice(j * SUBLANES, (j + 1) * SUBLANES)

    @pl.when(pl.program_id(1) == 0)
    def _():
        tail_ref[...] = jnp.zeros(tail_ref.shape, F32)
        carry_ref[...] = jnp.zeros(carry_ref.shape, F32)

    h = h_ref[...]
    xn = _rms(h, gn_ref[...]).astype(BF16)
    taps = cw_ref[...]
    bias = cb_ref[...]
    half_log_decay = -0.5 * LRU_C * jax.nn.softplus(-lam_ref[...])
    h_prev = carry_ref[...]
    row = lax.broadcasted_iota(jnp.int32, (SUBLANES, LRU_BLOCK), 0)
    block_cols = lambda n: slice(n * LRU_BLOCK, (n + 1) * LRU_BLOCK)

    def conv_and_gate_matmul(n, x_nat):
        cols = block_cols(n)
        for s in range(SUBLANES):
            slab_ref[n, s * pitch:s * pitch + chunk, :] = x_nat[s * chunk:(s + 1) * chunk, :]
        for j in range(chunk):
            xs_ref[n, lead + j * SUBLANES:lead + (j + 1) * SUBLANES, :] = (
                slab_ref[n, pl.ds(j, SUBLANES, stride=pitch), :])
        for k in range(1, n_back + 1):
            cur = xs_ref[n, lead + (chunk - k) * SUBLANES:lead + (chunk - k + 1) * SUBLANES, :]
            prev = tail_ref[n, vreg(n_back - k), :]
            xs_ref[n, lead - k * SUBLANES:lead - (k - 1) * SUBLANES, :] = pltpu.roll(
                jnp.where(row == SUBLANES - 1, prev, cur), 1, 0)
        tail_ref[n] = xs_ref[n, lead + (chunk - n_back) * SUBLANES:lead + chunk * SUBLANES, :]
        xr = bias[:, cols]
        for k in range(LRU_CONV_WIDTH):
            xr = xr + taps[n_back - k:n_back - k + 1, cols] * xs_ref[
                n, lead - k * SUBLANES:lead - k * SUBLANES + tm, :]
        return xr, _dot(xr.astype(BF16), wax_ref[n])

    def recurrence(n, xr, ri, gate):
        cols = block_cols(n)
        t_r = jnp.tanh(ri[:, 0:LRU_BLOCK] + ba_ref[:, cols])
        t_i = jnp.tanh(ri[:, LRU_BLOCK:2 * LRU_BLOCK] + bx_ref[:, cols])
        log_a = (t_r + 1.0) * half_log_decay[:, cols]
        a = jnp.exp(log_a)
        z = -jnp.tanh(log_a) * (a * a + 1.0)
        u = (0.5 * jnp.exp2(0.5 * jnp.log2(z))) * ((t_i + 1.0) * xr)
        h_loc = jnp.zeros((SUBLANES, LRU_BLOCK), F32)
        a_tot = jnp.ones((SUBLANES, LRU_BLOCK), F32)
        for j in range(chunk):
            h_loc = a[vreg(j)] * h_loc + u[vreg(j)]
            a_tot = a_tot * a[vreg(j)]
        a_cum, h_cum = _sublane_scan(a_tot, h_loc, row)
        ends = a_cum * h_prev[:, cols] + h_cum
        h_cur = jnp.where(row == 0, h_prev[:, cols], pltpu.roll(ends, 1, 0))
        for j in range(chunk):
            h_cur = a[vreg(j)] * h_cur + u[vreg(j)]
            slab_ref[n, pl.ds(j, SUBLANES, stride=pitch), :] = h_cur
        for s in range(SUBLANES):
            rows = slice(s * chunk, (s + 1) * chunk)
            hs = slab_ref[n, s * pitch:s * pitch + chunk, :]
            y_ref[rows, cols] = (hs * jax.nn.gelu(gate[rows, :])).astype(BF16)
        return ends[SUBLANES - 1:SUBLANES, :]

    pair_width = 2 * LRU_BLOCK
    n_pairs = N_LRU_BLOCKS // 2
    in_proj = lambda lo: _dot(xn, win_ref[:, lo:lo + pair_width])
    out_proj = lambda p: _dot(y_ref[:, p * pair_width:(p + 1) * pair_width],
                              wout_ref[p * pair_width:(p + 1) * pair_width, :])
    x_pairs = [in_proj(D_RNN), in_proj(D_RNN + pair_width)]
    out = h
    carries = []
    for pair in range(n_pairs):
        blocks = (2 * pair, 2 * pair + 1)
        halves = (slice(0, LRU_BLOCK), slice(LRU_BLOCK, pair_width))
        fronts = [conv_and_gate_matmul(n, x_pairs[pair][:, half]) for n, half in zip(blocks, halves)]
        if pair > 0:
            out = out + out_proj(pair - 1)
        if pair + 2 < n_pairs:
            x_pairs.append(in_proj(D_RNN + (pair + 2) * pair_width))
        gate = in_proj(pair * pair_width)
        for n, half, (xr, ri) in zip(blocks, halves, fronts):
            carries.append(recurrence(n, xr, ri, gate[:, half]))
    out = out + out_proj(n_pairs - 1)

    carry_ref[...] = jnp.concatenate(carries, axis=1)
    o_ref[...] = out


def _lru_mixer(h, g, layer, w_in, conv_w, conv_b, w_ax, b_a, b_x, lam, w_out, j, *, tm=512):
    b, s, d = h.shape
    tok = pl.BlockSpec((None, tm, d), lambda bi, si: (bi, si, 0))
    gates = _resident((None, N_LRU_BLOCKS, LRU_BLOCK, 2 * LRU_BLOCK), (j, 0, 0, 0))
    lead = (LRU_CONV_WIDTH - 1) * SUBLANES
    slab_rows = SUBLANES * (tm // SUBLANES + SUBLANES)
    return pl.pallas_call(
        functools.partial(_lru_kernel, tm=tm),
        grid=(b, s // tm),
        in_specs=[tok, _layer(g, layer), _layer(w_in, j), _layer(conv_w, j), _layer(conv_b, j), gates,
                  _layer(b_a, j), _layer(b_x, j), _layer(lam, j), _layer(w_out, j)],
        out_specs=tok,
        out_shape=jax.ShapeDtypeStruct(h.shape, F32),
        scratch_shapes=[pltpu.VMEM((N_LRU_BLOCKS, slab_rows, LRU_BLOCK), F32),
                        pltpu.VMEM((N_LRU_BLOCKS, lead + tm, LRU_BLOCK), F32),
                        pltpu.VMEM((N_LRU_BLOCKS, lead, LRU_BLOCK), F32),
                        pltpu.VMEM((1, D_RNN), F32), pltpu.VMEM((tm, D_RNN), BF16)],
        compiler_params=_params(("parallel", "arbitrary")),
        name="rglru_mixer",
    )(h, g, w_in, conv_w, conv_b, w_ax, b_a, b_x, lam, w_out)


def _rope_table_kernel(pos_ref, invf_ref, cos_ref, sin_ref):
    ang = pos_ref[...].astype(F32) * invf_ref[...]
    lane = lax.broadcasted_iota(jnp.int32, ang.shape, 1)
    cos_ref[...] = jnp.cos(ang)
    sin_ref[...] = jnp.where(lane < ROPE_HALF, -1.0, 1.0) * jnp.sin(ang)


def _rope_tables(positions, *, ts=1024):
    b, s = positions.shape
    inv_freq = ROPE_THETA ** (-2.0 * jnp.arange(ROPE_HALF, dtype=F32) / ROPE_DIM)
    invf = jnp.concatenate([inv_freq, inv_freq, jnp.zeros((HEAD_DIM - ROPE_DIM,), F32)])[None]
    out = jax.ShapeDtypeStruct((b, s, HEAD_DIM), F32)
    tab = pl.BlockSpec((None, ts, HEAD_DIM), lambda bi, si: (bi, si, 0))
    return pl.pallas_call(
        _rope_table_kernel,
        grid=(b, s // ts),
        in_specs=[pl.BlockSpec((None, ts, 1), lambda bi, si: (bi, si, 0)), _resident((1, HEAD_DIM))],
        out_specs=[tab, tab],
        out_shape=[out, out],
        compiler_params=_params(("parallel", "parallel")),
        name="rope_tables",
    )(positions[..., None], invf)


def _dilation_perm(tm, dil):
    p = jnp.arange(tm)
    src = (p % (tm // dil)) * dil + p // (tm // dil)
    return (src[:, None] == jnp.arange(tm)[None, :]).astype(BF16)


def _qkv_kernel(h_ref, cos_ref, sin_ref, gn_ref, wq_ref, wk_ref, wv_ref, q_ref, k_ref, v_ref,
                *regroup_scratch, dil, tm):
    sub = tm // dil
    xn = _rms(h_ref[...], gn_ref[...])
    if dil == 1:
        cos, sin = cos_ref[...], sin_ref[...]
        xn = xn.astype(BF16)
    else:
        slab_ref, xp_ref = regroup_scratch
        pitch = slab_ref.shape[1] // sub
        regroup = lambda ref: jnp.concatenate(
            [ref[pl.ds(r, sub, stride=dil), :] for r in range(dil)], axis=0)
        cos, sin = regroup(cos_ref), regroup(sin_ref)
        for c in range(D_MODEL // HEAD_DIM):
            cols = slice(c * HEAD_DIM, (c + 1) * HEAD_DIM)
            for i in range(sub if pitch != dil else 1):
                rows = dil if pitch != dil else tm
                slab_ref[c, i * pitch:i * pitch + rows, :] = xn[i * dil:i * dil + rows, cols]
            for r in range(dil):
                xp_ref[r * sub:(r + 1) * sub, cols] = (
                    slab_ref[c, pl.ds(r, sub, stride=pitch), :].astype(BF16))
        xn = xp_ref[...]

    def store(out_ref, cols, val):
        for r in range(dil):
            out_ref[r, :, cols] = val[r * sub:(r + 1) * sub, :]

    low_half = lax.broadcasted_iota(jnp.int32, cos.shape, 1) < ROPE_HALF
    for w_ref, out_ref in ((wq_ref, q_ref), (wk_ref, k_ref)):
        z = _dot(xn, w_ref[...])
        for hd in range(N_HEADS):
            cols = slice(hd * HEAD_DIM, (hd + 1) * HEAD_DIM)
            x = z[:, cols]
            partner = jnp.where(low_half, pltpu.roll(x, HEAD_DIM - ROPE_HALF, 1),
                                pltpu.roll(x, ROPE_HALF, 1))
            store(out_ref, cols, (x * cos + partner * sin).astype(BF16))
    store(v_ref, slice(None), _dot(xn, wv_ref[...]).astype(BF16))


def _qkv(h, cos_t, sin_t, g, layer, w_qkv, j, group, dil, *, tm=1024):
    b, s, d = h.shape
    n_groups = len(DILATED_PATTERNS)
    tok = lambda width: pl.BlockSpec((None, tm, width), lambda bi, ti: (bi, ti, 0))
    w_spec = lambda which: _resident((None, d, d), (j, 0, which * n_groups + group))
    out = jax.ShapeDtypeStruct((b, dil, s // dil, d), BF16)
    out_spec = pl.BlockSpec((None, dil, tm // dil, d), lambda bi, ti: (bi, 0, ti, 0))
    pitch = dil + 4 if dil % 8 == 0 else dil
    regroup_scratch = [] if dil == 1 else [
        pltpu.VMEM((d // HEAD_DIM, (tm // dil) * pitch, HEAD_DIM), F32), pltpu.VMEM((tm, d), BF16)]
    return pl.pallas_call(
        functools.partial(_qkv_kernel, dil=dil, tm=tm),
        grid=(b, s // tm),
        in_specs=[tok(d), tok(HEAD_DIM), tok(HEAD_DIM), _layer(g, layer), w_spec(0), w_spec(1),
                  w_spec(2)],
        out_specs=[out_spec] * 3,
        out_shape=[out] * 3,
        scratch_shapes=regroup_scratch,
        compiler_params=_params(("parallel", "parallel")),
        name=f"qkv_dil{dil}",
    )(h, cos_t, sin_t, g, w_qkv, w_qkv, w_qkv)


def _attn_kernel(q_ref, k_ref, v_ref, o_ref, lse_ref, kbuf_ref, vbuf_ref, m_ref, l_ref,
                 *, n_seq, n_qblocks, whole_sequence):
    step = pl.program_id(1)
    blk = ATTN_BLOCK
    tq = n_qblocks * blk
    vcols = lambda hd: slice(2 * hd * HEAD_DIM, (2 * hd + 1) * HEAD_DIM)

    @pl.when((pl.program_id(0) == 0) & (step == 0))
    def _():
        kbuf_ref[...] = jnp.zeros(kbuf_ref.shape, BF16)
        vbuf_ref[...] = jnp.ones(vbuf_ref.shape, BF16)

    qi = lax.broadcasted_iota(jnp.int32, (blk, 2 * blk), 0)
    kj = lax.broadcasted_iota(jnp.int32, (blk, 2 * blk), 1)
    rel = blk + qi - kj
    band = (rel >= 0) & (rel <= blk)
    after_start = False if whole_sequence else step > 0
    band_first = band & ((kj >= blk) | after_start)
    scale2 = HEAD_DIM ** -0.5 * math.log2(math.e)

    for sq in range(n_seq):
        kbuf_ref[sq, blk:blk + tq, :] = k_ref[sq]
        for hd in range(N_HEADS):
            vbuf_ref[sq, blk:blk + tq, vcols(hd)] = v_ref[sq, :, hd * HEAD_DIM:(hd + 1) * HEAD_DIM]
        for qb in range(n_qblocks):
            rows = slice(qb * blk, (qb + 1) * blk)
            keys = slice(qb * blk, (qb + 2) * blk)
            valid = band_first if qb == 0 else band
            for hd in range(N_HEADS):
                cols = slice(hd * HEAD_DIM, (hd + 1) * HEAD_DIM)
                stat = slice(hd * LSE_LANES_PER_HEAD, (hd + 1) * LSE_LANES_PER_HEAD)
                s2 = lax.dot_general(q_ref[sq, rows, cols], kbuf_ref[sq, keys, cols],
                                     (((1,), (1,)), ((), ())), preferred_element_type=F32) * scale2
                s2 = jnp.where(valid, s2, MASK_VALUE)
                m2 = jnp.max(s2, axis=-1, keepdims=True)
                p = jnp.exp2(s2 - m2)
                ol = _dot(p.astype(BF16),
                          vbuf_ref[sq, keys, 2 * hd * HEAD_DIM:2 * (hd + 1) * HEAD_DIM])
                l = ol[:, HEAD_DIM:]
                o_ref[sq, rows, cols] = (ol[:, :HEAD_DIM] * (1.0 / l)).astype(BF16)
                m_ref[sq, rows, stat] = jnp.broadcast_to(m2, (blk, LSE_LANES_PER_HEAD))
                l_ref[sq, rows, stat] = l[:, stat]
        if not whole_sequence:
            kbuf_ref[sq, 0:blk, :] = kbuf_ref[sq, tq:tq + blk, :]
            vbuf_ref[sq, 0:blk, :] = vbuf_ref[sq, tq:tq + blk, :]
    lse_ref[...] = (m_ref[...] + jnp.log2(l_ref[...])) * math.log(2.0)


def _band_attention(q, k, v, *, units_per_step=64):
    n, sub, d = q.shape
    blocks_per_step = units_per_step // N_HEADS
    n_qblocks = min(blocks_per_step, sub // ATTN_BLOCK)
    n_seq = blocks_per_step // n_qblocks
    tq = n_qblocks * ATTN_BLOCK
    tok = lambda width: pl.BlockSpec((n_seq, tq, width), lambda ni, ti: (ni, ti, 0))
    return pl.pallas_call(
        functools.partial(_attn_kernel, n_seq=n_seq, n_qblocks=n_qblocks, whole_sequence=(tq == sub)),
        grid=(n // n_seq, sub // tq),
        in_specs=[tok(d)] * 3,
        out_specs=[tok(d), tok(HEAD_DIM)],
        out_shape=[jax.ShapeDtypeStruct((n, sub, d), BF16),
                   jax.ShapeDtypeStruct((n, sub, HEAD_DIM), F32)],
        scratch_shapes=[pltpu.VMEM((n_seq, ATTN_BLOCK + tq, d), BF16),
                        pltpu.VMEM((n_seq, ATTN_BLOCK + tq, 2 * d), BF16),
                        pltpu.VMEM((n_seq, tq, HEAD_DIM), F32), pltpu.VMEM((n_seq, tq, HEAD_DIM), F32)],
        compiler_params=_params(("arbitrary", "arbitrary")),
        name=f"band_attn_len{sub}",
    )(q, k, v)


def _combine_kernel(h_ref, o0_ref, l0_ref, o1_ref, l1_ref, o2_ref, l2_ref, unperm1_ref, unperm2_ref,
                    wo_ref, out_ref, lse1_ref, lse2_ref, mix_ref, *, tm):
    def natural_lse(l_ref, nat_ref, dil):
        for r in range(dil):
            nat_ref[pl.ds(r, tm // dil, stride=dil), :] = l_ref[r]
        return nat_ref[...]

    def natural_out(o_ref, unperm_ref):
        return _dot(unperm_ref[...], o_ref[...].reshape(tm, D_MODEL))

    dil1, dil2 = DILATED_PATTERNS[1][1], DILATED_PATTERNS[2][1]
    lses = (l0_ref[0], natural_lse(l1_ref, lse1_ref, dil1), natural_lse(l2_ref, lse2_ref, dil2))
    outs = (o0_ref[0].astype(F32), natural_out(o1_ref, unperm1_ref), natural_out(o2_ref, unperm2_ref))
    m = jnp.maximum(jnp.maximum(lses[0], lses[1]), lses[2])
    es = [jnp.exp(l - m) for l in lses]
    inv = 1.0 / (es[0] + es[1] + es[2])
    ws = [e * inv for e in es]
    for hd in range(N_HEADS):
        cols = slice(hd * HEAD_DIM, (hd + 1) * HEAD_DIM)
        lane = hd * LSE_LANES_PER_HEAD
        mixed = None
        for w, o in zip(ws, outs):
            term = jnp.broadcast_to(w[:, lane:lane + 1], (tm, HEAD_DIM)) * o[:, cols]
            mixed = term if mixed is None else mixed + term
        mix_ref[:, cols] = mixed.astype(BF16)
    out_ref[...] = h_ref[...] + _dot(mix_ref[...], wo_ref[...])


def _combine(h, outs, lses, w_o, j, *, tm=512):
    b, s, d = h.shape
    tok = pl.BlockSpec((None, tm, d), lambda bi, ti: (bi, ti, 0))
    specs, unperms = [], []
    for (_, dil) in DILATED_PATTERNS:
        for width in (d, HEAD_DIM):
            specs.append(pl.BlockSpec((None, dil, tm // dil, width), lambda bi, ti: (bi, 0, ti, 0)))
        if dil > 1:
            unperms.append(_dilation_perm(tm, dil).T)
    args = [a for pair in zip(outs, lses) for a in pair]
    return pl.pallas_call(
        functools.partial(_combine_kernel, tm=tm),
        grid=(b, s // tm),
        in_specs=[tok] + specs + [_resident((tm, tm))] * len(unperms) + [_layer(w_o, j)],
        out_specs=tok,
        out_shape=jax.ShapeDtypeStruct(h.shape, F32),
        scratch_shapes=[pltpu.VMEM((tm, HEAD_DIM), F32), pltpu.VMEM((tm, HEAD_DIM), F32),
                        pltpu.VMEM((tm, d), BF16)],
        compiler_params=_params(("parallel", "parallel")),
        name="attn_combine",
    )(h, *args, *unperms, w_o)


def _attention_mixer(h, cos_t, sin_t, g, layer, w_qkv, w_o, j):
    b, s, d = h.shape
    outs, lses = [], []
    for group, (window, dil) in enumerate(DILATED_PATTERNS):
        assert window // dil == ATTN_BLOCK
        q, k, v = _qkv(h, cos_t, sin_t, g, layer, w_qkv, j, group, dil)
        sub = s // dil
        o_g, l_g = _band_attention(q.reshape(b * dil, sub, d), k.reshape(b * dil, sub, d),
                                   v.reshape(b * dil, sub, d))
        outs.append(o_g.reshape(b, dil, sub, d))
        lses.append(l_g.reshape(b, dil, sub, HEAD_DIM))
    return _combine(h, outs, lses, w_o, j)


def kernel(x, p, positions, norm_mix, norm_mlp, norm_ple, norm_final, sc_w_in, sc_w_conv, sc_w_out,
           attn_w_qkv, attn_w_o, lru_w_in, lru_conv_w, lru_conv_b, lru_w_a, lru_b_a, lru_w_x,
           lru_b_x, lru_lambda, lru_w_out, mlp_w_up, mlp_w_down, ple_w_gate, ple_w_proj):
    bf = lambda w: w.astype(BF16)
    norm_mix, norm_mlp, norm_ple = _rows(norm_mix), _rows(norm_mlp), _rows(norm_ple)
    sc_w_in, sc_w_out = bf(sc_w_in), bf(sc_w_out)
    attn_w_qkv, attn_w_o = bf(attn_w_qkv), bf(attn_w_o)
    lru_w_in, lru_w_out = bf(lru_w_in), bf(lru_w_out)
    lru_w_ax = bf(0.5 * jnp.concatenate([lru_w_a, lru_w_x], axis=-1))
    lru_conv_b, lru_b_a, lru_b_x, lru_lambda = (_rows(lru_conv_b), _rows(0.5 * lru_b_a),
                                                _rows(0.5 * lru_b_x), _rows(lru_lambda))
    mlp_w_up, mlp_w_down = bf(mlp_w_up), bf(mlp_w_down)
    ple_w_gate, ple_w_proj = bf(ple_w_gate), bf(ple_w_proj)
    cos_t, sin_t = _rope_tables(positions)

    h = x
    for i in range(DEPTH):
        kind, j = i % N_MIXERS, i // N_MIXERS
        if kind == 0:
            h = _conv_mixer(h, norm_mix, i, sc_w_in, sc_w_conv, sc_w_out, j)
        elif kind == 1:
            h = _attention_mixer(h, cos_t, sin_t, norm_mix, i, attn_w_qkv, attn_w_o, j)
        else:
            h = _lru_mixer(h, norm_mix, i, lru_w_in, lru_conv_w, lru_conv_b, lru_w_ax, lru_b_a,
                           lru_b_x, lru_lambda, lru_w_out, j)
        h = _post(h, p, norm_mlp, norm_ple, norm_final[None], mlp_w_up, mlp_w_down, ple_w_gate,
                  ple_w_proj, i)
    return h
```

```python
import functools
import math

import jax
import jax.numpy as jnp
from jax import lax
from jax.experimental import pallas as pl
from jax.experimental.pallas import tpu as pltpu

D_MODEL = 1024
DEPTH = 4
N_MIXERS = 3
PLE_DIM = 256
D_FF = 4 * D_MODEL
RMS_EPS = 1e-6
SC_WIDTH = 3
HEAD_DIM = 128
N_HEADS = D_MODEL // HEAD_DIM
DILATED_PATTERNS = ((128, 1), (512, 4), (2048, 16))
ROPE_THETA = 500000.0
ROPE_DIM = HEAD_DIM // 4
ROPE_HALF = ROPE_DIM // 2
D_RNN = 1280
N_LRU_BLOCKS = 10
LRU_BLOCK = D_RNN // N_LRU_BLOCKS
LRU_CONV_WIDTH = 4
LRU_C = 8.0

ATTN_BLOCK = 128
LSE_LANES_PER_HEAD = HEAD_DIM // N_HEADS
SUBLANES = 8
HALO = SUBLANES
MASK_VALUE = -1e30
VMEM_LIMIT_BYTES = 56 * 1024 * 1024
WEIGHT_CHUNK = (512, 1024)

F32 = jnp.float32
BF16 = jnp.bfloat16


def _rms(x, g):
    return x * lax.rsqrt(jnp.mean(x * x, axis=-1, keepdims=True) + RMS_EPS) * g


def _dot(a, b):
    return jnp.dot(a, b, preferred_element_type=F32)


def _resident(shape, index=None):
    index = (0,) * len(shape) if index is None else index
    return pl.BlockSpec(shape, lambda *_: index, pipeline_mode=pl.Buffered(1))


def _layer(stacked, layer):
    return _resident((None,) + stacked.shape[1:], (layer, 0, 0))


def _rows(stacked):
    return stacked[:, None, :]


def _params(semantics):
    return pltpu.CompilerParams(dimension_semantics=semantics, vmem_limit_bytes=VMEM_LIMIT_BYTES)


def _load_weights_as_bf16(loads, stage_ref, sem):
    _, stage_rows, stage_cols = stage_ref.shape
    chunks = []
    for src, dst in loads:
        rows, cols = dst.shape
        chunk_rows = min(rows, stage_rows)
        for r0 in range(0, rows, chunk_rows):
            for c0 in range(0, cols, stage_cols):
                window = (pl.ds(r0, chunk_rows), pl.ds(c0, stage_cols))
                chunks.append((src.at[window], dst.at[window], chunk_rows))

    def copy(i):
        src, _, chunk_rows = chunks[i]
        return pltpu.make_async_copy(src, stage_ref.at[i % 2, pl.ds(0, chunk_rows)], sem.at[i % 2])

    copy(0).start()
    for i, (_, dst, chunk_rows) in enumerate(chunks):
        if i + 1 < len(chunks):
            copy(i + 1).start()
        copy(i).wait()
        dst[...] = stage_ref[i % 2, 0:chunk_rows, :].astype(BF16)


def _post_kernel(h_ref, p_ref, gm_ref, gp_ref, gf_ref, wup_hbm, wdn_hbm, wg_hbm, wp_hbm, o_ref,
                 wup_ref, wdn_ref, wg_ref, wp_ref, stage_ref, sem, *, layer, tf):
    @pl.when((pl.program_id(0) == 0) & (pl.program_id(1) == 0))
    def _():
        _load_weights_as_bf16([(wup_hbm.at[layer], wup_ref), (wdn_hbm.at[layer], wdn_ref),
                               (wg_hbm.at[layer], wg_ref), (wp_hbm.at[layer], wp_ref)],
                              stage_ref, sem)

    h = h_ref[...]
    xn = _rms(h, gm_ref[...]).astype(BF16)
    acc = h
    for f in range(D_FF // tf):
        u = _dot(xn, wup_ref[:, f * tf:(f + 1) * tf])
        a = jnp.square(jnp.maximum(u, 0.0)).astype(BF16)
        acc = acc + _dot(a, wdn_ref[f * tf:(f + 1) * tf, :])
    xg = _rms(acc, gp_ref[...]).astype(BF16)
    gate = jax.nn.sigmoid(_dot(xg, wg_ref[...]))
    proj = _dot(p_ref[...].astype(BF16), wp_ref[...])
    out = acc + gate * proj
    if layer == DEPTH - 1:
        out = _rms(out, gf_ref[...])
    o_ref[...] = out


def _post(h, p, g_mlp, g_ple, g_final, w_up, w_down, w_gate, w_proj, layer, *, tm=1024, tf=512):
    b, s, d = h.shape
    tok = pl.BlockSpec((None, tm, d), lambda bi, si: (bi, si, 0))
    p_spec = pl.BlockSpec((None, None, tm, PLE_DIM), lambda bi, si: (layer, bi, si, 0))
    hbm = pl.BlockSpec(memory_space=pl.ANY)
    return pl.pallas_call(
        functools.partial(_post_kernel, layer=layer, tf=tf),
        grid=(b, s // tm),
        in_specs=[tok, p_spec, _layer(g_mlp, layer), _layer(g_ple, layer), _resident((1, d)),
                  hbm, hbm, hbm, hbm],
        out_specs=tok,
        out_shape=jax.ShapeDtypeStruct(h.shape, F32),
        scratch_shapes=[pltpu.VMEM(w_up.shape[1:], BF16), pltpu.VMEM(w_down.shape[1:], BF16),
                        pltpu.VMEM(w_gate.shape[1:], BF16), pltpu.VMEM(w_proj.shape[1:], BF16),
                        pltpu.VMEM((2,) + WEIGHT_CHUNK, F32), pltpu.SemaphoreType.DMA((2,))],
        compiler_params=_params(("arbitrary", "arbitrary")),
        name="mlp_ple",
    )(h, p, g_mlp, g_ple, g_final, w_up, w_down, w_gate, w_proj)


def _causal_taps(buf_ref, x, w, tm):
    k_width = w.shape[0]
    buf_ref[HALO:HALO + tm, :] = x
    out = w[k_width - 1:k_width] * x
    for back in range(1, k_width):
        out = out + w[k_width - 1 - back:k_width - back] * buf_ref[HALO - back:HALO - back + tm, :]
    buf_ref[0:HALO, :] = buf_ref[tm:tm + HALO, :]
    return out


def _conv_mixer_kernel(h_ref, gn_ref, win_ref, wconv_ref, wout_ref, o_ref, buf_ref, *, tm):
    @pl.when(pl.program_id(1) == 0)
    def _():
        buf_ref[0:HALO, :] = jnp.zeros((HALO, D_MODEL), F32)

    h = h_ref[...]
    xn = _rms(h, gn_ref[...]).astype(BF16)
    gate_b = _dot(xn, win_ref[:, 0:D_MODEL])
    gate_c = _dot(xn, win_ref[:, D_MODEL:2 * D_MODEL])
    xin = _dot(xn, win_ref[:, 2 * D_MODEL:3 * D_MODEL])
    conv = _causal_taps(buf_ref, gate_c * xin, wconv_ref[...], tm)
    y = (gate_b * conv).astype(BF16)
    o_ref[...] = h + _dot(y, wout_ref[...])


def _conv_mixer(h, g, layer, w_in, w_conv, w_out, j, *, tm=1024):
    b, s, d = h.shape
    tok = pl.BlockSpec((None, tm, d), lambda bi, si: (bi, si, 0))
    return pl.pallas_call(
        functools.partial(_conv_mixer_kernel, tm=tm),
        grid=(b, s // tm),
        in_specs=[tok, _layer(g, layer), _layer(w_in, j), _layer(w_conv, j), _layer(w_out, j)],
        out_specs=tok,
        out_shape=jax.ShapeDtypeStruct(h.shape, F32),
        scratch_shapes=[pltpu.VMEM((HALO + tm, d), F32)],
        compiler_params=_params(("parallel", "arbitrary")),
        name="conv_mixer",
    )(h, g, w_in, w_conv, w_out)


def _sublane_scan(a, u, row):
    for k in (1, 2, 4):
        keep = row >= k
        a_back = jnp.where(keep, pltpu.roll(a, k, 0), 1.0)
        u_back = jnp.where(keep, pltpu.roll(u, k, 0), 0.0)
        u = a * u_back + u
        a = a * a_back
    return a, u


def _lru_kernel(h_ref, gn_ref, win_ref, cw_ref, cb_ref, wax_ref, ba_ref, bx_ref, lam_ref, wout_ref,
                o_ref, slab_ref, xs_ref, tail_ref, carry_ref, y_ref, *, tm):
    chunk = tm // SUBLANES
    pitch = chunk + SUBLANES
    n_back = LRU_CONV_WIDTH - 1
    lead = n_back * SUBLANES
    vreg = lambda j: slice(j * SUBLANES, (j + 1) * SUBLANES)

    @pl.when(pl.program_id(1) == 0)
    def _():
        tail_ref[...] = jnp.zeros(tail_ref.shape, F32)
        carry_ref[...] = jnp.zeros(carry_ref.shape, F32)

    h = h_ref[...]
    xn = _rms(h, gn_ref[...]).astype(BF16)
    taps = cw_ref[...]
    bias = cb_ref[...]
    half_log_decay = -0.5 * LRU_C * jax.nn.softplus(-lam_ref[...])
    h_prev = carry_ref[...]
    row = lax.broadcasted_iota(jnp.int32, (SUBLANES, LRU_BLOCK), 0)
    block_cols = lambda n: slice(n * LRU_BLOCK, (n + 1) * LRU_BLOCK)

    def conv_and_gate_matmul(n, x_nat):
        cols = block_cols(n)
        for s in range(SUBLANES):
            slab_ref[n, s * pitch:s * pitch + chunk, :] = x_nat[s * chunk:(s + 1) * chunk, :]
        for j in range(chunk):
            xs_ref[n, lead + j * SUBLANES:lead + (j + 1) * SUBLANES, :] = (
                slab_ref[n, pl.ds(j, SUBLANES, stride=pitch), :])
        for k in range(1, n_back + 1):
            cur = xs_ref[n, lead + (chunk - k) * SUBLANES:lead + (chunk - k + 1) * SUBLANES, :]
            prev = tail_ref[n, vreg(n_back - k), :]
            xs_ref[n, lead - k * SUBLANES:lead - (k - 1) * SUBLANES, :] = pltpu.roll(
                jnp.where(row == SUBLANES - 1, prev, cur), 1, 0)
        tail_ref[n] = xs_ref[n, lead + (chunk - n_back) * SUBLANES:lead + chunk * SUBLANES, :]
        xr = bias[:, cols]
        for k in range(LRU_CONV_WIDTH):
            xr = xr + taps[n_back - k:n_back - k + 1, cols] * xs_ref[
                n, lead - k * SUBLANES:lead - k * SUBLANES + tm, :]
        return xr, _dot(xr.astype(BF16), wax_ref[n])

    def recurrence(n, xr, ri, gate):
        cols = block_cols(n)
        t_r = jnp.tanh(ri[:, 0:LRU_BLOCK] + ba_ref[:, cols])
        t_i = jnp.tanh(ri[:, LRU_BLOCK:2 * LRU_BLOCK] + bx_ref[:, cols])
        log_a = (t_r + 1.0) * half_log_decay[:, cols]
        a = jnp.exp(log_a)
        z = -jnp.tanh(log_a) * (a * a + 1.0)
        u = (0.5 * jnp.exp2(0.5 * jnp.log2(z))) * ((t_i + 1.0) * xr)
        h_loc = jnp.zeros((SUBLANES, LRU_BLOCK), F32)
        a_tot = jnp.ones((SUBLANES, LRU_BLOCK), F32)
        for j in range(chunk):
            h_loc = a[vreg(j)] * h_loc + u[vreg(j)]
            a_tot = a_tot * a[vreg(j)]
        a_cum, h_cum = _sublane_scan(a_tot, h_loc, row)
        ends = a_cum * h_prev[:, cols] + h_cum
        h_cur = jnp.where(row == 0, h_prev[:, cols], pltpu.roll(ends, 1, 0))
        for j in range(chunk):
            h_cur = a[vreg(j)] * h_cur + u[vreg(j)]
            slab_ref[n, pl.ds(j, SUBLANES, stride=pitch), :] = h_cur
        for s in range(SUBLANES):
            rows = slice(s * chunk, (s + 1) * chunk)
            hs = slab_ref[n, s * pitch:s * pitch + chunk, :]
            y_ref[rows, cols] = (hs * jax.nn.gelu(gate[rows, :])).astype(BF16)
        return ends[SUBLANES - 1:SUBLANES, :]

    pair_width = 2 * LRU_BLOCK
    n_pairs = N_LRU_BLOCKS // 2
    in_proj = lambda lo: _dot(xn, win_ref[:, lo:lo + pair_width])
    out_proj = lambda p: _dot(y_ref[:, p * pair_width:(p + 1) * pair_width],
                              wout_ref[p * pair_width:(p + 1) * pair_width, :])
    x_pairs = [in_proj(D_RNN), in_proj(D_RNN + pair_width)]
    out = h
    carries = []
    for pair in range(n_pairs):
        blocks = (2 * pair, 2 * pair + 1)
        halves = (slice(0, LRU_BLOCK), slice(LRU_BLOCK, pair_width))
        fronts = [conv_and_gate_matmul(n, x_pairs[pair][:, half]) for n, half in zip(blocks, halves)]
        if pair > 0:
            out = out + out_proj(pair - 1)
        if pair + 2 < n_pairs:
            x_pairs.append(in_proj(D_RNN + (pair + 2) * pair_width))
        gate = in_proj(pair * pair_width)
        for n, half, (xr, ri) in zip(blocks, halves, fronts):
            carries.append(recurrence(n, xr, ri, gate[:, half]))
    out = out + out_proj(n_pairs - 1)

    carry_ref[...] = jnp.concatenate(carries, axis=1)
    o_ref[...] = out


def _lru_mixer(h, g, layer, w_in, conv_w, conv_b, w_ax, b_a, b_x, lam, w_out, j, *, tm=512):
    b, s, d = h.shape
    tok = pl.BlockSpec((None, tm, d), lambda bi, si: (bi, si, 0))
    gates = _resident((None, N_LRU_BLOCKS, LRU_BLOCK, 2 * LRU_BLOCK), (j, 0, 0, 0))
    lead = (LRU_CONV_WIDTH - 1) * SUBLANES
    slab_rows = SUBLANES * (tm // SUBLANES + SUBLANES)
    return pl.pallas_call(
        functools.partial(_lru_kernel, tm=tm),
        grid=(b, s // tm),
        in_specs=[tok, _layer(g, layer), _layer(w_in, j), _layer(conv_w, j), _layer(conv_b, j), gates,
                  _layer(b_a, j), _layer(b_x, j), _layer(lam, j), _layer(w_out, j)],
        out_specs=tok,
        out_shape=jax.ShapeDtypeStruct(h.shape, F32),
        scratch_shapes=[pltpu.VMEM((N_LRU_BLOCKS, slab_rows, LRU_BLOCK), F32),
                        pltpu.VMEM((N_LRU_BLOCKS, lead + tm, LRU_BLOCK), F32),
                        pltpu.VMEM((N_LRU_BLOCKS, lead, LRU_BLOCK), F32),
                        pltpu.VMEM((1, D_RNN), F32), pltpu.VMEM((tm, D_RNN), BF16)],
        compiler_params=_params(("parallel", "arbitrary")),
        name="rglru_mixer",
    )(h, g, w_in, conv_w, conv_b, w_ax, b_a, b_x, lam, w_out)


def _rope_table_kernel(pos_ref, invf_ref, cos_ref, sin_ref):
    ang = pos_ref[...].astype(F32) * invf_ref[...]
    lane = lax.broadcasted_iota(jnp.int32, ang.shape, 1)
    cos_ref[...] = jnp.cos(ang)
    sin_ref[...] = jnp.where(lane < ROPE_HALF, -1.0, 1.0) * jnp.sin(ang)


def _rope_tables(positions, *, ts=1024):
    b, s = positions.shape
    inv_freq = ROPE_THETA ** (-2.0 * jnp.arange(ROPE_HALF, dtype=F32) / ROPE_DIM)
    invf = jnp.concatenate([inv_freq, inv_freq, jnp.zeros((HEAD_DIM - ROPE_DIM,), F32)])[None]
    out = jax.ShapeDtypeStruct((b, s, HEAD_DIM), F32)
    tab = pl.BlockSpec((None, ts, HEAD_DIM), lambda bi, si: (bi, si, 0))
    return pl.pallas_call(
        _rope_table_kernel,
        grid=(b, s // ts),
        in_specs=[pl.BlockSpec((None, ts, 1), lambda bi, si: (bi, si, 0)), _resident((1, HEAD_DIM))],
        out_specs=[tab, tab],
        out_shape=[out, out],
        compiler_params=_params(("parallel", "parallel")),
        name="rope_tables",
    )(positions[..., None], invf)


def _dilation_perm(tm, dil):
    p = jnp.arange(tm)
    src = (p % (tm // dil)) * dil + p // (tm // dil)
    return (src[:, None] == jnp.arange(tm)[None, :]).astype(BF16)


def _qkv_kernel(h_ref, cos_ref, sin_ref, gn_ref, wq_ref, wk_ref, wv_ref, q_ref, k_ref, v_ref,
                *regroup_scratch, dil, tm):
    sub = tm // dil
    xn = _rms(h_ref[...], gn_ref[...])
    if dil == 1:
        cos, sin = cos_ref[...], sin_ref[...]
        xn = xn.astype(BF16)
    else:
        slab_ref, xp_ref = regroup_scratch
        pitch = slab_ref.shape[1] // sub
        regroup = lambda ref: jnp.concatenate(
            [ref[pl.ds(r, sub, stride=dil), :] for r in range(dil)], axis=0)
        cos, sin = regroup(cos_ref), regroup(sin_ref)
        for c in range(D_MODEL // HEAD_DIM):
            cols = slice(c * HEAD_DIM, (c + 1) * HEAD_DIM)
            for i in range(sub if pitch != dil else 1):
                rows = dil if pitch != dil else tm
                slab_ref[c, i * pitch:i * pitch + rows, :] = xn[i * dil:i * dil + rows, cols]
            for r in range(dil):
                xp_ref[r * sub:(r + 1) * sub, cols] = (
                    slab_ref[c, pl.ds(r, sub, stride=pitch), :].astype(BF16))
        xn = xp_ref[...]

    def store(out_ref, cols, val):
        for r in range(dil):
            out_ref[r, :, cols] = val[r * sub:(r + 1) * sub, :]

    low_half = lax.broadcasted_iota(jnp.int32, cos.shape, 1) < ROPE_HALF
    for w_ref, out_ref in ((wq_ref, q_ref), (wk_ref, k_ref)):
        z = _dot(xn, w_ref[...])
        for hd in range(N_HEADS):
            cols = slice(hd * HEAD_DIM, (hd + 1) * HEAD_DIM)
            x = z[:, cols]
            partner = jnp.where(low_half, pltpu.roll(x, HEAD_DIM - ROPE_HALF, 1),
                                pltpu.roll(x, ROPE_HALF, 1))
            store(out_ref, cols, (x * cos + partner * sin).astype(BF16))
    store(v_ref, slice(None), _dot(xn, wv_ref[...]).astype(BF16))


def _qkv(h, cos_t, sin_t, g, layer, w_qkv, j, group, dil, *, tm=1024):
    b, s, d = h.shape
    n_groups = len(DILATED_PATTERNS)
    tok = lambda width: pl.BlockSpec((None, tm, width), lambda bi, ti: (bi, ti, 0))
    w_spec = lambda which: _resident((None, d, d), (j, 0, which * n_groups + group))
    out = jax.ShapeDtypeStruct((b, dil, s // dil, d), BF16)
    out_spec = pl.BlockSpec((None, dil, tm // dil, d), lambda bi, ti: (bi, 0, ti, 0))
    pitch = dil + 4 if dil % 8 == 0 else dil
    regroup_scratch = [] if dil == 1 else [
        pltpu.VMEM((d // HEAD_DIM, (tm // dil) * pitch, HEAD_DIM), F32), pltpu.VMEM((tm, d), BF16)]
    return pl.pallas_call(
        functools.partial(_qkv_kernel, dil=dil, tm=tm),
        grid=(b, s // tm),
        in_specs=[tok(d), tok(HEAD_DIM), tok(HEAD_DIM), _layer(g, layer), w_spec(0), w_spec(1),
                  w_spec(2)],
        out_specs=[out_spec] * 3,
        out_shape=[out] * 3,
        scratch_shapes=regroup_scratch,
        compiler_params=_params(("parallel", "parallel")),
        name=f"qkv_dil{dil}",
    )(h, cos_t, sin_t, g, w_qkv, w_qkv, w_qkv)


def _attn_kernel(q_ref, k_ref, v_ref, o_ref, lse_ref, kbuf_ref, vbuf_ref, m_ref, l_ref,
                 *, n_seq, n_qblocks, whole_sequence):
    step = pl.program_id(1)
    blk = ATTN_BLOCK
    tq = n_qblocks * blk
    vcols = lambda hd: slice(2 * hd * HEAD_DIM, (2 * hd + 1) * HEAD_DIM)

    @pl.when((pl.program_id(0) == 0) & (step == 0))
    def _():
        kbuf_ref[...] = jnp.zeros(kbuf_ref.shape, BF16)
        vbuf_ref[...] = jnp.ones(vbuf_ref.shape, BF16)

    qi = lax.broadcasted_iota(jnp.int32, (blk, 2 * blk), 0)
    kj = lax.broadcasted_iota(jnp.int32, (blk, 2 * blk), 1)
    rel = blk + qi - kj
    band = (rel >= 0) & (rel <= blk)
    after_start = False if whole_sequence else step > 0
    band_first = band & ((kj >= blk) | after_start)
    scale2 = HEAD_DIM ** -0.5 * math.log2(math.e)

    for sq in range(n_seq):
        kbuf_ref[sq, blk:blk + tq, :] = k_ref[sq]
        for hd in range(N_HEADS):
            vbuf_ref[sq, blk:blk + tq, vcols(hd)] = v_ref[sq, :, hd * HEAD_DIM:(hd + 1) * HEAD_DIM]
        for qb in range(n_qblocks):
            rows = slice(qb * blk, (qb + 1) * blk)
            keys = slice(qb * blk, (qb + 2) * blk)
            valid = band_first if qb == 0 else band
            for hd in range(N_HEADS):
                cols = slice(hd * HEAD_DIM, (hd + 1) * HEAD_DIM)
                stat = slice(hd * LSE_LANES_PER_HEAD, (hd + 1) * LSE_LANES_PER_HEAD)
                s2 = lax.dot_general(q_ref[sq, rows, cols], kbuf_ref[sq, keys, cols],
                                     (((1,), (1,)), ((), ())), preferred_element_type=F32) * scale2
                s2 = jnp.where(valid, s2, MASK_VALUE)
                m2 = jnp.max(s2, axis=-1, keepdims=True)
                p = jnp.exp2(s2 - m2)
                ol = _dot(p.astype(BF16),
                          vbuf_ref[sq, keys, 2 * hd * HEAD_DIM:2 * (hd + 1) * HEAD_DIM])
                l = ol[:, HEAD_DIM:]
                o_ref[sq, rows, cols] = (ol[:, :HEAD_DIM] * (1.0 / l)).astype(BF16)
                m_ref[sq, rows, stat] = jnp.broadcast_to(m2, (blk, LSE_LANES_PER_HEAD))
                l_ref[sq, rows, stat] = l[:, stat]
        if not whole_sequence:
            kbuf_ref[sq, 0:blk, :] = kbuf_ref[sq, tq:tq + blk, :]
            vbuf_ref[sq, 0:blk, :] = vbuf_ref[sq, tq:tq + blk, :]
    lse_ref[...] = (m_ref[...] + jnp.log2(l_ref[...])) * math.log(2.0)


def _band_attention(q, k, v, *, units_per_step=64):
    n, sub, d = q.shape
    blocks_per_step = units_per_step // N_HEADS
    n_qblocks = min(blocks_per_step, sub // ATTN_BLOCK)
    n_seq = blocks_per_step // n_qblocks
    tq = n_qblocks * ATTN_BLOCK
    tok = lambda width: pl.BlockSpec((n_seq, tq, width), lambda ni, ti: (ni, ti, 0))
    return pl.pallas_call(
        functools.partial(_attn_kernel, n_seq=n_seq, n_qblocks=n_qblocks, whole_sequence=(tq == sub)),
        grid=(n // n_seq, sub // tq),
        in_specs=[tok(d)] * 3,
        out_specs=[tok(d), tok(HEAD_DIM)],
        out_shape=[jax.ShapeDtypeStruct((n, sub, d), BF16),
                   jax.ShapeDtypeStruct((n, sub, HEAD_DIM), F32)],
        scratch_shapes=[pltpu.VMEM((n_seq, ATTN_BLOCK + tq, d), BF16),
                        pltpu.VMEM((n_seq, ATTN_BLOCK + tq, 2 * d), BF16),
                        pltpu.VMEM((n_seq, tq, HEAD_DIM), F32), pltpu.VMEM((n_seq, tq, HEAD_DIM), F32)],
        compiler_params=_params(("arbitrary", "arbitrary")),
        name=f"band_attn_len{sub}",
    )(q, k, v)


def _combine_kernel(h_ref, o0_ref, l0_ref, o1_ref, l1_ref, o2_ref, l2_ref, unperm1_ref, unperm2_ref,
                    wo_ref, out_ref, lse1_ref, lse2_ref, mix_ref, *, tm):
    def natural_lse(l_ref, nat_ref, dil):
        for r in range(dil):
            nat_ref[pl.ds(r, tm // dil, stride=dil), :] = l_ref[r]
        return nat_ref[...]

    def natural_out(o_ref, unperm_ref):
        return _dot(unperm_ref[...], o_ref[...].reshape(tm, D_MODEL))

    dil1, dil2 = DILATED_PATTERNS[1][1], DILATED_PATTERNS[2][1]
    lses = (l0_ref[0], natural_lse(l1_ref, lse1_ref, dil1), natural_lse(l2_ref, lse2_ref, dil2))
    outs = (o0_ref[0].astype(F32), natural_out(o1_ref, unperm1_ref), natural_out(o2_ref, unperm2_ref))
    m = jnp.maximum(jnp.maximum(lses[0], lses[1]), lses[2])
    es = [jnp.exp(l - m) for l in lses]
    inv = 1.0 / (es[0] + es[1] + es[2])
    ws = [e * inv for e in es]
    for hd in range(N_HEADS):
        cols = slice(hd * HEAD_DIM, (hd + 1) * HEAD_DIM)
        lane = hd * LSE_LANES_PER_HEAD
        mixed = None
        for w, o in zip(ws, outs):
            term = jnp.broadcast_to(w[:, lane:lane + 1], (tm, HEAD_DIM)) * o[:, cols]
            mixed = term if mixed is None else mixed + term
        mix_ref[:, cols] = mixed.astype(BF16)
    out_ref[...] = h_ref[...] + _dot(mix_ref[...], wo_ref[...])


def _combine(h, outs, lses, w_o, j, *, tm=512):
    b, s, d = h.shape
    tok = pl.BlockSpec((None, tm, d), lambda bi, ti: (bi, ti, 0))
    specs, unperms = [], []
    for (_, dil) in DILATED_PATTERNS:
        for width in (d, HEAD_DIM):
            specs.append(pl.BlockSpec((None, dil, tm // dil, width), lambda bi, ti: (bi, 0, ti, 0)))
        if dil > 1:
            unperms.append(_dilation_perm(tm, dil).T)
    args = [a for pair in zip(outs, lses) for a in pair]
    return pl.pallas_call(
        functools.partial(_combine_kernel, tm=tm),
        grid=(b, s // tm),
        in_specs=[tok] + specs + [_resident((tm, tm))] * len(unperms) + [_layer(w_o, j)],
        out_specs=tok,
        out_shape=jax.ShapeDtypeStruct(h.shape, F32),
        scratch_shapes=[pltpu.VMEM((tm, HEAD_DIM), F32), pltpu.VMEM((tm, HEAD_DIM), F32),
                        pltpu.VMEM((tm, d), BF16)],
        compiler_params=_params(("parallel", "parallel")),
        name="attn_combine",
    )(h, *args, *unperms, w_o)


def _attention_mixer(h, cos_t, sin_t, g, layer, w_qkv, w_o, j):
    b, s, d = h.shape
    outs, lses = [], []
    for group, (window, dil) in enumerate(DILATED_PATTERNS):
        assert window // dil == ATTN_BLOCK
        q, k, v = _qkv(h, cos_t, sin_t, g, layer, w_qkv, j, group, dil)
        sub = s // dil
        o_g, l_g = _band_attention(q.reshape(b * dil, sub, d), k.reshape(b * dil, sub, d),
                                   v.reshape(b * dil, sub, d))
        outs.append(o_g.reshape(b, dil, sub, d))
        lses.append(l_g.reshape(b, dil, sub, HEAD_DIM))
    return _combine(h, outs, lses, w_o, j)


def kernel(x, p, positions, norm_mix, norm_mlp, norm_ple, norm_final, sc_w_in, sc_w_conv, sc_w_out,
           attn_w_qkv, attn_w_o, lru_w_in, lru_conv_w, lru_conv_b, lru_w_a, lru_b_a, lru_w_x,
           lru_b_x, lru_lambda, lru_w_out, mlp_w_up, mlp_w_down, ple_w_gate, ple_w_proj):
    bf = lambda w: w.astype(BF16)
    norm_mix, norm_mlp, norm_ple = _rows(norm_mix), _rows(norm_mlp), _rows(norm_ple)
    sc_w_in, sc_w_out = bf(sc_w_in), bf(sc_w_out)
    attn_w_qkv, attn_w_o = bf(attn_w_qkv), bf(attn_w_o)
    lru_w_in, lru_w_out = bf(lru_w_in), bf(lru_w_out)
    lru_w_ax = bf(0.5 * jnp.concatenate([lru_w_a, lru_w_x], axis=-1))
    lru_conv_b, lru_b_a, lru_b_x, lru_lambda = (_rows(lru_conv_b), _rows(0.5 * lru_b_a),
                                                _rows(0.5 * lru_b_x), _rows(lru_lambda))
    cos_t, sin_t = _rope_tables(positions)

    h = x
    for i in range(DEPTH):
        kind, j = i % N_MIXERS, i // N_MIXERS
        if kind == 0:
            h = _conv_mixer(h, norm_mix, i, sc_w_in, sc_w_conv, sc_w_out, j)
        elif kind == 1:
            h = _attention_mixer(h, cos_t, sin_t, norm_mix, i, attn_w_qkv, attn_w_o, j)
        else:
            h = _lru_mixer(h, norm_mix, i, lru_w_in, lru_conv_w, lru_conv_b, lru_w_ax, lru_b_a,
                           lru_b_x, lru_lambda, lru_w_out, j)
        h = _post(h, p, norm_mlp, norm_ple, norm_final[None], mlp_w_up, mlp_w_down, ple_w_gate,
                  ple_w_proj, i)
    return h
```

```python
import functools
import math

import jax
import jax.numpy as jnp
from jax import lax
from jax.experimental import pallas as pl
from jax.experimental.pallas import tpu as pltpu

D_MODEL = 1024
DEPTH = 4
N_MIXERS = 3
PLE_DIM = 256
D_FF = 4 * D_MODEL
RMS_EPS = 1e-6
SC_WIDTH = 3
HEAD_DIM = 128
N_HEADS = D_MODEL // HEAD_DIM
DILATED_PATTERNS = ((128, 1), (512, 4), (2048, 16))
ROPE_THETA = 500000.0
ROPE_DIM = HEAD_DIM // 4
ROPE_HALF = ROPE_DIM // 2
D_RNN = 1280
N_LRU_BLOCKS = 10
LRU_BLOCK = D_RNN // N_LRU_BLOCKS
LRU_CONV_WIDTH = 4
LRU_C = 8.0

ATTN_BLOCK = 128
LSE_LANES_PER_HEAD = HEAD_DIM // N_HEADS
SUBLANES = 8
HALO = SUBLANES
MASK_VALUE = -1e30
VMEM_LIMIT_BYTES = 56 * 1024 * 1024
WEIGHT_CHUNK = (256, 512)
WEIGHT_SLOTS = 4

F32 = jnp.float32
BF16 = jnp.bfloat16


def _rms(x, g):
    return x * lax.rsqrt(jnp.mean(x * x, axis=-1, keepdims=True) + RMS_EPS) * g


def _dot(a, b):
    return jnp.dot(a, b, preferred_element_type=F32)


def _resident(shape, index=None):
    index = (0,) * len(shape) if index is None else index
    return pl.BlockSpec(shape, lambda *_: index, pipeline_mode=pl.Buffered(1))


def _layer(stacked, layer):
    return _resident((None,) + stacked.shape[1:], (layer, 0, 0))


def _rows(stacked):
    return stacked[:, None, :]


def _params(semantics):
    return pltpu.CompilerParams(dimension_semantics=semantics, vmem_limit_bytes=VMEM_LIMIT_BYTES)


def _load_weights_as_bf16(loads, stage_ref, sem):
    n_slots, chunk_rows, chunk_cols = stage_ref.shape
    chunks = []
    for src, dst in loads:
        rows, cols = dst.shape
        assert rows % chunk_rows == 0 and cols % chunk_cols == 0
        for r0 in range(0, rows, chunk_rows):
            for c0 in range(0, cols, chunk_cols):
                window = (pl.ds(r0, chunk_rows), pl.ds(c0, chunk_cols))
                chunks.append((src.at[window], dst.at[window]))

    def copy(i):
        return pltpu.make_async_copy(chunks[i][0], stage_ref.at[i % n_slots], sem.at[i % n_slots])

    ahead = n_slots - 1
    for i in range(min(ahead, len(chunks))):
        copy(i).start()
    for i, (_, dst) in enumerate(chunks):
        if i + ahead < len(chunks):
            copy(i + ahead).start()
        copy(i).wait()
        dst[...] = stage_ref[i % n_slots].astype(BF16)


def _weight_scratch(*shapes):
    return ([pltpu.VMEM(shape, BF16) for shape in shapes]
            + [pltpu.VMEM((WEIGHT_SLOTS,) + WEIGHT_CHUNK, F32), pltpu.SemaphoreType.DMA((WEIGHT_SLOTS,))])


def _first_grid_step():
    return (pl.program_id(0) == 0) & (pl.program_id(1) == 0)


HBM = pl.BlockSpec(memory_space=pl.ANY)


def _post_kernel(h_ref, p_ref, gm_ref, gp_ref, gf_ref, wup_hbm, wdn_hbm, wg_hbm, wp_hbm, o_ref,
                 wup_ref, wdn_ref, wg_ref, wp_ref, stage_ref, sem, *, layer, tf):
    @pl.when(_first_grid_step())
    def _():
        _load_weights_as_bf16([(wup_hbm.at[layer], wup_ref), (wdn_hbm.at[layer], wdn_ref),
                               (wg_hbm.at[layer], wg_ref), (wp_hbm.at[layer], wp_ref)],
                              stage_ref, sem)

    h = h_ref[...]
    xn = _rms(h, gm_ref[...]).astype(BF16)
    acc = h
    for f in range(D_FF // tf):
        u = _dot(xn, wup_ref[:, f * tf:(f + 1) * tf])
        a = jnp.square(jnp.maximum(u, 0.0)).astype(BF16)
        acc = acc + _dot(a, wdn_ref[f * tf:(f + 1) * tf, :])
    xg = _rms(acc, gp_ref[...]).astype(BF16)
    gate = jax.nn.sigmoid(_dot(xg, wg_ref[...]))
    proj = _dot(p_ref[...].astype(BF16), wp_ref[...])
    out = acc + gate * proj
    if layer == DEPTH - 1:
        out = _rms(out, gf_ref[...])
    o_ref[...] = out


def _post(h, p, g_mlp, g_ple, g_final, w_up, w_down, w_gate, w_proj, layer, *, tm=1024, tf=512):
    b, s, d = h.shape
    tok = pl.BlockSpec((None, tm, d), lambda bi, si: (bi, si, 0))
    p_spec = pl.BlockSpec((None, None, tm, PLE_DIM), lambda bi, si: (layer, bi, si, 0))
    return pl.pallas_call(
        functools.partial(_post_kernel, layer=layer, tf=tf),
        grid=(b, s // tm),
        in_specs=[tok, p_spec, _layer(g_mlp, layer), _layer(g_ple, layer), _resident((1, d)),
                  HBM, HBM, HBM, HBM],
        out_specs=tok,
        out_shape=jax.ShapeDtypeStruct(h.shape, F32),
        scratch_shapes=_weight_scratch(w_up.shape[1:], w_down.shape[1:], w_gate.shape[1:],
                                       w_proj.shape[1:]),
        compiler_params=_params(("arbitrary", "arbitrary")),
        name="mlp_ple",
    )(h, p, g_mlp, g_ple, g_final, w_up, w_down, w_gate, w_proj)


def _causal_taps(buf_ref, x, w, tm):
    k_width = w.shape[0]
    buf_ref[HALO:HALO + tm, :] = x
    out = w[k_width - 1:k_width] * x
    for back in range(1, k_width):
        out = out + w[k_width - 1 - back:k_width - back] * buf_ref[HALO - back:HALO - back + tm, :]
    buf_ref[0:HALO, :] = buf_ref[tm:tm + HALO, :]
    return out


def _conv_mixer_kernel(h_ref, gn_ref, win_hbm, wconv_ref, wout_hbm, o_ref, buf_ref,
                       win_ref, wout_ref, stage_ref, sem, *, j, tm):
    @pl.when(_first_grid_step())
    def _():
        _load_weights_as_bf16([(win_hbm.at[j], win_ref), (wout_hbm.at[j], wout_ref)], stage_ref, sem)

    @pl.when(pl.program_id(1) == 0)
    def _():
        buf_ref[0:HALO, :] = jnp.zeros((HALO, D_MODEL), F32)

    h = h_ref[...]
    xn = _rms(h, gn_ref[...]).astype(BF16)
    gate_b = _dot(xn, win_ref[:, 0:D_MODEL])
    gate_c = _dot(xn, win_ref[:, D_MODEL:2 * D_MODEL])
    xin = _dot(xn, win_ref[:, 2 * D_MODEL:3 * D_MODEL])
    conv = _causal_taps(buf_ref, gate_c * xin, wconv_ref[...], tm)
    y = (gate_b * conv).astype(BF16)
    o_ref[...] = h + _dot(y, wout_ref[...])


def _conv_mixer(h, g, layer, w_in, w_conv, w_out, j, *, tm=1024):
    b, s, d = h.shape
    tok = pl.BlockSpec((None, tm, d), lambda bi, si: (bi, si, 0))
    return pl.pallas_call(
        functools.partial(_conv_mixer_kernel, j=j, tm=tm),
        grid=(b, s // tm),
        in_specs=[tok, _layer(g, layer), HBM, _layer(w_conv, j), HBM],
        out_specs=tok,
        out_shape=jax.ShapeDtypeStruct(h.shape, F32),
        scratch_shapes=[pltpu.VMEM((HALO + tm, d), F32)] + _weight_scratch(w_in.shape[1:], w_out.shape[1:]),
        compiler_params=_params(("arbitrary", "arbitrary")),
        name="conv_mixer",
    )(h, g, w_in, w_conv, w_out)


def _sublane_scan(a, u, row):
    for k in (1, 2, 4):
        keep = row >= k
        a_back = jnp.where(keep, pltpu.roll(a, k, 0), 1.0)
        u_back = jnp.where(keep, pltpu.roll(u, k, 0), 0.0)
        u = a * u_back + u
        a = a * a_back
    return a, u


def _lru_kernel(h_ref, gn_ref, win_hbm, cw_ref, cb_ref, wax_ref, ba_ref, bx_ref, lam_ref, wout_hbm,
                o_ref, slab_ref, xs_ref, tail_ref, carry_ref, y_ref, win_ref, wout_ref, stage_ref, sem,
                *, j, tm):
    chunk = tm // SUBLANES
    pitch = chunk + SUBLANES
    n_back = LRU_CONV_WIDTH - 1
    lead = n_back * SUBLANES
    vreg = lambda i: slice(i * SUBLANES, (i + 1) * SUBLANES)

    @pl.when(_first_grid_step())
    def _():
        _load_weights_as_bf16([(win_hbm.at[j], win_ref), (wout_hbm.at[j], wout_ref)], stage_ref, sem)

    @pl.when(pl.program_id(1) == 0)
    def _():
        tail_ref[...] = jnp.zeros(tail_ref.shape, F32)
        carry_ref[...] = jnp.zeros(carry_ref.shape, F32)

    h = h_ref[...]
    xn = _rms(h, gn_ref[...]).astype(BF16)
    taps = cw_ref[...]
    bias = cb_ref[...]
    half_log_decay = -0.5 * LRU_C * jax.nn.softplus(-lam_ref[...])
    h_prev = carry_ref[...]
    row = lax.broadcasted_iota(jnp.int32, (SUBLANES, LRU_BLOCK), 0)
    block_cols = lambda n: slice(n * LRU_BLOCK, (n + 1) * LRU_BLOCK)

    def conv_and_gate_matmul(n, x_nat):
        cols = block_cols(n)
        for s in range(SUBLANES):
            slab_ref[n, s * pitch:s * pitch + chunk, :] = x_nat[s * chunk:(s + 1) * chunk, :]
        for i in range(chunk):
            xs_ref[n, lead + i * SUBLANES:lead + (i + 1) * SUBLANES, :] = (
                slab_ref[n, pl.ds(i, SUBLANES, stride=pitch), :])
        for k in range(1, n_back + 1):
            cur = xs_ref[n, lead + (chunk - k) * SUBLANES:lead + (chunk - k + 1) * SUBLANES, :]
            prev = tail_ref[n, vreg(n_back - k), :]
            xs_ref[n, lead - k * SUBLANES:lead - (k - 1) * SUBLANES, :] = pltpu.roll(
                jnp.where(row == SUBLANES - 1, prev, cur), 1, 0)
        tail_ref[n] = xs_ref[n, lead + (chunk - n_back) * SUBLANES:lead + chunk * SUBLANES, :]
        xr = bias[:, cols]
        for k in range(LRU_CONV_WIDTH):
            xr = xr + taps[n_back - k:n_back - k + 1, cols] * xs_ref[
                n, lead - k * SUBLANES:lead - k * SUBLANES + tm, :]
        return xr, _dot(xr.astype(BF16), wax_ref[n])

    def recurrence(n, xr, ri, gate):
        cols = block_cols(n)
        t_r = jnp.tanh(ri[:, 0:LRU_BLOCK] + ba_ref[:, cols])
        t_i = jnp.tanh(ri[:, LRU_BLOCK:2 * LRU_BLOCK] + bx_ref[:, cols])
        log_a = (t_r + 1.0) * half_log_decay[:, cols]
        a = jnp.exp(log_a)
        z = -jnp.tanh(log_a) * (a * a + 1.0)
        u = (0.5 * jnp.exp2(0.5 * jnp.log2(z))) * ((t_i + 1.0) * xr)
        h_loc = jnp.zeros((SUBLANES, LRU_BLOCK), F32)
        a_tot = jnp.ones((SUBLANES, LRU_BLOCK), F32)
        for i in range(chunk):
            h_loc = a[vreg(i)] * h_loc + u[vreg(i)]
            a_tot = a_tot * a[vreg(i)]
        a_cum, h_cum = _sublane_scan(a_tot, h_loc, row)
        ends = a_cum * h_prev[:, cols] + h_cum
        h_cur = jnp.where(row == 0, h_prev[:, cols], pltpu.roll(ends, 1, 0))
        for i in range(chunk):
            h_cur = a[vreg(i)] * h_cur + u[vreg(i)]
            slab_ref[n, pl.ds(i, SUBLANES, stride=pitch), :] = h_cur
        for s in range(SUBLANES):
            rows = slice(s * chunk, (s + 1) * chunk)
            hs = slab_ref[n, s * pitch:s * pitch + chunk, :]
            y_ref[rows, cols] = (hs * jax.nn.gelu(gate[rows, :])).astype(BF16)
        return ends[SUBLANES - 1:SUBLANES, :]

    pair_width = 2 * LRU_BLOCK
    n_pairs = N_LRU_BLOCKS // 2
    in_proj = lambda lo: _dot(xn, win_ref[:, lo:lo + pair_width])
    out_proj = lambda p: _dot(y_ref[:, p * pair_width:(p + 1) * pair_width],
                              wout_ref[p * pair_width:(p + 1) * pair_width, :])
    x_pairs = [in_proj(D_RNN), in_proj(D_RNN + pair_width)]
    out = h
    carries = []
    for pair in range(n_pairs):
        blocks = (2 * pair, 2 * pair + 1)
        halves = (slice(0, LRU_BLOCK), slice(LRU_BLOCK, pair_width))
        fronts = [conv_and_gate_matmul(n, x_pairs[pair][:, half]) for n, half in zip(blocks, halves)]
        if pair > 0:
            out = out + out_proj(pair - 1)
        if pair + 2 < n_pairs:
            x_pairs.append(in_proj(D_RNN + (pair + 2) * pair_width))
        gate = in_proj(pair * pair_width)
        for n, half, (xr, ri) in zip(blocks, halves, fronts):
            carries.append(recurrence(n, xr, ri, gate[:, half]))
    out = out + out_proj(n_pairs - 1)

    carry_ref[...] = jnp.concatenate(carries, axis=1)
    o_ref[...] = out


def _lru_mixer(h, g, layer, w_in, conv_w, conv_b, w_ax, b_a, b_x, lam, w_out, j, *, tm=512):
    b, s, d = h.shape
    tok = pl.BlockSpec((None, tm, d), lambda bi, si: (bi, si, 0))
    gates = _resident((None, N_LRU_BLOCKS, LRU_BLOCK, 2 * LRU_BLOCK), (j, 0, 0, 0))
    lead = (LRU_CONV_WIDTH - 1) * SUBLANES
    slab_rows = SUBLANES * (tm // SUBLANES + SUBLANES)
    return pl.pallas_call(
        functools.partial(_lru_kernel, j=j, tm=tm),
        grid=(b, s // tm),
        in_specs=[tok, _layer(g, layer), HBM, _layer(conv_w, j), _layer(conv_b, j), gates,
                  _layer(b_a, j), _layer(b_x, j), _layer(lam, j), HBM],
        out_specs=tok,
        out_shape=jax.ShapeDtypeStruct(h.shape, F32),
        scratch_shapes=[pltpu.VMEM((N_LRU_BLOCKS, slab_rows, LRU_BLOCK), F32),
                        pltpu.VMEM((N_LRU_BLOCKS, lead + tm, LRU_BLOCK), F32),
                        pltpu.VMEM((N_LRU_BLOCKS, lead, LRU_BLOCK), F32),
                        pltpu.VMEM((1, D_RNN), F32), pltpu.VMEM((tm, D_RNN), BF16)]
        + _weight_scratch(w_in.shape[1:], w_out.shape[1:]),
        compiler_params=_params(("arbitrary", "arbitrary")),
        name="rglru_mixer",
    )(h, g, w_in, conv_w, conv_b, w_ax, b_a, b_x, lam, w_out)


def _rope_table_kernel(pos_ref, invf_ref, cos_ref, sin_ref):
    ang = pos_ref[...].astype(F32) * invf_ref[...]
    lane = lax.broadcasted_iota(jnp.int32, ang.shape, 1)
    cos_ref[...] = jnp.cos(ang)
    sin_ref[...] = jnp.where(lane < ROPE_HALF, -1.0, 1.0) * jnp.sin(ang)


def _rope_tables(positions, *, ts=1024):
    b, s = positions.shape
    inv_freq = ROPE_THETA ** (-2.0 * jnp.arange(ROPE_HALF, dtype=F32) / ROPE_DIM)
    invf = jnp.concatenate([inv_freq, inv_freq, jnp.zeros((HEAD_DIM - ROPE_DIM,), F32)])[None]
    out = jax.ShapeDtypeStruct((b, s, HEAD_DIM), F32)
    tab = pl.BlockSpec((None, ts, HEAD_DIM), lambda bi, si: (bi, si, 0))
    return pl.pallas_call(
        _rope_table_kernel,
        grid=(b, s // ts),
        in_specs=[pl.BlockSpec((None, ts, 1), lambda bi, si: (bi, si, 0)), _resident((1, HEAD_DIM))],
        out_specs=[tab, tab],
        out_shape=[out, out],
        compiler_params=_params(("parallel", "parallel")),
        name="rope_tables",
    )(positions[..., None], invf)


def _dilation_perm(tm, dil):
    p = jnp.arange(tm)
    src = (p % (tm // dil)) * dil + p // (tm // dil)
    return (src[:, None] == jnp.arange(tm)[None, :]).astype(BF16)


def _qkv_kernel(h_ref, cos_ref, sin_ref, gn_ref, wqkv_hbm, q_ref, k_ref, v_ref,
                wq_ref, wk_ref, wv_ref, stage_ref, sem, *regroup_scratch, j, group, dil, tm):
    @pl.when(_first_grid_step())
    def _():
        n_groups = len(DILATED_PATTERNS)
        window = lambda which: wqkv_hbm.at[j, :, pl.ds((which * n_groups + group) * D_MODEL, D_MODEL)]
        _load_weights_as_bf16([(window(0), wq_ref), (window(1), wk_ref), (window(2), wv_ref)],
                              stage_ref, sem)

    sub = tm // dil
    xn = _rms(h_ref[...], gn_ref[...])
    if dil == 1:
        cos, sin = cos_ref[...], sin_ref[...]
        xn = xn.astype(BF16)
    else:
        slab_ref, xp_ref = regroup_scratch
        pitch = slab_ref.shape[1] // sub
        regroup = lambda ref: jnp.concatenate(
            [ref[pl.ds(r, sub, stride=dil), :] for r in range(dil)], axis=0)
        cos, sin = regroup(cos_ref), regroup(sin_ref)
        for c in range(D_MODEL // HEAD_DIM):
            cols = slice(c * HEAD_DIM, (c + 1) * HEAD_DIM)
            for i in range(sub if pitch != dil else 1):
                rows = dil if pitch != dil else tm
                slab_ref[c, i * pitch:i * pitch + rows, :] = xn[i * dil:i * dil + rows, cols]
            for r in range(dil):
                xp_ref[r * sub:(r + 1) * sub, cols] = (
                    slab_ref[c, pl.ds(r, sub, stride=pitch), :].astype(BF16))
        xn = xp_ref[...]

    def store(out_ref, cols, val):
        for r in range(dil):
            out_ref[r, :, cols] = val[r * sub:(r + 1) * sub, :]

    low_half = lax.broadcasted_iota(jnp.int32, cos.shape, 1) < ROPE_HALF
    for w_ref, out_ref in ((wq_ref, q_ref), (wk_ref, k_ref)):
        z = _dot(xn, w_ref[...])
        for hd in range(N_HEADS):
            cols = slice(hd * HEAD_DIM, (hd + 1) * HEAD_DIM)
            x = z[:, cols]
            partner = jnp.where(low_half, pltpu.roll(x, HEAD_DIM - ROPE_HALF, 1),
                                pltpu.roll(x, ROPE_HALF, 1))
            store(out_ref, cols, (x * cos + partner * sin).astype(BF16))
    store(v_ref, slice(None), _dot(xn, wv_ref[...]).astype(BF16))


def _qkv(h, cos_t, sin_t, g, layer, w_qkv, j, group, dil, *, tm=1024):
    b, s, d = h.shape
    tok = lambda width: pl.BlockSpec((None, tm, width), lambda bi, ti: (bi, ti, 0))
    out = jax.ShapeDtypeStruct((b, dil, s // dil, d), BF16)
    out_spec = pl.BlockSpec((None, dil, tm // dil, d), lambda bi, ti: (bi, 0, ti, 0))
    pitch = dil + 4 if dil % 8 == 0 else dil
    regroup_scratch = [] if dil == 1 else [
        pltpu.VMEM((d // HEAD_DIM, (tm // dil) * pitch, HEAD_DIM), F32), pltpu.VMEM((tm, d), BF16)]
    return pl.pallas_call(
        functools.partial(_qkv_kernel, j=j, group=group, dil=dil, tm=tm),
        grid=(b, s // tm),
        in_specs=[tok(d), tok(HEAD_DIM), tok(HEAD_DIM), _layer(g, layer), HBM],
        out_specs=[out_spec] * 3,
        out_shape=[out] * 3,
        scratch_shapes=_weight_scratch((d, d), (d, d), (d, d)) + regroup_scratch,
        compiler_params=_params(("arbitrary", "arbitrary")),
        name=f"qkv_dil{dil}",
    )(h, cos_t, sin_t, g, w_qkv)


def _attn_kernel(q_ref, k_ref, v_ref, o_ref, lse_ref, kbuf_ref, vbuf_ref, m_ref, l_ref,
                 *, n_seq, n_qblocks, whole_sequence):
    step = pl.program_id(1)
    blk = ATTN_BLOCK
    tq = n_qblocks * blk
    vcols = lambda hd: slice(2 * hd * HEAD_DIM, (2 * hd + 1) * HEAD_DIM)

    @pl.when(_first_grid_step())
    def _():
        kbuf_ref[...] = jnp.zeros(kbuf_ref.shape, BF16)
        vbuf_ref[...] = jnp.ones(vbuf_ref.shape, BF16)

    qi = lax.broadcasted_iota(jnp.int32, (blk, 2 * blk), 0)
    kj = lax.broadcasted_iota(jnp.int32, (blk, 2 * blk), 1)
    rel = blk + qi - kj
    band = (rel >= 0) & (rel <= blk)
    after_start = False if whole_sequence else step > 0
    band_first = band & ((kj >= blk) | after_start)
    scale2 = HEAD_DIM ** -0.5 * math.log2(math.e)

    for sq in range(n_seq):
        kbuf_ref[sq, blk:blk + tq, :] = k_ref[sq]
        for hd in range(N_HEADS):
            vbuf_ref[sq, blk:blk + tq, vcols(hd)] = v_ref[sq, :, hd * HEAD_DIM:(hd + 1) * HEAD_DIM]
        for qb in range(n_qblocks):
            rows = slice(qb * blk, (qb + 1) * blk)
            keys = slice(qb * blk, (qb + 2) * blk)
            valid = band_first if qb == 0 else band
            for hd in range(N_HEADS):
                cols = slice(hd * HEAD_DIM, (hd + 1) * HEAD_DIM)
                stat = slice(hd * LSE_LANES_PER_HEAD, (hd + 1) * LSE_LANES_PER_HEAD)
                s2 = lax.dot_general(q_ref[sq, rows, cols], kbuf_ref[sq, keys, cols],
                                     (((1,), (1,)), ((), ())), preferred_element_type=F32) * scale2
                s2 = jnp.where(valid, s2, MASK_VALUE)
                m2 = jnp.max(s2, axis=-1, keepdims=True)
                p = jnp.exp2(s2 - m2)
                ol = _dot(p.astype(BF16),
                          vbuf_ref[sq, keys, 2 * hd * HEAD_DIM:2 * (hd + 1) * HEAD_DIM])
                l = ol[:, HEAD_DIM:]
                o_ref[sq, rows, cols] = (ol[:, :HEAD_DIM] * (1.0 / l)).astype(BF16)
                m_ref[sq, rows, stat] = jnp.broadcast_to(m2, (blk, LSE_LANES_PER_HEAD))
                l_ref[sq, rows, stat] = l[:, stat]
        if not whole_sequence:
            kbuf_ref[sq, 0:blk, :] = kbuf_ref[sq, tq:tq + blk, :]
            vbuf_ref[sq, 0:blk, :] = vbuf_ref[sq, tq:tq + blk, :]
    lse_ref[...] = (m_ref[...] + jnp.log2(l_ref[...])) * math.log(2.0)


def _band_attention(q, k, v, *, units_per_step=64):
    n, sub, d = q.shape
    blocks_per_step = units_per_step // N_HEADS
    n_qblocks = min(blocks_per_step, sub // ATTN_BLOCK)
    n_seq = blocks_per_step // n_qblocks
    tq = n_qblocks * ATTN_BLOCK
    tok = lambda width: pl.BlockSpec((n_seq, tq, width), lambda ni, ti: (ni, ti, 0))
    return pl.pallas_call(
        functools.partial(_attn_kernel, n_seq=n_seq, n_qblocks=n_qblocks, whole_sequence=(tq == sub)),
        grid=(n // n_seq, sub // tq),
        in_specs=[tok(d)] * 3,
        out_specs=[tok(d), tok(HEAD_DIM)],
        out_shape=[jax.ShapeDtypeStruct((n, sub, d), BF16),
                   jax.ShapeDtypeStruct((n, sub, HEAD_DIM), F32)],
        scratch_shapes=[pltpu.VMEM((n_seq, ATTN_BLOCK + tq, d), BF16),
                        pltpu.VMEM((n_seq, ATTN_BLOCK + tq, 2 * d), BF16),
                        pltpu.VMEM((n_seq, tq, HEAD_DIM), F32), pltpu.VMEM((n_seq, tq, HEAD_DIM), F32)],
        compiler_params=_params(("arbitrary", "arbitrary")),
        name=f"band_attn_len{sub}",
    )(q, k, v)


def _combine_kernel(h_ref, o0_ref, l0_ref, o1_ref, l1_ref, o2_ref, l2_ref, unperm1_ref, unperm2_ref,
                    wo_hbm, out_ref, lse1_ref, lse2_ref, mix_ref, wo_ref, stage_ref, sem, *, j, tm):
    @pl.when(_first_grid_step())
    def _():
        _load_weights_as_bf16([(wo_hbm.at[j], wo_ref)], stage_ref, sem)

    def natural_lse(l_ref, nat_ref, dil):
        for r in range(dil):
            nat_ref[pl.ds(r, tm // dil, stride=dil), :] = l_ref[r]
        return nat_ref[...]

    def natural_out(o_ref, unperm_ref):
        return _dot(unperm_ref[...], o_ref[...].reshape(tm, D_MODEL))

    dil1, dil2 = DILATED_PATTERNS[1][1], DILATED_PATTERNS[2][1]
    lses = (l0_ref[0], natural_lse(l1_ref, lse1_ref, dil1), natural_lse(l2_ref, lse2_ref, dil2))
    outs = (o0_ref[0].astype(F32), natural_out(o1_ref, unperm1_ref), natural_out(o2_ref, unperm2_ref))
    m = jnp.maximum(jnp.maximum(lses[0], lses[1]), lses[2])
    es = [jnp.exp(l - m) for l in lses]
    inv = 1.0 / (es[0] + es[1] + es[2])
    ws = [e * inv for e in es]
    for hd in range(N_HEADS):
        cols = slice(hd * HEAD_DIM, (hd + 1) * HEAD_DIM)
        lane = hd * LSE_LANES_PER_HEAD
        mixed = None
        for w, o in zip(ws, outs):
            term = jnp.broadcast_to(w[:, lane:lane + 1], (tm, HEAD_DIM)) * o[:, cols]
            mixed = term if mixed is None else mixed + term
        mix_ref[:, cols] = mixed.astype(BF16)
    out_ref[...] = h_ref[...] + _dot(mix_ref[...], wo_ref[...])


def _combine(h, outs, lses, w_o, j, *, tm=512):
    b, s, d = h.shape
    tok = pl.BlockSpec((None, tm, d), lambda bi, ti: (bi, ti, 0))
    specs, unperms = [], []
    for (_, dil) in DILATED_PATTERNS:
        for width in (d, HEAD_DIM):
            specs.append(pl.BlockSpec((None, dil, tm // dil, width), lambda bi, ti: (bi, 0, ti, 0)))
        if dil > 1:
            unperms.append(_dilation_perm(tm, dil).T)
    args = [a for pair in zip(outs, lses) for a in pair]
    return pl.pallas_call(
        functools.partial(_combine_kernel, j=j, tm=tm),
        grid=(b, s // tm),
        in_specs=[tok] + specs + [_resident((tm, tm))] * len(unperms) + [HBM],
        out_specs=tok,
        out_shape=jax.ShapeDtypeStruct(h.shape, F32),
        scratch_shapes=[pltpu.VMEM((tm, HEAD_DIM), F32), pltpu.VMEM((tm, HEAD_DIM), F32),
                        pltpu.VMEM((tm, d), BF16)] + _weight_scratch(w_o.shape[1:]),
        compiler_params=_params(("arbitrary", "arbitrary")),
        name="attn_combine",
    )(h, *args, *unperms, w_o)


def _attention_mixer(h, cos_t, sin_t, g, layer, w_qkv, w_o, j):
    b, s, d = h.shape
    outs, lses = [], []
    for group, (window, dil) in enumerate(DILATED_PATTERNS):
        assert window // dil == ATTN_BLOCK
        q, k, v = _qkv(h, cos_t, sin_t, g, layer, w_qkv, j, group, dil)
        sub = s // dil
        o_g, l_g = _band_attention(q.reshape(b * dil, sub, d), k.reshape(b * dil, sub, d),
                                   v.reshape(b * dil, sub, d))
        outs.append(o_g.reshape(b, dil, sub, d))
        lses.append(l_g.reshape(b, dil, sub, HEAD_DIM))
    return _combine(h, outs, lses, w_o, j)


def kernel(x, p, positions, norm_mix, norm_mlp, norm_ple, norm_final, sc_w_in, sc_w_conv, sc_w_out,
           attn_w_qkv, attn_w_o, lru_w_in, lru_conv_w, lru_conv_b, lru_w_a, lru_b_a, lru_w_x,
           lru_b_x, lru_lambda, lru_w_out, mlp_w_up, mlp_w_down, ple_w_gate, ple_w_proj):
    norm_mix, norm_mlp, norm_ple = _rows(norm_mix), _rows(norm_mlp), _rows(norm_ple)
    lru_w_ax = (0.5 * jnp.concatenate([lru_w_a, lru_w_x], axis=-1)).astype(BF16)
    lru_conv_b, lru_b_a, lru_b_x, lru_lambda = (_rows(lru_conv_b), _rows(0.5 * lru_b_a),
                                                _rows(0.5 * lru_b_x), _rows(lru_lambda))
    cos_t, sin_t = _rope_tables(positions)

    h = x
    for i in range(DEPTH):
        kind, j = i % N_MIXERS, i // N_MIXERS
        if kind == 0:
            h = _conv_mixer(h, norm_mix, i, sc_w_in, sc_w_conv, sc_w_out, j)
        elif kind == 1:
            h = _attention_mixer(h, cos_t, sin_t, norm_mix, i, attn_w_qkv, attn_w_o, j)
        else:
            h = _lru_mixer(h, norm_mix, i, lru_w_in, lru_conv_w, lru_conv_b, lru_w_ax, lru_b_a,
                           lru_b_x, lru_lambda, lru_w_out, j)
        h = _post(h, p, norm_mlp, norm_ple, norm_final[None], mlp_w_up, mlp_w_down, ple_w_gate,
                  ple_w_proj, i)
    return h
```

```python
import functools
import math

import jax
import jax.numpy as jnp
from jax import lax
from jax.experimental import pallas as pl
from jax.experimental.pallas import tpu as pltpu

D_MODEL = 1024
DEPTH = 4
N_MIXERS = 3
PLE_DIM = 256
D_FF = 4 * D_MODEL
RMS_EPS = 1e-6
SC_WIDTH = 3
HEAD_DIM = 128
N_HEADS = D_MODEL // HEAD_DIM
DILATED_PATTERNS = ((128, 1), (512, 4), (2048, 16))
ROPE_THETA = 500000.0
ROPE_DIM = HEAD_DIM // 4
ROPE_HALF = ROPE_DIM // 2
D_RNN = 1280
N_LRU_BLOCKS = 10
LRU_BLOCK = D_RNN // N_LRU_BLOCKS
LRU_CONV_WIDTH = 4
LRU_C = 8.0

ATTN_BLOCK = 128
LSE_LANES_PER_HEAD = HEAD_DIM // N_HEADS
SUBLANES = 8
HALO = SUBLANES
MASK_VALUE = -1e30
VMEM_LIMIT_BYTES = 56 * 1024 * 1024
WEIGHT_CHUNK = (512, 1024)
WEIGHT_SLOTS = 4

F32 = jnp.float32
BF16 = jnp.bfloat16


def _rms(x, g):
    return x * lax.rsqrt(jnp.mean(x * x, axis=-1, keepdims=True) + RMS_EPS) * g


def _dot(a, b):
    return jnp.dot(a, b, preferred_element_type=F32)


def _resident(shape, index=None):
    index = (0,) * len(shape) if index is None else index
    return pl.BlockSpec(shape, lambda *_: index, pipeline_mode=pl.Buffered(1))


def _layer(stacked, layer):
    return _resident((None,) + stacked.shape[1:], (layer, 0, 0))


def _rows(stacked):
    return stacked[:, None, :]


def _params(semantics):
    return pltpu.CompilerParams(dimension_semantics=semantics, vmem_limit_bytes=VMEM_LIMIT_BYTES)


def _load_weights_as_bf16(loads, stage_ref, sem):
    n_slots, stage_rows, chunk_cols = stage_ref.shape
    chunks = []
    for src, dst in loads:
        rows, cols = dst.shape
        chunk_rows = min(rows, stage_rows)
        assert rows % chunk_rows == 0 and cols % chunk_cols == 0
        for r0 in range(0, rows, chunk_rows):
            for c0 in range(0, cols, chunk_cols):
                window = (pl.ds(r0, chunk_rows), pl.ds(c0, chunk_cols))
                chunks.append((src.at[window], dst.at[window], chunk_rows))

    def copy(i):
        src, _, chunk_rows = chunks[i]
        slot = i % n_slots
        return pltpu.make_async_copy(src, stage_ref.at[slot, pl.ds(0, chunk_rows)], sem.at[slot])

    ahead = n_slots - 1
    for i in range(min(ahead, len(chunks))):
        copy(i).start()
    for i, (_, dst, chunk_rows) in enumerate(chunks):
        if i + ahead < len(chunks):
            copy(i + ahead).start()
        copy(i).wait()
        dst[...] = stage_ref[i % n_slots, 0:chunk_rows, :].astype(BF16)


def _weight_scratch(*shapes):
    return ([pltpu.VMEM(shape, BF16) for shape in shapes]
            + [pltpu.VMEM((WEIGHT_SLOTS,) + WEIGHT_CHUNK, F32), pltpu.SemaphoreType.DMA((WEIGHT_SLOTS,))])


def _first_grid_step():
    return (pl.program_id(0) == 0) & (pl.program_id(1) == 0)


HBM = pl.BlockSpec(memory_space=pl.ANY)


def _post_kernel(h_ref, p_ref, gm_ref, gp_ref, gf_ref, wup_hbm, wdn_hbm, wg_hbm, wp_hbm, o_ref,
                 wup_ref, wdn_ref, wg_ref, wp_ref, stage_ref, sem, *, layer, tf):
    @pl.when(_first_grid_step())
    def _():
        _load_weights_as_bf16([(wup_hbm.at[layer], wup_ref), (wdn_hbm.at[layer], wdn_ref),
                               (wg_hbm.at[layer], wg_ref), (wp_hbm.at[layer], wp_ref)],
                              stage_ref, sem)

    h = h_ref[...]
    xn = _rms(h, gm_ref[...]).astype(BF16)
    acc = h
    for f in range(D_FF // tf):
        u = _dot(xn, wup_ref[:, f * tf:(f + 1) * tf])
        a = jnp.square(jnp.maximum(u, 0.0)).astype(BF16)
        acc = acc + _dot(a, wdn_ref[f * tf:(f + 1) * tf, :])
    xg = _rms(acc, gp_ref[...]).astype(BF16)
    gate = jax.nn.sigmoid(_dot(xg, wg_ref[...]))
    proj = _dot(p_ref[...].astype(BF16), wp_ref[...])
    out = acc + gate * proj
    if layer == DEPTH - 1:
        out = _rms(out, gf_ref[...])
    o_ref[...] = out


def _post(h, p, g_mlp, g_ple, g_final, w_up, w_down, w_gate, w_proj, layer, *, tm=1024, tf=512):
    b, s, d = h.shape
    tok = pl.BlockSpec((None, tm, d), lambda bi, si: (bi, si, 0))
    p_spec = pl.BlockSpec((None, None, tm, PLE_DIM), lambda bi, si: (layer, bi, si, 0))
    return pl.pallas_call(
        functools.partial(_post_kernel, layer=layer, tf=tf),
        grid=(b, s // tm),
        in_specs=[tok, p_spec, _layer(g_mlp, layer), _layer(g_ple, layer), _resident((1, d)),
                  HBM, HBM, HBM, HBM],
        out_specs=tok,
        out_shape=jax.ShapeDtypeStruct(h.shape, F32),
        scratch_shapes=_weight_scratch(w_up.shape[1:], w_down.shape[1:], w_gate.shape[1:],
                                       w_proj.shape[1:]),
        compiler_params=_params(("arbitrary", "arbitrary")),
        name="mlp_ple",
    )(h, p, g_mlp, g_ple, g_final, w_up, w_down, w_gate, w_proj)


def _causal_taps(buf_ref, x, w, tm):
    k_width = w.shape[0]
    buf_ref[HALO:HALO + tm, :] = x
    out = w[k_width - 1:k_width] * x
    for back in range(1, k_width):
        out = out + w[k_width - 1 - back:k_width - back] * buf_ref[HALO - back:HALO - back + tm, :]
    buf_ref[0:HALO, :] = buf_ref[tm:tm + HALO, :]
    return out


def _conv_mixer_kernel(h_ref, gn_ref, win_ref, wconv_ref, wout_ref, o_ref, buf_ref, *, tm):
    @pl.when(pl.program_id(1) == 0)
    def _():
        buf_ref[0:HALO, :] = jnp.zeros((HALO, D_MODEL), F32)

    h = h_ref[...]
    xn = _rms(h, gn_ref[...]).astype(BF16)
    gate_b = _dot(xn, win_ref[:, 0:D_MODEL])
    gate_c = _dot(xn, win_ref[:, D_MODEL:2 * D_MODEL])
    xin = _dot(xn, win_ref[:, 2 * D_MODEL:3 * D_MODEL])
    conv = _causal_taps(buf_ref, gate_c * xin, wconv_ref[...], tm)
    y = (gate_b * conv).astype(BF16)
    o_ref[...] = h + _dot(y, wout_ref[...])


def _conv_mixer(h, g, layer, w_in, w_conv, w_out, j, *, tm=1024):
    b, s, d = h.shape
    tok = pl.BlockSpec((None, tm, d), lambda bi, si: (bi, si, 0))
    return pl.pallas_call(
        functools.partial(_conv_mixer_kernel, tm=tm),
        grid=(b, s // tm),
        in_specs=[tok, _layer(g, layer), _layer(w_in, j), _layer(w_conv, j), _layer(w_out, j)],
        out_specs=tok,
        out_shape=jax.ShapeDtypeStruct(h.shape, F32),
        scratch_shapes=[pltpu.VMEM((HALO + tm, d), F32)],
        compiler_params=_params(("parallel", "arbitrary")),
        name="conv_mixer",
    )(h, g, w_in, w_conv, w_out)


def _sublane_scan(a, u, row):
    for k in (1, 2, 4):
        keep = row >= k
        a_back = jnp.where(keep, pltpu.roll(a, k, 0), 1.0)
        u_back = jnp.where(keep, pltpu.roll(u, k, 0), 0.0)
        u = a * u_back + u
        a = a * a_back
    return a, u


def _lru_kernel(h_ref, gn_ref, win_ref, cw_ref, cb_ref, wax_ref, ba_ref, bx_ref, lam_ref, wout_ref,
                o_ref, slab_ref, xs_ref, tail_ref, carry_ref, y_ref, *, tm):
    chunk = tm // SUBLANES
    pitch = chunk + SUBLANES
    n_back = LRU_CONV_WIDTH - 1
    lead = n_back * SUBLANES
    vreg = lambda i: slice(i * SUBLANES, (i + 1) * SUBLANES)

    @pl.when(pl.program_id(1) == 0)
    def _():
        tail_ref[...] = jnp.zeros(tail_ref.shape, F32)
        carry_ref[...] = jnp.zeros(carry_ref.shape, F32)

    h = h_ref[...]
    xn = _rms(h, gn_ref[...]).astype(BF16)
    taps = cw_ref[...]
    bias = cb_ref[...]
    half_log_decay = -0.5 * LRU_C * jax.nn.softplus(-lam_ref[...])
    h_prev = carry_ref[...]
    row = lax.broadcasted_iota(jnp.int32, (SUBLANES, LRU_BLOCK), 0)
    block_cols = lambda n: slice(n * LRU_BLOCK, (n + 1) * LRU_BLOCK)

    def conv_and_gate_matmul(n, x_nat):
        cols = block_cols(n)
        for s in range(SUBLANES):
            slab_ref[n, s * pitch:s * pitch + chunk, :] = x_nat[s * chunk:(s + 1) * chunk, :]
        for i in range(chunk):
            xs_ref[n, lead + i * SUBLANES:lead + (i + 1) * SUBLANES, :] = (
                slab_ref[n, pl.ds(i, SUBLANES, stride=pitch), :])
        for k in range(1, n_back + 1):
            cur = xs_ref[n, lead + (chunk - k) * SUBLANES:lead + (chunk - k + 1) * SUBLANES, :]
            prev = tail_ref[n, vreg(n_back - k), :]
            xs_ref[n, lead - k * SUBLANES:lead - (k - 1) * SUBLANES, :] = pltpu.roll(
                jnp.where(row == SUBLANES - 1, prev, cur), 1, 0)
        tail_ref[n] = xs_ref[n, lead + (chunk - n_back) * SUBLANES:lead + chunk * SUBLANES, :]
        xr = bias[:, cols]
        for k in range(LRU_CONV_WIDTH):
            xr = xr + taps[n_back - k:n_back - k + 1, cols] * xs_ref[
                n, lead - k * SUBLANES:lead - k * SUBLANES + tm, :]
        return xr, _dot(xr.astype(BF16), wax_ref[n])

    def recurrence(n, xr, ri, gate):
        cols = block_cols(n)
        t_r = jnp.tanh(ri[:, 0:LRU_BLOCK] + ba_ref[:, cols])
        t_i = jnp.tanh(ri[:, LRU_BLOCK:2 * LRU_BLOCK] + bx_ref[:, cols])
        log_a = (t_r + 1.0) * half_log_decay[:, cols]
        a = jnp.exp(log_a)
        z = -jnp.tanh(log_a) * (a * a + 1.0)
        u = (0.5 * jnp.exp2(0.5 * jnp.log2(z))) * ((t_i + 1.0) * xr)
        h_loc = jnp.zeros((SUBLANES, LRU_BLOCK), F32)
        a_tot = jnp.ones((SUBLANES, LRU_BLOCK), F32)
        for i in range(chunk):
            h_loc = a[vreg(i)] * h_loc + u[vreg(i)]
            a_tot = a_tot * a[vreg(i)]
        a_cum, h_cum = _sublane_scan(a_tot, h_loc, row)
        ends = a_cum * h_prev[:, cols] + h_cum
        h_cur = jnp.where(row == 0, h_prev[:, cols], pltpu.roll(ends, 1, 0))
        for i in range(chunk):
            h_cur = a[vreg(i)] * h_cur + u[vreg(i)]
            slab_ref[n, pl.ds(i, SUBLANES, stride=pitch), :] = h_cur
        for s in range(SUBLANES):
            rows = slice(s * chunk, (s + 1) * chunk)
            hs = slab_ref[n, s * pitch:s * pitch + chunk, :]
            y_ref[rows, cols] = (hs * jax.nn.gelu(gate[rows, :])).astype(BF16)
        return ends[SUBLANES - 1:SUBLANES, :]

    pair_width = 2 * LRU_BLOCK
    n_pairs = N_LRU_BLOCKS // 2
    in_proj = lambda lo: _dot(xn, win_ref[:, lo:lo + pair_width])
    out_proj = lambda p: _dot(y_ref[:, p * pair_width:(p + 1) * pair_width],
                              wout_ref[p * pair_width:(p + 1) * pair_width, :])
    x_pairs = [in_proj(D_RNN), in_proj(D_RNN + pair_width)]
    out = h
    carries = []
    for pair in range(n_pairs):
        blocks = (2 * pair, 2 * pair + 1)
        halves = (slice(0, LRU_BLOCK), slice(LRU_BLOCK, pair_width))
        fronts = [conv_and_gate_matmul(n, x_pairs[pair][:, half]) for n, half in zip(blocks, halves)]
        if pair > 0:
            out = out + out_proj(pair - 1)
        if pair + 2 < n_pairs:
            x_pairs.append(in_proj(D_RNN + (pair + 2) * pair_width))
        gate = in_proj(pair * pair_width)
        for n, half, (xr, ri) in zip(blocks, halves, fronts):
            carries.append(recurrence(n, xr, ri, gate[:, half]))
    out = out + out_proj(n_pairs - 1)

    carry_ref[...] = jnp.concatenate(carries, axis=1)
    o_ref[...] = out


def _lru_mixer(h, g, layer, w_in, conv_w, conv_b, w_ax, b_a, b_x, lam, w_out, j, *, tm=512):
    b, s, d = h.shape
    tok = pl.BlockSpec((None, tm, d), lambda bi, si: (bi, si, 0))
    gates = _resident((None, N_LRU_BLOCKS, LRU_BLOCK, 2 * LRU_BLOCK), (j, 0, 0, 0))
    lead = (LRU_CONV_WIDTH - 1) * SUBLANES
    slab_rows = SUBLANES * (tm // SUBLANES + SUBLANES)
    return pl.pallas_call(
        functools.partial(_lru_kernel, tm=tm),
        grid=(b, s // tm),
        in_specs=[tok, _layer(g, layer), _layer(w_in, j), _layer(conv_w, j), _layer(conv_b, j), gates,
                  _layer(b_a, j), _layer(b_x, j), _layer(lam, j), _layer(w_out, j)],
        out_specs=tok,
        out_shape=jax.ShapeDtypeStruct(h.shape, F32),
        scratch_shapes=[pltpu.VMEM((N_LRU_BLOCKS, slab_rows, LRU_BLOCK), F32),
                        pltpu.VMEM((N_LRU_BLOCKS, lead + tm, LRU_BLOCK), F32),
                        pltpu.VMEM((N_LRU_BLOCKS, lead, LRU_BLOCK), F32),
                        pltpu.VMEM((1, D_RNN), F32), pltpu.VMEM((tm, D_RNN), BF16)],
        compiler_params=_params(("parallel", "arbitrary")),
        name="rglru_mixer",
    )(h, g, w_in, conv_w, conv_b, w_ax, b_a, b_x, lam, w_out)


def _rope_table_kernel(pos_ref, invf_ref, cos_ref, sin_ref):
    ang = pos_ref[...].astype(F32) * invf_ref[...]
    lane = lax.broadcasted_iota(jnp.int32, ang.shape, 1)
    cos_ref[...] = jnp.cos(ang)
    sin_ref[...] = jnp.where(lane < ROPE_HALF, -1.0, 1.0) * jnp.sin(ang)


def _rope_tables(positions, *, ts=1024):
    b, s = positions.shape
    inv_freq = ROPE_THETA ** (-2.0 * jnp.arange(ROPE_HALF, dtype=F32) / ROPE_DIM)
    invf = jnp.concatenate([inv_freq, inv_freq, jnp.zeros((HEAD_DIM - ROPE_DIM,), F32)])[None]
    out = jax.ShapeDtypeStruct((b, s, HEAD_DIM), F32)
    tab = pl.BlockSpec((None, ts, HEAD_DIM), lambda bi, si: (bi, si, 0))
    return pl.pallas_call(
        _rope_table_kernel,
        grid=(b, s // ts),
        in_specs=[pl.BlockSpec((None, ts, 1), lambda bi, si: (bi, si, 0)), _resident((1, HEAD_DIM))],
        out_specs=[tab, tab],
        out_shape=[out, out],
        compiler_params=_params(("parallel", "parallel")),
        name="rope_tables",
    )(positions[..., None], invf)


def _dilation_perm(tm, dil):
    p = jnp.arange(tm)
    src = (p % (tm // dil)) * dil + p // (tm // dil)
    return (src[:, None] == jnp.arange(tm)[None, :]).astype(BF16)


def _qkv_kernel(h_ref, cos_ref, sin_ref, gn_ref, wq_ref, wk_ref, wv_ref, q_ref, k_ref, v_ref,
                *regroup_scratch, dil, tm):
    sub = tm // dil
    xn = _rms(h_ref[...], gn_ref[...])
    if dil == 1:
        cos, sin = cos_ref[...], sin_ref[...]
        xn = xn.astype(BF16)
    else:
        slab_ref, xp_ref = regroup_scratch
        pitch = slab_ref.shape[1] // sub
        regroup = lambda ref: jnp.concatenate(
            [ref[pl.ds(r, sub, stride=dil), :] for r in range(dil)], axis=0)
        cos, sin = regroup(cos_ref), regroup(sin_ref)
        for c in range(D_MODEL // HEAD_DIM):
            cols = slice(c * HEAD_DIM, (c + 1) * HEAD_DIM)
            for i in range(sub if pitch != dil else 1):
                rows = dil if pitch != dil else tm
                slab_ref[c, i * pitch:i * pitch + rows, :] = xn[i * dil:i * dil + rows, cols]
            for r in range(dil):
                xp_ref[r * sub:(r + 1) * sub, cols] = (
                    slab_ref[c, pl.ds(r, sub, stride=pitch), :].astype(BF16))
        xn = xp_ref[...]

    def store(out_ref, cols, val):
        for r in range(dil):
            out_ref[r, :, cols] = val[r * sub:(r + 1) * sub, :]

    low_half = lax.broadcasted_iota(jnp.int32, cos.shape, 1) < ROPE_HALF
    for w_ref, out_ref in ((wq_ref, q_ref), (wk_ref, k_ref)):
        z = _dot(xn, w_ref[...])
        for hd in range(N_HEADS):
            cols = slice(hd * HEAD_DIM, (hd + 1) * HEAD_DIM)
            x = z[:, cols]
            partner = jnp.where(low_half, pltpu.roll(x, HEAD_DIM - ROPE_HALF, 1),
                                pltpu.roll(x, ROPE_HALF, 1))
            store(out_ref, cols, (x * cos + partner * sin).astype(BF16))
    store(v_ref, slice(None), _dot(xn, wv_ref[...]).astype(BF16))


def _qkv(h, cos_t, sin_t, g, layer, w_qkv, j, group, dil, *, tm=1024):
    b, s, d = h.shape
    n_groups = len(DILATED_PATTERNS)
    tok = lambda width: pl.BlockSpec((None, tm, width), lambda bi, ti: (bi, ti, 0))
    w_spec = lambda which: _resident((None, d, d), (j, 0, which * n_groups + group))
    out = jax.ShapeDtypeStruct((b, dil, s // dil, d), BF16)
    out_spec = pl.BlockSpec((None, dil, tm // dil, d), lambda bi, ti: (bi, 0, ti, 0))
    pitch = dil + 4 if dil % 8 == 0 else dil
    regroup_scratch = [] if dil == 1 else [
        pltpu.VMEM((d // HEAD_DIM, (tm // dil) * pitch, HEAD_DIM), F32), pltpu.VMEM((tm, d), BF16)]
    return pl.pallas_call(
        functools.partial(_qkv_kernel, dil=dil, tm=tm),
        grid=(b, s // tm),
        in_specs=[tok(d), tok(HEAD_DIM), tok(HEAD_DIM), _layer(g, layer), w_spec(0), w_spec(1),
                  w_spec(2)],
        out_specs=[out_spec] * 3,
        out_shape=[out] * 3,
        scratch_shapes=regroup_scratch,
        compiler_params=_params(("parallel", "parallel")),
        name=f"qkv_dil{dil}",
    )(h, cos_t, sin_t, g, w_qkv, w_qkv, w_qkv)


def _attn_kernel(q_ref, k_ref, v_ref, o_ref, lse_ref, kbuf_ref, vbuf_ref, m_ref, l_ref,
                 *, n_seq, n_qblocks, whole_sequence):
    step = pl.program_id(1)
    blk = ATTN_BLOCK
    tq = n_qblocks * blk
    vcols = lambda hd: slice(2 * hd * HEAD_DIM, (2 * hd + 1) * HEAD_DIM)

    @pl.when(_first_grid_step())
    def _():
        kbuf_ref[...] = jnp.zeros(kbuf_ref.shape, BF16)
        vbuf_ref[...] = jnp.ones(vbuf_ref.shape, BF16)

    qi = lax.broadcasted_iota(jnp.int32, (blk, 2 * blk), 0)
    kj = lax.broadcasted_iota(jnp.int32, (blk, 2 * blk), 1)
    rel = blk + qi - kj
    band = (rel >= 0) & (rel <= blk)
    after_start = False if whole_sequence else step > 0
    band_first = band & ((kj >= blk) | after_start)
    scale2 = HEAD_DIM ** -0.5 * math.log2(math.e)

    for sq in range(n_seq):
        kbuf_ref[sq, blk:blk + tq, :] = k_ref[sq]
        for hd in range(N_HEADS):
            vbuf_ref[sq, blk:blk + tq, vcols(hd)] = v_ref[sq, :, hd * HEAD_DIM:(hd + 1) * HEAD_DIM]
        for qb in range(n_qblocks):
            rows = slice(qb * blk, (qb + 1) * blk)
            keys = slice(qb * blk, (qb + 2) * blk)
            valid = band_first if qb == 0 else band
            for hd in range(N_HEADS):
                cols = slice(hd * HEAD_DIM, (hd + 1) * HEAD_DIM)
                stat = slice(hd * LSE_LANES_PER_HEAD, (hd + 1) * LSE_LANES_PER_HEAD)
                s2 = lax.dot_general(q_ref[sq, rows, cols], kbuf_ref[sq, keys, cols],
                                     (((1,), (1,)), ((), ())), preferred_element_type=F32) * scale2
                s2 = jnp.where(valid, s2, MASK_VALUE)
                m2 = jnp.max(s2, axis=-1, keepdims=True)
                p = jnp.exp2(s2 - m2)
                ol = _dot(p.astype(BF16),
                          vbuf_ref[sq, keys, 2 * hd * HEAD_DIM:2 * (hd + 1) * HEAD_DIM])
                l = ol[:, HEAD_DIM:]
                o_ref[sq, rows, cols] = (ol[:, :HEAD_DIM] * (1.0 / l)).astype(BF16)
                m_ref[sq, rows, stat] = jnp.broadcast_to(m2, (blk, LSE_LANES_PER_HEAD))
                l_ref[sq, rows, stat] = l[:, stat]
        if not whole_sequence:
            kbuf_ref[sq, 0:blk, :] = kbuf_ref[sq, tq:tq + blk, :]
            vbuf_ref[sq, 0:blk, :] = vbuf_ref[sq, tq:tq + blk, :]
    lse_ref[...] = (m_ref[...] + jnp.log2(l_ref[...])) * math.log(2.0)


def _band_attention(q, k, v, *, units_per_step=64):
    n, sub, d = q.shape
    blocks_per_step = units_per_step // N_HEADS
    n_qblocks = min(blocks_per_step, sub // ATTN_BLOCK)
    n_seq = blocks_per_step // n_qblocks
    tq = n_qblocks * ATTN_BLOCK
    tok = lambda width: pl.BlockSpec((n_seq, tq, width), lambda ni, ti: (ni, ti, 0))
    return pl.pallas_call(
        functools.partial(_attn_kernel, n_seq=n_seq, n_qblocks=n_qblocks, whole_sequence=(tq == sub)),
        grid=(n // n_seq, sub // tq),
        in_specs=[tok(d)] * 3,
        out_specs=[tok(d), tok(HEAD_DIM)],
        out_shape=[jax.ShapeDtypeStruct((n, sub, d), BF16),
                   jax.ShapeDtypeStruct((n, sub, HEAD_DIM), F32)],
        scratch_shapes=[pltpu.VMEM((n_seq, ATTN_BLOCK + tq, d), BF16),
                        pltpu.VMEM((n_seq, ATTN_BLOCK + tq, 2 * d), BF16),
                        pltpu.VMEM((n_seq, tq, HEAD_DIM), F32), pltpu.VMEM((n_seq, tq, HEAD_DIM), F32)],
        compiler_params=_params(("arbitrary", "arbitrary")),
        name=f"band_attn_len{sub}",
    )(q, k, v)


def _combine_kernel(h_ref, o0_ref, l0_ref, o1_ref, l1_ref, o2_ref, l2_ref, unperm1_ref, unperm2_ref,
                    wo_ref, out_ref, lse1_ref, lse2_ref, mix_ref, *, tm):
    def natural_lse(l_ref, nat_ref, dil):
        for r in range(dil):
            nat_ref[pl.ds(r, tm // dil, stride=dil), :] = l_ref[r]
        return nat_ref[...]

    def natural_out(o_ref, unperm_ref):
        return _dot(unperm_ref[...], o_ref[...].reshape(tm, D_MODEL))

    dil1, dil2 = DILATED_PATTERNS[1][1], DILATED_PATTERNS[2][1]
    lses = (l0_ref[0], natural_lse(l1_ref, lse1_ref, dil1), natural_lse(l2_ref, lse2_ref, dil2))
    outs = (o0_ref[0].astype(F32), natural_out(o1_ref, unperm1_ref), natural_out(o2_ref, unperm2_ref))
    m = jnp.maximum(jnp.maximum(lses[0], lses[1]), lses[2])
    es = [jnp.exp(l - m) for l in lses]
    inv = 1.0 / (es[0] + es[1] + es[2])
    ws = [e * inv for e in es]
    for hd in range(N_HEADS):
        cols = slice(hd * HEAD_DIM, (hd + 1) * HEAD_DIM)
        lane = hd * LSE_LANES_PER_HEAD
        mixed = None
        for w, o in zip(ws, outs):
            term = jnp.broadcast_to(w[:, lane:lane + 1], (tm, HEAD_DIM)) * o[:, cols]
            mixed = term if mixed is None else mixed + term
        mix_ref[:, cols] = mixed.astype(BF16)
    out_ref[...] = h_ref[...] + _dot(mix_ref[...], wo_ref[...])


def _combine(h, outs, lses, w_o, j, *, tm=512):
    b, s, d = h.shape
    tok = pl.BlockSpec((None, tm, d), lambda bi, ti: (bi, ti, 0))
    specs, unperms = [], []
    for (_, dil) in DILATED_PATTERNS:
        for width in (d, HEAD_DIM):
            specs.append(pl.BlockSpec((None, dil, tm // dil, width), lambda bi, ti: (bi, 0, ti, 0)))
        if dil > 1:
            unperms.append(_dilation_perm(tm, dil).T)
    args = [a for pair in zip(outs, lses) for a in pair]
    return pl.pallas_call(
        functools.partial(_combine_kernel, tm=tm),
        grid=(b, s // tm),
        in_specs=[tok] + specs + [_resident((tm, tm))] * len(unperms) + [_layer(w_o, j)],
        out_specs=tok,
        out_shape=jax.ShapeDtypeStruct(h.shape, F32),
        scratch_shapes=[pltpu.VMEM((tm, HEAD_DIM), F32), pltpu.VMEM((tm, HEAD_DIM), F32),
                        pltpu.VMEM((tm, d), BF16)],
        compiler_params=_params(("parallel", "parallel")),
        name="attn_combine",
    )(h, *args, *unperms, w_o)


def _attention_mixer(h, cos_t, sin_t, g, layer, w_qkv, w_o, j):
    b, s, d = h.shape
    outs, lses = [], []
    for group, (window, dil) in enumerate(DILATED_PATTERNS):
        assert window // dil == ATTN_BLOCK
        q, k, v = _qkv(h, cos_t, sin_t, g, layer, w_qkv, j, group, dil)
        sub = s // dil
        o_g, l_g = _band_attention(q.reshape(b * dil, sub, d), k.reshape(b * dil, sub, d),
                                   v.reshape(b * dil, sub, d))
        outs.append(o_g.reshape(b, dil, sub, d))
        lses.append(l_g.reshape(b, dil, sub, HEAD_DIM))
    return _combine(h, outs, lses, w_o, j)


def kernel(x, p, positions, norm_mix, norm_mlp, norm_ple, norm_final, sc_w_in, sc_w_conv, sc_w_out,
           attn_w_qkv, attn_w_o, lru_w_in, lru_conv_w, lru_conv_b, lru_w_a, lru_b_a, lru_w_x,
           lru_b_x, lru_lambda, lru_w_out, mlp_w_up, mlp_w_down, ple_w_gate, ple_w_proj):
    bf = lambda w: w.astype(BF16)
    norm_mix, norm_mlp, norm_ple = _rows(norm_mix), _rows(norm_mlp), _rows(norm_ple)
    sc_w_in, sc_w_out = bf(sc_w_in), bf(sc_w_out)
    attn_w_qkv, attn_w_o = bf(attn_w_qkv), bf(attn_w_o)
    lru_w_in, lru_w_out = bf(lru_w_in), bf(lru_w_out)
    lru_w_ax = bf(0.5 * jnp.concatenate([lru_w_a, lru_w_x], axis=-1))
    lru_conv_b, lru_b_a, lru_b_x, lru_lambda = (_rows(lru_conv_b), _rows(0.5 * lru_b_a),
                                                _rows(0.5 * lru_b_x), _rows(lru_lambda))
    cos_t, sin_t = _rope_tables(positions)

    h = x
    for i in range(DEPTH):
        kind, j = i % N_MIXERS, i // N_MIXERS
        if kind == 0:
            h = _conv_mixer(h, norm_mix, i, sc_w_in, sc_w_conv, sc_w_out, j)
        elif kind == 1:
            h = _attention_mixer(h, cos_t, sin_t, norm_mix, i, attn_w_qkv, attn_w_o, j)
        else:
            h = _lru_mixer(h, norm_mix, i, lru_w_in, lru_conv_w, lru_conv_b, lru_w_ax, lru_b_a,
                           lru_b_x, lru_lambda, lru_w_out, j)
        h = _post(h, p, norm_mlp, norm_ple, norm_final[None], mlp_w_up, mlp_w_down, ple_w_gate,
                  ple_w_proj, i)
    return h
```

```python
import functools
import math

import jax
import jax.numpy as jnp
from jax import lax
from jax.experimental import pallas as pl
from jax.experimental.pallas import tpu as pltpu

D_MODEL = 1024
DEPTH = 4
N_MIXERS = 3
PLE_DIM = 256
D_FF = 4 * D_MODEL
RMS_EPS = 1e-6
SC_WIDTH = 3
HEAD_DIM = 128
N_HEADS = D_MODEL // HEAD_DIM
DILATED_PATTERNS = ((128, 1), (512, 4), (2048, 16))
ROPE_THETA = 500000.0
ROPE_DIM = HEAD_DIM // 4
ROPE_HALF = ROPE_DIM // 2
D_RNN = 1280
N_LRU_BLOCKS = 10
LRU_BLOCK = D_RNN // N_LRU_BLOCKS
LRU_CONV_WIDTH = 4
LRU_C = 8.0

ATTN_BLOCK = 128
LSE_LANES_PER_HEAD = HEAD_DIM // N_HEADS
SUBLANES = 8
HALO = SUBLANES
MASK_VALUE = -1e30
VMEM_LIMIT_BYTES = 56 * 1024 * 1024
WEIGHT_CHUNK = (512, 1024)
WEIGHT_SLOTS = 4

F32 = jnp.float32
BF16 = jnp.bfloat16


def _rms(x, g):
    return x * lax.rsqrt(jnp.mean(x * x, axis=-1, keepdims=True) + RMS_EPS) * g


def _dot(a, b):
    return jnp.dot(a, b, preferred_element_type=F32)


def _resident(shape, index=None):
    index = (0,) * len(shape) if index is None else index
    return pl.BlockSpec(shape, lambda *_: index, pipeline_mode=pl.Buffered(1))


def _layer(stacked, layer):
    return _resident((None,) + stacked.shape[1:], (layer, 0, 0))


def _rows(stacked):
    return stacked[:, None, :]


def _params(semantics):
    return pltpu.CompilerParams(dimension_semantics=semantics, vmem_limit_bytes=VMEM_LIMIT_BYTES)


def _load_weights_as_bf16(loads, stage_ref, sem):
    n_slots, stage_rows, chunk_cols = stage_ref.shape
    chunks = []
    for src, dst in loads:
        rows, cols = dst.shape
        chunk_rows = min(rows, stage_rows)
        assert rows % chunk_rows == 0 and cols % chunk_cols == 0
        for r0 in range(0, rows, chunk_rows):
            for c0 in range(0, cols, chunk_cols):
                window = (pl.ds(r0, chunk_rows), pl.ds(c0, chunk_cols))
                chunks.append((src.at[window], dst.at[window], chunk_rows))

    def copy(i):
        src, _, chunk_rows = chunks[i]
        slot = i % n_slots
        return pltpu.make_async_copy(src, stage_ref.at[slot, pl.ds(0, chunk_rows)], sem.at[slot])

    ahead = n_slots - 1
    for i in range(min(ahead, len(chunks))):
        copy(i).start()
    for i, (_, dst, chunk_rows) in enumerate(chunks):
        if i + ahead < len(chunks):
            copy(i + ahead).start()
        copy(i).wait()
        dst[...] = stage_ref[i % n_slots, 0:chunk_rows, :].astype(BF16)


def _weight_scratch(*shapes):
    return ([pltpu.VMEM(shape, BF16) for shape in shapes]
            + [pltpu.VMEM((WEIGHT_SLOTS,) + WEIGHT_CHUNK, F32), pltpu.SemaphoreType.DMA((WEIGHT_SLOTS,))])


def _first_grid_step():
    return (pl.program_id(0) == 0) & (pl.program_id(1) == 0)


HBM = pl.BlockSpec(memory_space=pl.ANY)


def _post_kernel(h_ref, p_ref, gm_ref, gp_ref, gf_ref, wup_hbm, wdn_hbm, wg_hbm, wp_hbm, o_ref,
                 wup_ref, wdn_ref, wg_ref, wp_ref, stage_ref, sem, *, layer, tf):
    @pl.when(_first_grid_step())
    def _():
        _load_weights_as_bf16([(wup_hbm.at[layer], wup_ref), (wdn_hbm.at[layer], wdn_ref),
                               (wg_hbm.at[layer], wg_ref), (wp_hbm.at[layer], wp_ref)],
                              stage_ref, sem)

    h = h_ref[...]
    xn = _rms(h, gm_ref[...]).astype(BF16)
    acc = h
    for f in range(D_FF // tf):
        u = _dot(xn, wup_ref[:, f * tf:(f + 1) * tf])
        a = jnp.square(jnp.maximum(u, 0.0)).astype(BF16)
        acc = acc + _dot(a, wdn_ref[f * tf:(f + 1) * tf, :])
    xg = _rms(acc, gp_ref[...]).astype(BF16)
    gate = jax.nn.sigmoid(_dot(xg, wg_ref[...]))
    proj = _dot(p_ref[...].astype(BF16), wp_ref[...])
    out = acc + gate * proj
    if layer == DEPTH - 1:
        out = _rms(out, gf_ref[...])
    o_ref[...] = out


def _post(h, p, g_mlp, g_ple, g_final, w_up, w_down, w_gate, w_proj, layer, *, tm=1024, tf=512):
    b, s, d = h.shape
    tok = pl.BlockSpec((None, tm, d), lambda bi, si: (bi, si, 0))
    p_spec = pl.BlockSpec((None, None, tm, PLE_DIM), lambda bi, si: (layer, bi, si, 0))
    return pl.pallas_call(
        functools.partial(_post_kernel, layer=layer, tf=tf),
        grid=(b, s // tm),
        in_specs=[tok, p_spec, _layer(g_mlp, layer), _layer(g_ple, layer), _resident((1, d)),
                  HBM, HBM, HBM, HBM],
        out_specs=tok,
        out_shape=jax.ShapeDtypeStruct(h.shape, F32),
        scratch_shapes=_weight_scratch(w_up.shape[1:], w_down.shape[1:], w_gate.shape[1:],
                                       w_proj.shape[1:]),
        compiler_params=_params(("arbitrary", "arbitrary")),
        name="mlp_ple",
    )(h, p, g_mlp, g_ple, g_final, w_up, w_down, w_gate, w_proj)


def _causal_taps(buf_ref, x, w, tm):
    k_width = w.shape[0]
    buf_ref[HALO:HALO + tm, :] = x
    out = w[k_width - 1:k_width] * x
    for back in range(1, k_width):
        out = out + w[k_width - 1 - back:k_width - back] * buf_ref[HALO - back:HALO - back + tm, :]
    buf_ref[0:HALO, :] = buf_ref[tm:tm + HALO, :]
    return out


def _conv_mixer_kernel(h_ref, gn_ref, win_ref, wconv_ref, wout_ref, o_ref, buf_ref, *, tm):
    @pl.when(pl.program_id(1) == 0)
    def _():
        buf_ref[0:HALO, :] = jnp.zeros((HALO, D_MODEL), F32)

    h = h_ref[...]
    xn = _rms(h, gn_ref[...]).astype(BF16)
    gate_b = _dot(xn, win_ref[:, 0:D_MODEL])
    gate_c = _dot(xn, win_ref[:, D_MODEL:2 * D_MODEL])
    xin = _dot(xn, win_ref[:, 2 * D_MODEL:3 * D_MODEL])
    conv = _causal_taps(buf_ref, gate_c * xin, wconv_ref[...], tm)
    y = (gate_b * conv).astype(BF16)
    o_ref[...] = h + _dot(y, wout_ref[...])


def _conv_mixer(h, g, layer, w_in, w_conv, w_out, j, *, tm=1024):
    b, s, d = h.shape
    tok = pl.BlockSpec((None, tm, d), lambda bi, si: (bi, si, 0))
    return pl.pallas_call(
        functools.partial(_conv_mixer_kernel, tm=tm),
        grid=(b, s // tm),
        in_specs=[tok, _layer(g, layer), _layer(w_in, j), _layer(w_conv, j), _layer(w_out, j)],
        out_specs=tok,
        out_shape=jax.ShapeDtypeStruct(h.shape, F32),
        scratch_shapes=[pltpu.VMEM((HALO + tm, d), F32)],
        compiler_params=_params(("parallel", "arbitrary")),
        name="conv_mixer",
    )(h, g, w_in, w_conv, w_out)


def _sublane_scan(a, u, row):
    for k in (1, 2, 4):
        keep = row >= k
        a_back = jnp.where(keep, pltpu.roll(a, k, 0), 1.0)
        u_back = jnp.where(keep, pltpu.roll(u, k, 0), 0.0)
        u = a * u_back + u
        a = a * a_back
    return a, u


def _lru_kernel(h_ref, gn_ref, win_ref, cw_ref, cb_ref, wax_ref, ba_ref, bx_ref, lam_ref, wout_ref,
                o_ref, slab_ref, xs_ref, tail_ref, carry_ref, y_ref, *, tm):
    chunk = tm // SUBLANES
    pitch = chunk + SUBLANES
    n_back = LRU_CONV_WIDTH - 1
    lead = n_back * SUBLANES
    vreg = lambda i: slice(i * SUBLANES, (i + 1) * SUBLANES)

    @pl.when(pl.program_id(1) == 0)
    def _():
        tail_ref[...] = jnp.zeros(tail_ref.shape, F32)
        carry_ref[...] = jnp.zeros(carry_ref.shape, F32)

    h = h_ref[...]
    xn = _rms(h, gn_ref[...]).astype(BF16)
    taps = cw_ref[...]
    bias = cb_ref[...]
    half_log_decay = -0.5 * LRU_C * jax.nn.softplus(-lam_ref[...])
    h_prev = carry_ref[...]
    row = lax.broadcasted_iota(jnp.int32, (SUBLANES, LRU_BLOCK), 0)
    block_cols = lambda n: slice(n * LRU_BLOCK, (n + 1) * LRU_BLOCK)

    def conv_and_gate_matmul(n, x_nat):
        cols = block_cols(n)
        for s in range(SUBLANES):
            slab_ref[n, s * pitch:s * pitch + chunk, :] = x_nat[s * chunk:(s + 1) * chunk, :]
        for i in range(chunk):
            xs_ref[n, lead + i * SUBLANES:lead + (i + 1) * SUBLANES, :] = (
                slab_ref[n, pl.ds(i, SUBLANES, stride=pitch), :])
        for k in range(1, n_back + 1):
            cur = xs_ref[n, lead + (chunk - k) * SUBLANES:lead + (chunk - k + 1) * SUBLANES, :]
            prev = tail_ref[n, vreg(n_back - k), :]
            xs_ref[n, lead - k * SUBLANES:lead - (k - 1) * SUBLANES, :] = pltpu.roll(
                jnp.where(row == SUBLANES - 1, prev, cur), 1, 0)
        tail_ref[n] = xs_ref[n, lead + (chunk - n_back) * SUBLANES:lead + chunk * SUBLANES, :]
        xr = bias[:, cols]
        for k in range(LRU_CONV_WIDTH):
            xr = xr + taps[n_back - k:n_back - k + 1, cols] * xs_ref[
                n, lead - k * SUBLANES:lead - k * SUBLANES + tm, :]
        return xr, _dot(xr.astype(BF16), wax_ref[n])

    def recurrence(n, xr, ri, gate):
        cols = block_cols(n)
        t_r = jnp.tanh(ri[:, 0:LRU_BLOCK] + ba_ref[:, cols])
        t_i = jnp.tanh(ri[:, LRU_BLOCK:2 * LRU_BLOCK] + bx_ref[:, cols])
        log_a = (t_r + 1.0) * half_log_decay[:, cols]
        a = jnp.exp(log_a)
        z = -jnp.tanh(log_a) * (a * a + 1.0)
        u = (0.5 * jnp.exp2(0.5 * jnp.log2(z))) * ((t_i + 1.0) * xr)
        h_loc = jnp.zeros((SUBLANES, LRU_BLOCK), F32)
        a_tot = jnp.ones((SUBLANES, LRU_BLOCK), F32)
        for i in range(chunk):
            h_loc = a[vreg(i)] * h_loc + u[vreg(i)]
            a_tot = a_tot * a[vreg(i)]
        a_cum, h_cum = _sublane_scan(a_tot, h_loc, row)
        ends = a_cum * h_prev[:, cols] + h_cum
        h_cur = jnp.where(row == 0, h_prev[:, cols], pltpu.roll(ends, 1, 0))
        for i in range(chunk):
            h_cur = a[vreg(i)] * h_cur + u[vreg(i)]
            slab_ref[n, pl.ds(i, SUBLANES, stride=pitch), :] = h_cur
        for s in range(SUBLANES):
            rows = slice(s * chunk, (s + 1) * chunk)
            hs = slab_ref[n, s * pitch:s * pitch + chunk, :]
            y_ref[rows, cols] = (hs * jax.nn.gelu(gate[rows, :])).astype(BF16)
        return ends[SUBLANES - 1:SUBLANES, :]

    pair_width = 2 * LRU_BLOCK
    n_pairs = N_LRU_BLOCKS // 2
    in_proj = lambda lo: _dot(xn, win_ref[:, lo:lo + pair_width])
    out_proj = lambda p: _dot(y_ref[:, p * pair_width:(p + 1) * pair_width],
                              wout_ref[p * pair_width:(p + 1) * pair_width, :])
    x_pairs = [in_proj(D_RNN), in_proj(D_RNN + pair_width)]
    out = h
    carries = []
    for pair in range(n_pairs):
        blocks = (2 * pair, 2 * pair + 1)
        halves = (slice(0, LRU_BLOCK), slice(LRU_BLOCK, pair_width))
        fronts = [conv_and_gate_matmul(n, x_pairs[pair][:, half]) for n, half in zip(blocks, halves)]
        if pair > 0:
            out = out + out_proj(pair - 1)
        if pair + 2 < n_pairs:
            x_pairs.append(in_proj(D_RNN + (pair + 2) * pair_width))
        gate = in_proj(pair * pair_width)
        for n, half, (xr, ri) in zip(blocks, halves, fronts):
            carries.append(recurrence(n, xr, ri, gate[:, half]))
    out = out + out_proj(n_pairs - 1)

    carry_ref[...] = jnp.concatenate(carries, axis=1)
    o_ref[...] = out


def _lru_mixer(h, g, layer, w_in, conv_w, conv_b, w_ax, b_a, b_x, lam, w_out, j, *, tm=512):
    b, s, d = h.shape
    tok = pl.BlockSpec((None, tm, d), lambda bi, si: (bi, si, 0))
    gates = _resident((None, N_LRU_BLOCKS, LRU_BLOCK, 2 * LRU_BLOCK), (j, 0, 0, 0))
    lead = (LRU_CONV_WIDTH - 1) * SUBLANES
    slab_rows = SUBLANES * (tm // SUBLANES + SUBLANES)
    return pl.pallas_call(
        functools.partial(_lru_kernel, tm=tm),
        grid=(b, s // tm),
        in_specs=[tok, _layer(g, layer), _layer(w_in, j), _layer(conv_w, j), _layer(conv_b, j), gates,
                  _layer(b_a, j), _layer(b_x, j), _layer(lam, j), _layer(w_out, j)],
        out_specs=tok,
        out_shape=jax.ShapeDtypeStruct(h.shape, F32),
        scratch_shapes=[pltpu.VMEM((N_LRU_BLOCKS, slab_rows, LRU_BLOCK), F32),
                        pltpu.VMEM((N_LRU_BLOCKS, lead + tm, LRU_BLOCK), F32),
                        pltpu.VMEM((N_LRU_BLOCKS, lead, LRU_BLOCK), F32),
                        pltpu.VMEM((1, D_RNN), F32), pltpu.VMEM((tm, D_RNN), BF16)],
        compiler_params=_params(("parallel", "arbitrary")),
        name="rglru_mixer",
    )(h, g, w_in, conv_w, conv_b, w_ax, b_a, b_x, lam, w_out)


def _rope_table_kernel(pos_ref, invf_ref, sign_ref, cos_ref, sin_ref):
    pack = HEAD_DIM // ROPE_DIM
    rows = pos_ref.shape[0]
    pos = pos_ref[...].astype(F32)
    lane = lax.broadcasted_iota(jnp.int32, (rows, HEAD_DIM), 1)
    packed = pos[:, pack - 1:pack]
    for k in range(pack - 2, -1, -1):
        packed = jnp.where(lane < (k + 1) * ROPE_DIM, pos[:, k:k + 1], packed)
    ang = packed * invf_ref[...]
    cos = jnp.cos(ang)
    sin = sign_ref[...] * jnp.sin(ang)
    for k in range(pack):
        shift = (HEAD_DIM - k * ROPE_DIM) % HEAD_DIM
        cos_k = pltpu.roll(cos, shift, 1) if shift else cos
        sin_k = pltpu.roll(sin, shift, 1) if shift else sin
        cos_ref[pl.ds(k, rows, stride=pack), :] = jnp.where(lane < ROPE_DIM, cos_k, 1.0)
        sin_ref[pl.ds(k, rows, stride=pack), :] = jnp.where(lane < ROPE_DIM, sin_k, 0.0)


def _rope_tables(positions, *, ts=1024):
    b, s = positions.shape
    pack = HEAD_DIM // ROPE_DIM
    inv_freq = ROPE_THETA ** (-2.0 * jnp.arange(ROPE_HALF, dtype=F32) / ROPE_DIM)
    invf = jnp.tile(inv_freq, 2 * pack)[None]
    sign = jnp.tile(jnp.concatenate([-jnp.ones((ROPE_HALF,), F32), jnp.ones((ROPE_HALF,), F32)]), pack)[None]
    out = jax.ShapeDtypeStruct((b, s, HEAD_DIM), F32)
    tab = pl.BlockSpec((None, ts, HEAD_DIM), lambda bi, si: (bi, si, 0))
    return pl.pallas_call(
        _rope_table_kernel,
        grid=(b, s // ts),
        in_specs=[pl.BlockSpec((None, ts // pack, pack), lambda bi, si: (bi, si, 0)),
                  _resident((1, HEAD_DIM)), _resident((1, HEAD_DIM))],
        out_specs=[tab, tab],
        out_shape=[out, out],
        compiler_params=_params(("parallel", "parallel")),
        name="rope_tables",
    )(positions.reshape(b, s // pack, pack), invf, sign)


def _dilation_pitch(dil):
    return dil + 4 if dil % 8 == 0 else dil


def _qkv_kernel(h_ref, cos_ref, sin_ref, gn_ref, wq_ref, wk_ref, wv_ref, q_ref, k_ref, v_ref,
                *regroup_scratch, dil, tm):
    sub = tm // dil
    xn = _rms(h_ref[...], gn_ref[...])
    if dil == 1:
        cos, sin = cos_ref[...], sin_ref[...]
        xn = xn.astype(BF16)
    else:
        slab_ref, xp_ref = regroup_scratch
        pitch = slab_ref.shape[1] // sub
        regroup = lambda ref: jnp.concatenate(
            [ref[pl.ds(r, sub, stride=dil), :] for r in range(dil)], axis=0)
        cos, sin = regroup(cos_ref), regroup(sin_ref)
        for c in range(D_MODEL // HEAD_DIM):
            cols = slice(c * HEAD_DIM, (c + 1) * HEAD_DIM)
            for i in range(sub if pitch != dil else 1):
                rows = dil if pitch != dil else tm
                slab_ref[c, i * pitch:i * pitch + rows, :] = xn[i * dil:i * dil + rows, cols]
            for r in range(dil):
                xp_ref[r * sub:(r + 1) * sub, cols] = (
                    slab_ref[c, pl.ds(r, sub, stride=pitch), :].astype(BF16))
        xn = xp_ref[...]

    def store(out_ref, cols, val):
        for r in range(dil):
            out_ref[r, :, cols] = val[r * sub:(r + 1) * sub, :]

    low_half = lax.broadcasted_iota(jnp.int32, cos.shape, 1) < ROPE_HALF
    for w_ref, out_ref in ((wq_ref, q_ref), (wk_ref, k_ref)):
        z = _dot(xn, w_ref[...])
        for hd in range(N_HEADS):
            cols = slice(hd * HEAD_DIM, (hd + 1) * HEAD_DIM)
            x = z[:, cols]
            partner = jnp.where(low_half, pltpu.roll(x, HEAD_DIM - ROPE_HALF, 1),
                                pltpu.roll(x, ROPE_HALF, 1))
            store(out_ref, cols, (x * cos + partner * sin).astype(BF16))
    store(v_ref, slice(None), _dot(xn, wv_ref[...]).astype(BF16))


def _qkv(h, cos_t, sin_t, g, layer, w_qkv, j, group, dil, *, tm=1024):
    b, s, d = h.shape
    n_groups = len(DILATED_PATTERNS)
    tok = lambda width: pl.BlockSpec((None, tm, width), lambda bi, ti: (bi, ti, 0))
    w_spec = lambda which: _resident((None, d, d), (j, 0, which * n_groups + group))
    out = jax.ShapeDtypeStruct((b, dil, s // dil, d), BF16)
    out_spec = pl.BlockSpec((None, dil, tm // dil, d), lambda bi, ti: (bi, 0, ti, 0))
    pitch = _dilation_pitch(dil)
    regroup_scratch = [] if dil == 1 else [
        pltpu.VMEM((d // HEAD_DIM, (tm // dil) * pitch, HEAD_DIM), F32), pltpu.VMEM((tm, d), BF16)]
    return pl.pallas_call(
        functools.partial(_qkv_kernel, dil=dil, tm=tm),
        grid=(b, s // tm),
        in_specs=[tok(d), tok(HEAD_DIM), tok(HEAD_DIM), _layer(g, layer), w_spec(0), w_spec(1),
                  w_spec(2)],
        out_specs=[out_spec] * 3,
        out_shape=[out] * 3,
        scratch_shapes=regroup_scratch,
        compiler_params=_params(("parallel", "parallel")),
        name=f"qkv_dil{dil}",
    )(h, cos_t, sin_t, g, w_qkv, w_qkv, w_qkv)


def _attn_kernel(q_ref, k_ref, v_ref, o_ref, lse_ref, kbuf_ref, vbuf_ref, m_ref, l_ref,
                 *, n_seq, n_qblocks, whole_sequence):
    step = pl.program_id(1)
    blk = ATTN_BLOCK
    tq = n_qblocks * blk
    vcols = lambda hd: slice(2 * hd * HEAD_DIM, (2 * hd + 1) * HEAD_DIM)

    @pl.when(_first_grid_step())
    def _():
        kbuf_ref[...] = jnp.zeros(kbuf_ref.shape, BF16)
        vbuf_ref[...] = jnp.ones(vbuf_ref.shape, BF16)

    qi = lax.broadcasted_iota(jnp.int32, (blk, 2 * blk), 0)
    kj = lax.broadcasted_iota(jnp.int32, (blk, 2 * blk), 1)
    rel = blk + qi - kj
    band = (rel >= 0) & (rel <= blk)
    after_start = False if whole_sequence else step > 0
    band_first = band & ((kj >= blk) | after_start)
    scale2 = HEAD_DIM ** -0.5 * math.log2(math.e)

    for sq in range(n_seq):
        kbuf_ref[sq, blk:blk + tq, :] = k_ref[sq]
        for hd in range(N_HEADS):
            vbuf_ref[sq, blk:blk + tq, vcols(hd)] = v_ref[sq, :, hd * HEAD_DIM:(hd + 1) * HEAD_DIM]
        for qb in range(n_qblocks):
            rows = slice(qb * blk, (qb + 1) * blk)
            keys = slice(qb * blk, (qb + 2) * blk)
            valid = band_first if qb == 0 else band
            for hd in range(N_HEADS):
                cols = slice(hd * HEAD_DIM, (hd + 1) * HEAD_DIM)
                stat = slice(hd * LSE_LANES_PER_HEAD, (hd + 1) * LSE_LANES_PER_HEAD)
                s2 = lax.dot_general(q_ref[sq, rows, cols], kbuf_ref[sq, keys, cols],
                                     (((1,), (1,)), ((), ())), preferred_element_type=F32) * scale2
                s2 = jnp.where(valid, s2, MASK_VALUE)
                m2 = jnp.max(s2, axis=-1, keepdims=True)
                p = jnp.exp2(s2 - m2)
                ol = _dot(p.astype(BF16),
                          vbuf_ref[sq, keys, 2 * hd * HEAD_DIM:2 * (hd + 1) * HEAD_DIM])
                l = ol[:, HEAD_DIM:]
                o_ref[sq, rows, cols] = (ol[:, :HEAD_DIM] * (1.0 / l)).astype(BF16)
                m_ref[sq, rows, stat] = jnp.broadcast_to(m2, (blk, LSE_LANES_PER_HEAD))
                l_ref[sq, rows, stat] = l[:, stat]
        if not whole_sequence:
            kbuf_ref[sq, 0:blk, :] = kbuf_ref[sq, tq:tq + blk, :]
            vbuf_ref[sq, 0:blk, :] = vbuf_ref[sq, tq:tq + blk, :]
    lse_ref[...] = (m_ref[...] + jnp.log2(l_ref[...])) * math.log(2.0)


def _band_attention(q, k, v, *, units_per_step=64):
    n, sub, d = q.shape
    blocks_per_step = units_per_step // N_HEADS
    n_qblocks = min(blocks_per_step, sub // ATTN_BLOCK)
    n_seq = blocks_per_step // n_qblocks
    tq = n_qblocks * ATTN_BLOCK
    tok = lambda width: pl.BlockSpec((n_seq, tq, width), lambda ni, ti: (ni, ti, 0))
    return pl.pallas_call(
        functools.partial(_attn_kernel, n_seq=n_seq, n_qblocks=n_qblocks, whole_sequence=(tq == sub)),
        grid=(n // n_seq, sub // tq),
        in_specs=[tok(d)] * 3,
        out_specs=[tok(d), tok(HEAD_DIM)],
        out_shape=[jax.ShapeDtypeStruct((n, sub, d), BF16),
                   jax.ShapeDtypeStruct((n, sub, HEAD_DIM), F32)],
        scratch_shapes=[pltpu.VMEM((n_seq, ATTN_BLOCK + tq, d), BF16),
                        pltpu.VMEM((n_seq, ATTN_BLOCK + tq, 2 * d), BF16),
                        pltpu.VMEM((n_seq, tq, HEAD_DIM), F32), pltpu.VMEM((n_seq, tq, HEAD_DIM), F32)],
        compiler_params=_params(("arbitrary", "arbitrary")),
        name=f"band_attn_len{sub}",
    )(q, k, v)


def _combine_kernel(h_ref, o0_ref, l0_ref, o1_ref, l1_ref, o2_ref, l2_ref, wo_ref, out_ref,
                    lse1_ref, lse2_ref, slab1_ref, slab2_ref, mix_ref, *, tm):
    def natural_lse(l_ref, nat_ref, dil):
        for r in range(dil):
            nat_ref[pl.ds(r, tm // dil, stride=dil), :] = l_ref[r]
        return nat_ref[...]

    def natural_out(o_ref, slab_ref, dil, hd):
        sub = tm // dil
        pitch = slab_ref.shape[1] // sub
        cols = slice(hd * HEAD_DIM, (hd + 1) * HEAD_DIM)
        for r in range(dil):
            slab_ref[hd, pl.ds(r, sub, stride=pitch), :] = o_ref[r, :, cols].astype(F32)
        if pitch == dil:
            return slab_ref[hd]
        return jnp.concatenate([slab_ref[hd, i * pitch:i * pitch + dil, :] for i in range(sub)], axis=0)

    dil1, dil2 = DILATED_PATTERNS[1][1], DILATED_PATTERNS[2][1]
    lses = (l0_ref[0], natural_lse(l1_ref, lse1_ref, dil1), natural_lse(l2_ref, lse2_ref, dil2))
    m = jnp.maximum(jnp.maximum(lses[0], lses[1]), lses[2])
    es = [jnp.exp(l - m) for l in lses]
    inv = 1.0 / (es[0] + es[1] + es[2])
    w1, w2 = es[1] * inv, es[2] * inv
    for hd in range(N_HEADS):
        cols = slice(hd * HEAD_DIM, (hd + 1) * HEAD_DIM)
        lane = slice(hd * LSE_LANES_PER_HEAD, hd * LSE_LANES_PER_HEAD + 1)
        o0 = o0_ref[0, :, cols].astype(F32)
        mixed = o0
        for w, o in ((w1, natural_out(o1_ref, slab1_ref, dil1, hd)),
                     (w2, natural_out(o2_ref, slab2_ref, dil2, hd))):
            mixed = mixed + jnp.broadcast_to(w[:, lane], (tm, HEAD_DIM)) * (o - o0)
        mix_ref[:, cols] = mixed.astype(BF16)
    out_ref[...] = h_ref[...] + _dot(mix_ref[...], wo_ref[...])


def _combine(h, outs, lses, w_o, j, *, tm=512):
    b, s, d = h.shape
    tok = pl.BlockSpec((None, tm, d), lambda bi, ti: (bi, ti, 0))
    specs, slabs = [], []
    for (_, dil) in DILATED_PATTERNS:
        for width in (d, HEAD_DIM):
            specs.append(pl.BlockSpec((None, dil, tm // dil, width), lambda bi, ti: (bi, 0, ti, 0)))
        if dil > 1:
            slabs.append(pltpu.VMEM((N_HEADS, (tm // dil) * _dilation_pitch(dil), HEAD_DIM), F32))
    args = [a for pair in zip(outs, lses) for a in pair]
    return pl.pallas_call(
        functools.partial(_combine_kernel, tm=tm),
        grid=(b, s // tm),
        in_specs=[tok] + specs + [_layer(w_o, j)],
        out_specs=tok,
        out_shape=jax.ShapeDtypeStruct(h.shape, F32),
        scratch_shapes=[pltpu.VMEM((tm, HEAD_DIM), F32), pltpu.VMEM((tm, HEAD_DIM), F32)] + slabs
        + [pltpu.VMEM((tm, d), BF16)],
        compiler_params=_params(("parallel", "parallel")),
        name="attn_combine",
    )(h, *args, w_o)


def _attention_mixer(h, cos_t, sin_t, g, layer, w_qkv, w_o, j):
    b, s, d = h.shape
    outs, lses = [], []
    for group, (window, dil) in enumerate(DILATED_PATTERNS):
        assert window // dil == ATTN_BLOCK
        q, k, v = _qkv(h, cos_t, sin_t, g, layer, w_qkv, j, group, dil)
        sub = s // dil
        o_g, l_g = _band_attention(q.reshape(b * dil, sub, d), k.reshape(b * dil, sub, d),
                                   v.reshape(b * dil, sub, d))
        outs.append(o_g.reshape(b, dil, sub, d))
        lses.append(l_g.reshape(b, dil, sub, HEAD_DIM))
    return _combine(h, outs, lses, w_o, j)


def kernel(x, p, positions, norm_mix, norm_mlp, norm_ple, norm_final, sc_w_in, sc_w_conv, sc_w_out,
           attn_w_qkv, attn_w_o, lru_w_in, lru_conv_w, lru_conv_b, lru_w_a, lru_b_a, lru_w_x,
           lru_b_x, lru_lambda, lru_w_out, mlp_w_up, mlp_w_down, ple_w_gate, ple_w_proj):
    bf = lambda w: w.astype(BF16)
    norm_mix, norm_mlp, norm_ple = _rows(norm_mix), _rows(norm_mlp), _rows(norm_ple)
    sc_w_in, sc_w_out = bf(sc_w_in), bf(sc_w_out)
    attn_w_qkv, attn_w_o = bf(attn_w_qkv), bf(attn_w_o)
    lru_w_in, lru_w_out = bf(lru_w_in), bf(lru_w_out)
    lru_w_ax = bf(0.5 * jnp.concatenate([lru_w_a, lru_w_x], axis=-1))
    lru_conv_b, lru_b_a, lru_b_x, lru_lambda = (_rows(lru_conv_b), _rows(0.5 * lru_b_a),
                                                _rows(0.5 * lru_b_x), _rows(lru_lambda))
    cos_t, sin_t = _rope_tables(positions)

    h = x
    for i in range(DEPTH):
        kind, j = i % N_MIXERS, i // N_MIXERS
        if kind == 0:
            h = _conv_mixer(h, norm_mix, i, sc_w_in, sc_w_conv, sc_w_out, j)
        elif kind == 1:
            h = _attention_mixer(h, cos_t, sin_t, norm_mix, i, attn_w_qkv, attn_w_o, j)
        else:
            h = _lru_mixer(h, norm_mix, i, lru_w_in, lru_conv_w, lru_conv_b, lru_w_ax, lru_b_a,
                           lru_b_x, lru_lambda, lru_w_out, j)
        h = _post(h, p, norm_mlp, norm_ple, norm_final[None], mlp_w_up, mlp_w_down, ple_w_gate,
                  ple_w_proj, i)
    return h
```

```python
import functools
import math

import jax
import jax.numpy as jnp
from jax import lax
from jax.experimental import pallas as pl
from jax.experimental.pallas import tpu as pltpu

D_MODEL = 1024
DEPTH = 4
N_MIXERS = 3
PLE_DIM = 256
D_FF = 4 * D_MODEL
RMS_EPS = 1e-6
SC_WIDTH = 3
HEAD_DIM = 128
N_HEADS = D_MODEL // HEAD_DIM
DILATED_PATTERNS = ((128, 1), (512, 4), (2048, 16))
ROPE_THETA = 500000.0
ROPE_DIM = HEAD_DIM // 4
ROPE_HALF = ROPE_DIM // 2
D_RNN = 1280
N_LRU_BLOCKS = 10
LRU_BLOCK = D_RNN // N_LRU_BLOCKS
LRU_CONV_WIDTH = 4
LRU_C = 8.0

ATTN_BLOCK = 128
LSE_LANES_PER_HEAD = HEAD_DIM // N_HEADS
SUBLANES = 8
HALO = SUBLANES
MASK_VALUE = -1e30
SCORE_SCALE = HEAD_DIM ** -0.5 * math.log2(math.e)
VMEM_LIMIT_BYTES = 56 * 1024 * 1024
WEIGHT_CHUNK = (512, 1024)
WEIGHT_SLOTS = 4

F32 = jnp.float32
BF16 = jnp.bfloat16


def _rms(x, g):
    return x * lax.rsqrt(jnp.mean(x * x, axis=-1, keepdims=True) + RMS_EPS) * g


def _dot(a, b):
    return jnp.dot(a, b, preferred_element_type=F32)


def _resident(shape, index=None):
    index = (0,) * len(shape) if index is None else index
    return pl.BlockSpec(shape, lambda *_: index, pipeline_mode=pl.Buffered(1))


def _layer(stacked, layer):
    return _resident((None,) + stacked.shape[1:], (layer, 0, 0))


def _rows(stacked):
    return stacked[:, None, :]


def _params(semantics):
    return pltpu.CompilerParams(dimension_semantics=semantics, vmem_limit_bytes=VMEM_LIMIT_BYTES)


def _load_weights_as_bf16(loads, stage_ref, sem):
    n_slots, stage_rows, chunk_cols = stage_ref.shape
    chunks = []
    for src, dst in loads:
        rows, cols = dst.shape
        chunk_rows = min(rows, stage_rows)
        assert rows % chunk_rows == 0 and cols % chunk_cols == 0
        for r0 in range(0, rows, chunk_rows):
            for c0 in range(0, cols, chunk_cols):
                window = (pl.ds(r0, chunk_rows), pl.ds(c0, chunk_cols))
                chunks.append((src.at[window], dst.at[window], chunk_rows))

    def copy(i):
        src, _, chunk_rows = chunks[i]
        slot = i % n_slots
        return pltpu.make_async_copy(src, stage_ref.at[slot, pl.ds(0, chunk_rows)], sem.at[slot])

    ahead = n_slots - 1
    for i in range(min(ahead, len(chunks))):
        copy(i).start()
    for i, (_, dst, chunk_rows) in enumerate(chunks):
        if i + ahead < len(chunks):
            copy(i + ahead).start()
        copy(i).wait()
        dst[...] = stage_ref[i % n_slots, 0:chunk_rows, :].astype(BF16)


def _weight_scratch(*shapes):
    return ([pltpu.VMEM(shape, BF16) for shape in shapes]
            + [pltpu.VMEM((WEIGHT_SLOTS,) + WEIGHT_CHUNK, F32), pltpu.SemaphoreType.DMA((WEIGHT_SLOTS,))])


def _first_grid_step():
    return (pl.program_id(0) == 0) & (pl.program_id(1) == 0)


HBM = pl.BlockSpec(memory_space=pl.ANY)


def _post_kernel(h_ref, p_ref, gm_ref, gp_ref, gf_ref, wup_hbm, wdn_hbm, wg_hbm, wp_hbm, o_ref,
                 wup_ref, wdn_ref, wg_ref, wp_ref, stage_ref, sem, *, layer, tf):
    @pl.when(_first_grid_step())
    def _():
        _load_weights_as_bf16([(wup_hbm.at[layer], wup_ref), (wdn_hbm.at[layer], wdn_ref),
                               (wg_hbm.at[layer], wg_ref), (wp_hbm.at[layer], wp_ref)],
                              stage_ref, sem)

    h = h_ref[...]
    xn = _rms(h, gm_ref[...]).astype(BF16)
    acc = h
    for f in range(D_FF // tf):
        u = _dot(xn, wup_ref[:, f * tf:(f + 1) * tf])
        a = jnp.square(jnp.maximum(u, 0.0)).astype(BF16)
        acc = acc + _dot(a, wdn_ref[f * tf:(f + 1) * tf, :])
    xg = _rms(acc, gp_ref[...]).astype(BF16)
    gate = jax.nn.sigmoid(_dot(xg, wg_ref[...]))
    proj = _dot(p_ref[...].astype(BF16), wp_ref[...])
    out = acc + gate * proj
    if layer == DEPTH - 1:
        out = _rms(out, gf_ref[...])
    o_ref[...] = out


def _post(h, p, g_mlp, g_ple, g_final, w_up, w_down, w_gate, w_proj, layer, *, tm=1024, tf=512):
    b, s, d = h.shape
    tok = pl.BlockSpec((None, tm, d), lambda bi, si: (bi, si, 0))
    p_spec = pl.BlockSpec((None, None, tm, PLE_DIM), lambda bi, si: (layer, bi, si, 0))
    return pl.pallas_call(
        functools.partial(_post_kernel, layer=layer, tf=tf),
        grid=(b, s // tm),
        in_specs=[tok, p_spec, _layer(g_mlp, layer), _layer(g_ple, layer), _resident((1, d)),
                  HBM, HBM, HBM, HBM],
        out_specs=tok,
        out_shape=jax.ShapeDtypeStruct(h.shape, F32),
        scratch_shapes=_weight_scratch(w_up.shape[1:], w_down.shape[1:], w_gate.shape[1:],
                                       w_proj.shape[1:]),
        compiler_params=_params(("arbitrary", "arbitrary")),
        name="mlp_ple",
    )(h, p, g_mlp, g_ple, g_final, w_up, w_down, w_gate, w_proj)


def _causal_taps(buf_ref, x, w, tm):
    k_width = w.shape[0]
    buf_ref[HALO:HALO + tm, :] = x
    out = w[k_width - 1:k_width] * x
    for back in range(1, k_width):
        out = out + w[k_width - 1 - back:k_width - back] * buf_ref[HALO - back:HALO - back + tm, :]
    buf_ref[0:HALO, :] = buf_ref[tm:tm + HALO, :]
    return out


def _conv_mixer_kernel(h_ref, gn_ref, win_ref, wconv_ref, wout_ref, o_ref, buf_ref, *, tm):
    @pl.when(pl.program_id(1) == 0)
    def _():
        buf_ref[0:HALO, :] = jnp.zeros((HALO, D_MODEL), F32)

    h = h_ref[...]
    xn = _rms(h, gn_ref[...]).astype(BF16)
    gate_b = _dot(xn, win_ref[:, 0:D_MODEL])
    gate_c = _dot(xn, win_ref[:, D_MODEL:2 * D_MODEL])
    xin = _dot(xn, win_ref[:, 2 * D_MODEL:3 * D_MODEL])
    conv = _causal_taps(buf_ref, gate_c * xin, wconv_ref[...], tm)
    y = (gate_b * conv).astype(BF16)
    o_ref[...] = h + _dot(y, wout_ref[...])


def _conv_mixer(h, g, layer, w_in, w_conv, w_out, j, *, tm=1024):
    b, s, d = h.shape
    tok = pl.BlockSpec((None, tm, d), lambda bi, si: (bi, si, 0))
    return pl.pallas_call(
        functools.partial(_conv_mixer_kernel, tm=tm),
        grid=(b, s // tm),
        in_specs=[tok, _layer(g, layer), _layer(w_in, j), _layer(w_conv, j), _layer(w_out, j)],
        out_specs=tok,
        out_shape=jax.ShapeDtypeStruct(h.shape, F32),
        scratch_shapes=[pltpu.VMEM((HALO + tm, d), F32)],
        compiler_params=_params(("parallel", "arbitrary")),
        name="conv_mixer",
    )(h, g, w_in, w_conv, w_out)


def _sublane_scan(a, u, row):
    for k in (1, 2, 4):
        keep = row >= k
        a_back = jnp.where(keep, pltpu.roll(a, k, 0), 1.0)
        u_back = jnp.where(keep, pltpu.roll(u, k, 0), 0.0)
        u = a * u_back + u
        a = a * a_back
    return a, u


def _lru_kernel(h_ref, gn_ref, win_ref, cw_ref, cb_ref, wax_ref, ba_ref, bx_ref, lam_ref, wout_ref,
                o_ref, slab_ref, xs_ref, tail_ref, carry_ref, y_ref, *, tm):
    chunk = tm // SUBLANES
    pitch = chunk + SUBLANES
    n_back = LRU_CONV_WIDTH - 1
    lead = n_back * SUBLANES
    vreg = lambda i: slice(i * SUBLANES, (i + 1) * SUBLANES)

    @pl.when(pl.program_id(1) == 0)
    def _():
        tail_ref[...] = jnp.zeros(tail_ref.shape, F32)
        carry_ref[...] = jnp.zeros(carry_ref.shape, F32)

    h = h_ref[...]
    xn = _rms(h, gn_ref[...]).astype(BF16)
    taps = cw_ref[...]
    bias = cb_ref[...]
    half_log_decay = -0.5 * LRU_C * jax.nn.softplus(-lam_ref[...])
    h_prev = carry_ref[...]
    row = lax.broadcasted_iota(jnp.int32, (SUBLANES, LRU_BLOCK), 0)
    block_cols = lambda n: slice(n * LRU_BLOCK, (n + 1) * LRU_BLOCK)

    def conv_and_gate_matmul(n, x_nat):
        cols = block_cols(n)
        for s in range(SUBLANES):
            slab_ref[n, s * pitch:s * pitch + chunk, :] = x_nat[s * chunk:(s + 1) * chunk, :]
        for i in range(chunk):
            xs_ref[n, lead + i * SUBLANES:lead + (i + 1) * SUBLANES, :] = (
                slab_ref[n, pl.ds(i, SUBLANES, stride=pitch), :])
        for k in range(1, n_back + 1):
            cur = xs_ref[n, lead + (chunk - k) * SUBLANES:lead + (chunk - k + 1) * SUBLANES, :]
            prev = tail_ref[n, vreg(n_back - k), :]
            xs_ref[n, lead - k * SUBLANES:lead - (k - 1) * SUBLANES, :] = pltpu.roll(
                jnp.where(row == SUBLANES - 1, prev, cur), 1, 0)
        tail_ref[n] = xs_ref[n, lead + (chunk - n_back) * SUBLANES:lead + chunk * SUBLANES, :]
        xr = bias[:, cols]
        for k in range(LRU_CONV_WIDTH):
            xr = xr + taps[n_back - k:n_back - k + 1, cols] * xs_ref[
                n, lead - k * SUBLANES:lead - k * SUBLANES + tm, :]
        return xr, _dot(xr.astype(BF16), wax_ref[n])

    def recurrence(n, xr, ri, gate):
        cols = block_cols(n)
        t_r = jnp.tanh(ri[:, 0:LRU_BLOCK] + ba_ref[:, cols])
        t_i = jnp.tanh(ri[:, LRU_BLOCK:2 * LRU_BLOCK] + bx_ref[:, cols])
        log_a = (t_r + 1.0) * half_log_decay[:, cols]
        a = jnp.exp(log_a)
        z = -jnp.tanh(log_a) * (a * a + 1.0)
        u = (0.5 * jnp.exp2(0.5 * jnp.log2(z))) * ((t_i + 1.0) * xr)
        h_loc = jnp.zeros((SUBLANES, LRU_BLOCK), F32)
        a_tot = jnp.ones((SUBLANES, LRU_BLOCK), F32)
        for i in range(chunk):
            h_loc = a[vreg(i)] * h_loc + u[vreg(i)]
            a_tot = a_tot * a[vreg(i)]
        a_cum, h_cum = _sublane_scan(a_tot, h_loc, row)
        ends = a_cum * h_prev[:, cols] + h_cum
        h_cur = jnp.where(row == 0, h_prev[:, cols], pltpu.roll(ends, 1, 0))
        for i in range(chunk):
            h_cur = a[vreg(i)] * h_cur + u[vreg(i)]
            slab_ref[n, pl.ds(i, SUBLANES, stride=pitch), :] = h_cur
        for s in range(SUBLANES):
            rows = slice(s * chunk, (s + 1) * chunk)
            hs = slab_ref[n, s * pitch:s * pitch + chunk, :]
            y_ref[rows, cols] = (hs * jax.nn.gelu(gate[rows, :])).astype(BF16)
        return ends[SUBLANES - 1:SUBLANES, :]

    pair_width = 2 * LRU_BLOCK
    n_pairs = N_LRU_BLOCKS // 2
    in_proj = lambda lo: _dot(xn, win_ref[:, lo:lo + pair_width])
    out_proj = lambda p: _dot(y_ref[:, p * pair_width:(p + 1) * pair_width],
                              wout_ref[p * pair_width:(p + 1) * pair_width, :])
    x_pairs = [in_proj(D_RNN), in_proj(D_RNN + pair_width)]
    out = h
    carries = []
    for pair in range(n_pairs):
        blocks = (2 * pair, 2 * pair + 1)
        halves = (slice(0, LRU_BLOCK), slice(LRU_BLOCK, pair_width))
        fronts = [conv_and_gate_matmul(n, x_pairs[pair][:, half]) for n, half in zip(blocks, halves)]
        if pair > 0:
            out = out + out_proj(pair - 1)
        if pair + 2 < n_pairs:
            x_pairs.append(in_proj(D_RNN + (pair + 2) * pair_width))
        gate = in_proj(pair * pair_width)
        for n, half, (xr, ri) in zip(blocks, halves, fronts):
            carries.append(recurrence(n, xr, ri, gate[:, half]))
    out = out + out_proj(n_pairs - 1)

    carry_ref[...] = jnp.concatenate(carries, axis=1)
    o_ref[...] = out


def _lru_mixer(h, g, layer, w_in, conv_w, conv_b, w_ax, b_a, b_x, lam, w_out, j, *, tm=512):
    b, s, d = h.shape
    tok = pl.BlockSpec((None, tm, d), lambda bi, si: (bi, si, 0))
    gates = _resident((None, N_LRU_BLOCKS, LRU_BLOCK, 2 * LRU_BLOCK), (j, 0, 0, 0))
    lead = (LRU_CONV_WIDTH - 1) * SUBLANES
    slab_rows = SUBLANES * (tm // SUBLANES + SUBLANES)
    return pl.pallas_call(
        functools.partial(_lru_kernel, tm=tm),
        grid=(b, s // tm),
        in_specs=[tok, _layer(g, layer), _layer(w_in, j), _layer(conv_w, j), _layer(conv_b, j), gates,
                  _layer(b_a, j), _layer(b_x, j), _layer(lam, j), _layer(w_out, j)],
        out_specs=tok,
        out_shape=jax.ShapeDtypeStruct(h.shape, F32),
        scratch_shapes=[pltpu.VMEM((N_LRU_BLOCKS, slab_rows, LRU_BLOCK), F32),
                        pltpu.VMEM((N_LRU_BLOCKS, lead + tm, LRU_BLOCK), F32),
                        pltpu.VMEM((N_LRU_BLOCKS, lead, LRU_BLOCK), F32),
                        pltpu.VMEM((1, D_RNN), F32), pltpu.VMEM((tm, D_RNN), BF16)],
        compiler_params=_params(("parallel", "arbitrary")),
        name="rglru_mixer",
    )(h, g, w_in, conv_w, conv_b, w_ax, b_a, b_x, lam, w_out)


def _rope_table_kernel(pos_ref, invf_ref, sign_ref, cos_ref, sin_ref):
    pack = HEAD_DIM // ROPE_DIM
    rows = pos_ref.shape[0]
    pos = pos_ref[...].astype(F32)
    lane = lax.broadcasted_iota(jnp.int32, (rows, HEAD_DIM), 1)
    packed = pos[:, pack - 1:pack]
    for k in range(pack - 2, -1, -1):
        packed = jnp.where(lane < (k + 1) * ROPE_DIM, pos[:, k:k + 1], packed)
    ang = packed * invf_ref[...]
    cos = jnp.cos(ang)
    sin = sign_ref[...] * jnp.sin(ang)
    for k in range(pack):
        shift = (HEAD_DIM - k * ROPE_DIM) % HEAD_DIM
        cos_k = pltpu.roll(cos, shift, 1) if shift else cos
        sin_k = pltpu.roll(sin, shift, 1) if shift else sin
        cos_ref[pl.ds(k, rows, stride=pack), :] = jnp.where(lane < ROPE_DIM, cos_k, 1.0)
        sin_ref[pl.ds(k, rows, stride=pack), :] = jnp.where(lane < ROPE_DIM, sin_k, 0.0)


def _rope_tables(positions, *, ts=1024):
    b, s = positions.shape
    pack = HEAD_DIM // ROPE_DIM
    inv_freq = ROPE_THETA ** (-2.0 * jnp.arange(ROPE_HALF, dtype=F32) / ROPE_DIM)
    invf = jnp.tile(inv_freq, 2 * pack)[None]
    sign = jnp.tile(jnp.concatenate([-jnp.ones((ROPE_HALF,), F32), jnp.ones((ROPE_HALF,), F32)]), pack)[None]
    out = jax.ShapeDtypeStruct((b, s, HEAD_DIM), F32)
    tab = pl.BlockSpec((None, ts, HEAD_DIM), lambda bi, si: (bi, si, 0))
    return pl.pallas_call(
        _rope_table_kernel,
        grid=(b, s // ts),
        in_specs=[pl.BlockSpec((None, ts // pack, pack), lambda bi, si: (bi, si, 0)),
                  _resident((1, HEAD_DIM)), _resident((1, HEAD_DIM))],
        out_specs=[tab, tab],
        out_shape=[out, out],
        compiler_params=_params(("parallel", "parallel")),
        name="rope_tables",
    )(positions.reshape(b, s // pack, pack), invf, sign)


def _dilation_pitch(dil):
    return dil + 4 if dil % 8 == 0 else dil


def _qkv_kernel(h_ref, cos_ref, sin_ref, gn_ref, wq_ref, wk_ref, wv_ref, q_ref, k_ref, v_ref,
                *regroup_scratch, dil, tm):
    sub = tm // dil
    xn = _rms(h_ref[...], gn_ref[...])
    if dil == 1:
        cos, sin = cos_ref[...], sin_ref[...]
        xn = xn.astype(BF16)
    else:
        slab_ref, xp_ref = regroup_scratch
        pitch = slab_ref.shape[1] // sub
        regroup = lambda ref: jnp.concatenate(
            [ref[pl.ds(r, sub, stride=dil), :] for r in range(dil)], axis=0)
        cos, sin = regroup(cos_ref), regroup(sin_ref)
        for c in range(D_MODEL // HEAD_DIM):
            cols = slice(c * HEAD_DIM, (c + 1) * HEAD_DIM)
            for i in range(sub if pitch != dil else 1):
                rows = dil if pitch != dil else tm
                slab_ref[c, i * pitch:i * pitch + rows, :] = xn[i * dil:i * dil + rows, cols]
            for r in range(dil):
                xp_ref[r * sub:(r + 1) * sub, cols] = (
                    slab_ref[c, pl.ds(r, sub, stride=pitch), :].astype(BF16))
        xn = xp_ref[...]

    def store(out_ref, cols, val):
        for r in range(dil):
            out_ref[r, :, cols] = val[r * sub:(r + 1) * sub, :]

    low_half = lax.broadcasted_iota(jnp.int32, cos.shape, 1) < ROPE_HALF
    for w_ref, out_ref, scale in ((wq_ref, q_ref, SCORE_SCALE), (wk_ref, k_ref, 1.0)):
        z = _dot(xn, w_ref[...])
        cos_s, sin_s = (cos, sin) if scale == 1.0 else (cos * scale, sin * scale)
        for hd in range(N_HEADS):
            cols = slice(hd * HEAD_DIM, (hd + 1) * HEAD_DIM)
            x = z[:, cols]
            partner = jnp.where(low_half, pltpu.roll(x, HEAD_DIM - ROPE_HALF, 1),
                                pltpu.roll(x, ROPE_HALF, 1))
            store(out_ref, cols, (x * cos_s + partner * sin_s).astype(BF16))
    store(v_ref, slice(None), _dot(xn, wv_ref[...]).astype(BF16))


def _qkv(h, cos_t, sin_t, g, layer, w_qkv, j, group, dil, *, tm=1024):
    b, s, d = h.shape
    n_groups = len(DILATED_PATTERNS)
    tok = lambda width: pl.BlockSpec((None, tm, width), lambda bi, ti: (bi, ti, 0))
    w_spec = lambda which: _resident((None, d, d), (j, 0, which * n_groups + group))
    out = jax.ShapeDtypeStruct((b, dil, s // dil, d), BF16)
    out_spec = pl.BlockSpec((None, dil, tm // dil, d), lambda bi, ti: (bi, 0, ti, 0))
    pitch = _dilation_pitch(dil)
    regroup_scratch = [] if dil == 1 else [
        pltpu.VMEM((d // HEAD_DIM, (tm // dil) * pitch, HEAD_DIM), F32), pltpu.VMEM((tm, d), BF16)]
    return pl.pallas_call(
        functools.partial(_qkv_kernel, dil=dil, tm=tm),
        grid=(b, s // tm),
        in_specs=[tok(d), tok(HEAD_DIM), tok(HEAD_DIM), _layer(g, layer), w_spec(0), w_spec(1),
                  w_spec(2)],
        out_specs=[out_spec] * 3,
        out_shape=[out] * 3,
        scratch_shapes=regroup_scratch,
        compiler_params=_params(("parallel", "parallel")),
        name=f"qkv_dil{dil}",
    )(h, cos_t, sin_t, g, w_qkv, w_qkv, w_qkv)


def _attn_kernel(q_ref, k_ref, v_ref, o_ref, lse_ref, kbuf_ref, vbuf_ref, m_ref, l_ref,
                 *, n_seq, n_qblocks, whole_sequence):
    step = pl.program_id(1)
    blk = ATTN_BLOCK
    tq = n_qblocks * blk
    vcols = lambda hd: slice(2 * hd * HEAD_DIM, (2 * hd + 1) * HEAD_DIM)

    @pl.when(_first_grid_step())
    def _():
        kbuf_ref[...] = jnp.zeros(kbuf_ref.shape, BF16)
        vbuf_ref[...] = jnp.ones(vbuf_ref.shape, BF16)

    qi = lax.broadcasted_iota(jnp.int32, (blk, 2 * blk), 0)
    kj = lax.broadcasted_iota(jnp.int32, (blk, 2 * blk), 1)
    rel = blk + qi - kj
    band = (rel >= 0) & (rel <= blk)
    after_start = False if whole_sequence else step > 0
    band_first = band & ((kj >= blk) | after_start)

    for sq in range(n_seq):
        kbuf_ref[sq, blk:blk + tq, :] = k_ref[sq]
        for hd in range(N_HEADS):
            vbuf_ref[sq, blk:blk + tq, vcols(hd)] = v_ref[sq, :, hd * HEAD_DIM:(hd + 1) * HEAD_DIM]
        for qb in range(n_qblocks):
            rows = slice(qb * blk, (qb + 1) * blk)
            keys = slice(qb * blk, (qb + 2) * blk)
            valid = band_first if qb == 0 else band
            for hd in range(N_HEADS):
                cols = slice(hd * HEAD_DIM, (hd + 1) * HEAD_DIM)
                stat = slice(hd * LSE_LANES_PER_HEAD, (hd + 1) * LSE_LANES_PER_HEAD)
                s2 = lax.dot_general(q_ref[sq, rows, cols], kbuf_ref[sq, keys, cols],
                                     (((1,), (1,)), ((), ())), preferred_element_type=F32)
                s2 = jnp.where(valid, s2, MASK_VALUE)
                m2 = jnp.max(s2, axis=-1, keepdims=True)
                p = jnp.exp2(s2 - m2)
                ol = _dot(p.astype(BF16),
                          vbuf_ref[sq, keys, 2 * hd * HEAD_DIM:2 * (hd + 1) * HEAD_DIM])
                l = ol[:, HEAD_DIM:]
                o_ref[sq, rows, cols] = (ol[:, :HEAD_DIM] * (1.0 / l)).astype(BF16)
                m_ref[sq, rows, stat] = jnp.broadcast_to(m2, (blk, LSE_LANES_PER_HEAD))
                l_ref[sq, rows, stat] = l[:, stat]
        if not whole_sequence:
            kbuf_ref[sq, 0:blk, :] = kbuf_ref[sq, tq:tq + blk, :]
            vbuf_ref[sq, 0:blk, :] = vbuf_ref[sq, tq:tq + blk, :]
    lse_ref[...] = (m_ref[...] + jnp.log2(l_ref[...])) * math.log(2.0)


def _band_attention(q, k, v, *, units_per_step=64):
    n, sub, d = q.shape
    blocks_per_step = units_per_step // N_HEADS
    n_qblocks = min(blocks_per_step, sub // ATTN_BLOCK)
    n_seq = blocks_per_step // n_qblocks
    tq = n_qblocks * ATTN_BLOCK
    tok = lambda width: pl.BlockSpec((n_seq, tq, width), lambda ni, ti: (ni, ti, 0))
    return pl.pallas_call(
        functools.partial(_attn_kernel, n_seq=n_seq, n_qblocks=n_qblocks, whole_sequence=(tq == sub)),
        grid=(n // n_seq, sub // tq),
        in_specs=[tok(d)] * 3,
        out_specs=[tok(d), tok(HEAD_DIM)],
        out_shape=[jax.ShapeDtypeStruct((n, sub, d), BF16),
                   jax.ShapeDtypeStruct((n, sub, HEAD_DIM), F32)],
        scratch_shapes=[pltpu.VMEM((n_seq, ATTN_BLOCK + tq, d), BF16),
                        pltpu.VMEM((n_seq, ATTN_BLOCK + tq, 2 * d), BF16),
                        pltpu.VMEM((n_seq, tq, HEAD_DIM), F32), pltpu.VMEM((n_seq, tq, HEAD_DIM), F32)],
        compiler_params=_params(("arbitrary", "arbitrary")),
        name=f"band_attn_len{sub}",
    )(q, k, v)


def _combine_kernel(h_ref, o0_ref, l0_ref, o1_ref, l1_ref, o2_ref, l2_ref, wo_ref, out_ref,
                    lse1_ref, lse2_ref, slab1_ref, slab2_ref, mix_ref, *, tm):
    def natural_lse(l_ref, nat_ref, dil):
        for r in range(dil):
            nat_ref[pl.ds(r, tm // dil, stride=dil), :] = l_ref[r]
        return nat_ref[...]

    def natural_out(o_ref, slab_ref, dil, hd):
        sub = tm // dil
        pitch = slab_ref.shape[1] // sub
        cols = slice(hd * HEAD_DIM, (hd + 1) * HEAD_DIM)
        for r in range(dil):
            slab_ref[hd, pl.ds(r, sub, stride=pitch), :] = o_ref[r, :, cols].astype(F32)
        if pitch == dil:
            return slab_ref[hd]
        return jnp.concatenate([slab_ref[hd, i * pitch:i * pitch + dil, :] for i in range(sub)], axis=0)

    dil1, dil2 = DILATED_PATTERNS[1][1], DILATED_PATTERNS[2][1]
    lses = (l0_ref[0], natural_lse(l1_ref, lse1_ref, dil1), natural_lse(l2_ref, lse2_ref, dil2))
    m = jnp.maximum(jnp.maximum(lses[0], lses[1]), lses[2])
    es = [jnp.exp(l - m) for l in lses]
    inv = 1.0 / (es[0] + es[1] + es[2])
    w1, w2 = es[1] * inv, es[2] * inv
    for hd in range(N_HEADS):
        cols = slice(hd * HEAD_DIM, (hd + 1) * HEAD_DIM)
        lane = slice(hd * LSE_LANES_PER_HEAD, hd * LSE_LANES_PER_HEAD + 1)
        o0 = o0_ref[0, :, cols].astype(F32)
        mixed = o0
        for w, o in ((w1, natural_out(o1_ref, slab1_ref, dil1, hd)),
                     (w2, natural_out(o2_ref, slab2_ref, dil2, hd))):
            mixed = mixed + jnp.broadcast_to(w[:, lane], (tm, HEAD_DIM)) * (o - o0)
        mix_ref[:, cols] = mixed.astype(BF16)
    out_ref[...] = h_ref[...] + _dot(mix_ref[...], wo_ref[...])


def _combine(h, outs, lses, w_o, j, *, tm=1024):
    b, s, d = h.shape
    tok = pl.BlockSpec((None, tm, d), lambda bi, ti: (bi, ti, 0))
    specs, slabs = [], []
    for (_, dil) in DILATED_PATTERNS:
        for width in (d, HEAD_DIM):
            specs.append(pl.BlockSpec((None, dil, tm // dil, width), lambda bi, ti: (bi, 0, ti, 0)))
        if dil > 1:
            slabs.append(pltpu.VMEM((N_HEADS, (tm // dil) * _dilation_pitch(dil), HEAD_DIM), F32))
    args = [a for pair in zip(outs, lses) for a in pair]
    return pl.pallas_call(
        functools.partial(_combine_kernel, tm=tm),
        grid=(b, s // tm),
        in_specs=[tok] + specs + [_layer(w_o, j)],
        out_specs=tok,
        out_shape=jax.ShapeDtypeStruct(h.shape, F32),
        scratch_shapes=[pltpu.VMEM((tm, HEAD_DIM), F32), pltpu.VMEM((tm, HEAD_DIM), F32)] + slabs
        + [pltpu.VMEM((tm, d), BF16)],
        compiler_params=_params(("parallel", "parallel")),
        name="attn_combine",
    )(h, *args, w_o)


def _attention_mixer(h, cos_t, sin_t, g, layer, w_qkv, w_o, j):
    b, s, d = h.shape
    outs, lses = [], []
    for group, (window, dil) in enumerate(DILATED_PATTERNS):
        assert window // dil == ATTN_BLOCK
        q, k, v = _qkv(h, cos_t, sin_t, g, layer, w_qkv, j, group, dil)
        sub = s // dil
        o_g, l_g = _band_attention(q.reshape(b * dil, sub, d), k.reshape(b * dil, sub, d),
                                   v.reshape(b * dil, sub, d))
        outs.append(o_g.reshape(b, dil, sub, d))
        lses.append(l_g.reshape(b, dil, sub, HEAD_DIM))
    return _combine(h, outs, lses, w_o, j)


def kernel(x, p, positions, norm_mix, norm_mlp, norm_ple, norm_final, sc_w_in, sc_w_conv, sc_w_out,
           attn_w_qkv, attn_w_o, lru_w_in, lru_conv_w, lru_conv_b, lru_w_a, lru_b_a, lru_w_x,
           lru_b_x, lru_lambda, lru_w_out, mlp_w_up, mlp_w_down, ple_w_gate, ple_w_proj):
    bf = lambda w: w.astype(BF16)
    norm_mix, norm_mlp, norm_ple = _rows(norm_mix), _rows(norm_mlp), _rows(norm_ple)
    sc_w_in, sc_w_out = bf(sc_w_in), bf(sc_w_out)
    attn_w_qkv, attn_w_o = bf(attn_w_qkv), bf(attn_w_o)
    lru_w_in, lru_w_out = bf(lru_w_in), bf(lru_w_out)
    lru_w_ax = bf(0.5 * jnp.concatenate([lru_w_a, lru_w_x], axis=-1))
    lru_conv_b, lru_b_a, lru_b_x, lru_lambda = (_rows(lru_conv_b), _rows(0.5 * lru_b_a),
                                                _rows(0.5 * lru_b_x), _rows(lru_lambda))
    cos_t, sin_t = _rope_tables(positions)

    h = x
    for i in range(DEPTH):
        kind, j = i % N_MIXERS, i // N_MIXERS
        if kind == 0:
            h = _conv_mixer(h, norm_mix, i, sc_w_in, sc_w_conv, sc_w_out, j)
        elif kind == 1:
            h = _attention_mixer(h, cos_t, sin_t, norm_mix, i, attn_w_qkv, attn_w_o, j)
        else:
            h = _lru_mixer(h, norm_mix, i, lru_w_in, lru_conv_w, lru_conv_b, lru_w_ax, lru_b_a,
                           lru_b_x, lru_lambda, lru_w_out, j)
        h = _post(h, p, norm_mlp, norm_ple, norm_final[None], mlp_w_up, mlp_w_down, ple_w_gate,
                  ple_w_proj, i)
    return h
```

```python
import functools
import math

import jax
import jax.numpy as jnp
from jax import lax
from jax.experimental import pallas as pl
from jax.experimental.pallas import tpu as pltpu

D_MODEL = 1024
DEPTH = 4
N_MIXERS = 3
PLE_DIM = 256
D_FF = 4 * D_MODEL
RMS_EPS = 1e-6
SC_WIDTH = 3
HEAD_DIM = 128
N_HEADS = D_MODEL // HEAD_DIM
DILATED_PATTERNS = ((128, 1), (512, 4), (2048, 16))
ROPE_THETA = 500000.0
ROPE_DIM = HEAD_DIM // 4
ROPE_HALF = ROPE_DIM // 2
D_RNN = 1280
N_LRU_BLOCKS = 10
LRU_BLOCK = D_RNN // N_LRU_BLOCKS
LRU_CONV_WIDTH = 4
LRU_C = 8.0

ATTN_BLOCK = 128
LSE_LANES_PER_HEAD = HEAD_DIM // N_HEADS
SUBLANES = 8
HALO = SUBLANES
MASK_VALUE = -1e30
SCORE_SCALE = HEAD_DIM ** -0.5 * math.log2(math.e)
VMEM_LIMIT_BYTES = 56 * 1024 * 1024
BF16_SUBLANES = 16

F32 = jnp.float32
BF16 = jnp.bfloat16


def _rms(x, g):
    return x * lax.rsqrt(jnp.mean(x * x, axis=-1, keepdims=True) + RMS_EPS) * g


def _dot(a, b):
    return jnp.dot(a, b, preferred_element_type=F32)


def _resident(shape, index=None):
    index = (0,) * len(shape) if index is None else index
    return pl.BlockSpec(shape, lambda *_: index, pipeline_mode=pl.Buffered(1))


def _layer(stacked, layer):
    return _resident((None,) + stacked.shape[1:], (layer, 0, 0))


def _rows(stacked):
    return stacked[:, None, :]


def _params(semantics):
    return pltpu.CompilerParams(dimension_semantics=semantics, vmem_limit_bytes=VMEM_LIMIT_BYTES)


def _call(body, *, name, grid, in_specs, out_specs, out_shape, args, scratch_shapes=(), semantics,
          casts=()):
    n_in, n_out, n_casts = len(in_specs), len(out_specs), len(casts)
    n_steps = math.prod(grid)

    def step_of(*idx):
        step = 0
        for i, n in zip(idx, grid):
            step = step * n + i
        return step

    in_specs, out_specs, out_shape, args = list(in_specs), list(out_specs), list(out_shape), list(args)
    for stack, layer in casts:
        _, rows, cols = stack.shape
        chunk = rows // n_steps
        assert rows % n_steps == 0 and chunk % BF16_SUBLANES == 0
        in_specs.append(pl.BlockSpec((None, chunk, cols), lambda *idx, layer=layer: (layer, step_of(*idx), 0)))
        out_specs.append(pl.BlockSpec((chunk, cols), lambda *idx: (step_of(*idx), 0)))
        out_shape.append(jax.ShapeDtypeStruct((rows, cols), BF16))
        args.append(stack)

    def kernel(*refs):
        main_in, rest = refs[:n_in], refs[n_in:]
        cast_in, rest = rest[:n_casts], rest[n_casts:]
        main_out, rest = rest[:n_out], rest[n_out:]
        cast_out, scratch = rest[:n_casts], rest[n_casts:]
        for src, dst in zip(cast_in, cast_out):
            dst[...] = src[...].astype(BF16)
        body(*main_in, *main_out, *scratch)

    outs = pl.pallas_call(
        kernel, grid=grid, in_specs=in_specs, out_specs=out_specs, out_shape=out_shape,
        scratch_shapes=list(scratch_shapes), compiler_params=_params(semantics), name=name)(*args)
    return outs[:n_out], outs[n_out:]


def _first_grid_step():
    return (pl.program_id(0) == 0) & (pl.program_id(1) == 0)


def _post_kernel(h_ref, p_ref, gm_ref, gp_ref, gf_ref, wup_ref, wdn_ref, wg_ref, wp_ref, o_ref,
                 *, final, tf):
    h = h_ref[...]
    xn = _rms(h, gm_ref[...]).astype(BF16)
    acc = h
    for f in range(D_FF // tf):
        u = _dot(xn, wup_ref[:, f * tf:(f + 1) * tf])
        a = jnp.square(jnp.maximum(u, 0.0)).astype(BF16)
        acc = acc + _dot(a, wdn_ref[f * tf:(f + 1) * tf, :])
    xg = _rms(acc, gp_ref[...]).astype(BF16)
    gate = jax.nn.sigmoid(_dot(xg, wg_ref[...]))
    proj = _dot(p_ref[...].astype(BF16), wp_ref[...])
    out = acc + gate * proj
    if final:
        out = _rms(out, gf_ref[...])
    o_ref[...] = out


def _post(h, p, g_mlp, g_ple, g_final, w_up, w_down, w_gate, w_proj_all, layer, casts, *, tm=1024, tf=512):
    b, s, d = h.shape
    tok = pl.BlockSpec((None, tm, d), lambda bi, si: (bi, si, 0))
    p_spec = pl.BlockSpec((None, None, tm, PLE_DIM), lambda bi, si: (layer, bi, si, 0))
    (out,), cast = _call(
        functools.partial(_post_kernel, final=(layer == DEPTH - 1), tf=tf),
        name="mlp_ple", grid=(b, s // tm),
        in_specs=[tok, p_spec, _layer(g_mlp, layer), _layer(g_ple, layer), _resident((1, d)),
                  _resident(w_up.shape), _resident(w_down.shape), _resident(w_gate.shape),
                  _resident((PLE_DIM, d), (layer, 0))],
        out_specs=[tok], out_shape=[jax.ShapeDtypeStruct(h.shape, F32)],
        args=(h, p, g_mlp, g_ple, g_final, w_up, w_down, w_gate, w_proj_all),
        semantics=("parallel", "parallel"), casts=casts)
    return out, cast


def _causal_taps(buf_ref, x, w, tm):
    k_width = w.shape[0]
    buf_ref[HALO:HALO + tm, :] = x
    out = w[k_width - 1:k_width] * x
    for back in range(1, k_width):
        out = out + w[k_width - 1 - back:k_width - back] * buf_ref[HALO - back:HALO - back + tm, :]
    buf_ref[0:HALO, :] = buf_ref[tm:tm + HALO, :]
    return out


def _conv_mixer_kernel(h_ref, gn_ref, win_ref, wconv_ref, wout_ref, o_ref, buf_ref, *, tm):
    @pl.when(pl.program_id(1) == 0)
    def _():
        buf_ref[0:HALO, :] = jnp.zeros((HALO, D_MODEL), F32)

    h = h_ref[...]
    xn = _rms(h, gn_ref[...]).astype(BF16)
    gate_b = _dot(xn, win_ref[:, 0:D_MODEL])
    gate_c = _dot(xn, win_ref[:, D_MODEL:2 * D_MODEL])
    xin = _dot(xn, win_ref[:, 2 * D_MODEL:3 * D_MODEL])
    conv = _causal_taps(buf_ref, gate_c * xin, wconv_ref[...], tm)
    y = (gate_b * conv).astype(BF16)
    o_ref[...] = h + _dot(y, wout_ref[...])


def _conv_mixer(h, g, layer, w_in, w_out, w_conv, j, casts, *, tm=1024):
    b, s, d = h.shape
    tok = pl.BlockSpec((None, tm, d), lambda bi, si: (bi, si, 0))
    (out,), cast = _call(
        functools.partial(_conv_mixer_kernel, tm=tm),
        name="conv_mixer", grid=(b, s // tm),
        in_specs=[tok, _layer(g, layer), _resident(w_in.shape), _layer(w_conv, j), _resident(w_out.shape)],
        out_specs=[tok], out_shape=[jax.ShapeDtypeStruct(h.shape, F32)],
        args=(h, g, w_in, w_conv, w_out),
        scratch_shapes=[pltpu.VMEM((HALO + tm, d), F32)],
        semantics=("parallel", "arbitrary"), casts=casts)
    return out, cast


def _sublane_scan(a, u, row):
    for k in (1, 2, 4):
        keep = row >= k
        a_back = jnp.where(keep, pltpu.roll(a, k, 0), 1.0)
        u_back = jnp.where(keep, pltpu.roll(u, k, 0), 0.0)
        u = a * u_back + u
        a = a * a_back
    return a, u


def _lru_kernel(h_ref, gn_ref, win_ref, cw_ref, cb_ref, wax_ref, ba_ref, bx_ref, lam_ref, wout_ref,
                o_ref, slab_ref, xs_ref, tail_ref, carry_ref, y_ref, *, tm):
    chunk = tm // SUBLANES
    pitch = chunk + SUBLANES
    n_back = LRU_CONV_WIDTH - 1
    lead = n_back * SUBLANES
    vreg = lambda i: slice(i * SUBLANES, (i + 1) * SUBLANES)

    @pl.when(pl.program_id(1) == 0)
    def _():
        tail_ref[...] = jnp.zeros(tail_ref.shape, F32)
        carry_ref[...] = jnp.zeros(carry_ref.shape, F32)

    h = h_ref[...]
    xn = _rms(h, gn_ref[...]).astype(BF16)
    taps = cw_ref[...]
    bias = cb_ref[...]
    half_log_decay = -0.5 * LRU_C * jax.nn.softplus(-lam_ref[...])
    h_prev = carry_ref[...]
    row = lax.broadcasted_iota(jnp.int32, (SUBLANES, LRU_BLOCK), 0)
    block_cols = lambda n: slice(n * LRU_BLOCK, (n + 1) * LRU_BLOCK)

    def conv_and_gate_matmul(n, x_nat):
        cols = block_cols(n)
        for s in range(SUBLANES):
            slab_ref[n, s * pitch:s * pitch + chunk, :] = x_nat[s * chunk:(s + 1) * chunk, :]
        for i in range(chunk):
            xs_ref[n, lead + i * SUBLANES:lead + (i + 1) * SUBLANES, :] = (
                slab_ref[n, pl.ds(i, SUBLANES, stride=pitch), :])
        for k in range(1, n_back + 1):
            cur = xs_ref[n, lead + (chunk - k) * SUBLANES:lead + (chunk - k + 1) * SUBLANES, :]
            prev = tail_ref[n, vreg(n_back - k), :]
            xs_ref[n, lead - k * SUBLANES:lead - (k - 1) * SUBLANES, :] = pltpu.roll(
                jnp.where(row == SUBLANES - 1, prev, cur), 1, 0)
        tail_ref[n] = xs_ref[n, lead + (chunk - n_back) * SUBLANES:lead + chunk * SUBLANES, :]
        xr = bias[:, cols]
        for k in range(LRU_CONV_WIDTH):
            xr = xr + taps[n_back - k:n_back - k + 1, cols] * xs_ref[
                n, lead - k * SUBLANES:lead - k * SUBLANES + tm, :]
        return xr, _dot(xr.astype(BF16), wax_ref[n])

    def recurrence(n, xr, ri, gate):
        cols = block_cols(n)
        t_r = jnp.tanh(ri[:, 0:LRU_BLOCK] + ba_ref[:, cols])
        t_i = jnp.tanh(ri[:, LRU_BLOCK:2 * LRU_BLOCK] + bx_ref[:, cols])
        log_a = (t_r + 1.0) * half_log_decay[:, cols]
        a = jnp.exp(log_a)
        z = -jnp.tanh(log_a) * (a * a + 1.0)
        u = (0.5 * jnp.exp2(0.5 * jnp.log2(z))) * ((t_i + 1.0) * xr)
        h_loc = jnp.zeros((SUBLANES, LRU_BLOCK), F32)
        a_tot = jnp.ones((SUBLANES, LRU_BLOCK), F32)
        for i in range(chunk):
            h_loc = a[vreg(i)] * h_loc + u[vreg(i)]
            a_tot = a_tot * a[vreg(i)]
        a_cum, h_cum = _sublane_scan(a_tot, h_loc, row)
        ends = a_cum * h_prev[:, cols] + h_cum
        h_cur = jnp.where(row == 0, h_prev[:, cols], pltpu.roll(ends, 1, 0))
        for i in range(chunk):
            h_cur = a[vreg(i)] * h_cur + u[vreg(i)]
            slab_ref[n, pl.ds(i, SUBLANES, stride=pitch), :] = h_cur
        for s in range(SUBLANES):
            rows = slice(s * chunk, (s + 1) * chunk)
            hs = slab_ref[n, s * pitch:s * pitch + chunk, :]
            y_ref[rows, cols] = (hs * jax.nn.gelu(gate[rows, :])).astype(BF16)
        return ends[SUBLANES - 1:SUBLANES, :]

    pair_width = 2 * LRU_BLOCK
    n_pairs = N_LRU_BLOCKS // 2
    in_proj = lambda lo: _dot(xn, win_ref[:, lo:lo + pair_width])
    out_proj = lambda p: _dot(y_ref[:, p * pair_width:(p + 1) * pair_width],
                              wout_ref[p * pair_width:(p + 1) * pair_width, :])
    x_pairs = [in_proj(D_RNN), in_proj(D_RNN + pair_width)]
    out = h
    carries = []
    for pair in range(n_pairs):
        blocks = (2 * pair, 2 * pair + 1)
        halves = (slice(0, LRU_BLOCK), slice(LRU_BLOCK, pair_width))
        fronts = [conv_and_gate_matmul(n, x_pairs[pair][:, half]) for n, half in zip(blocks, halves)]
        if pair > 0:
            out = out + out_proj(pair - 1)
        if pair + 2 < n_pairs:
            x_pairs.append(in_proj(D_RNN + (pair + 2) * pair_width))
        gate = in_proj(pair * pair_width)
        for n, half, (xr, ri) in zip(blocks, halves, fronts):
            carries.append(recurrence(n, xr, ri, gate[:, half]))
    out = out + out_proj(n_pairs - 1)

    carry_ref[...] = jnp.concatenate(carries, axis=1)
    o_ref[...] = out


def _lru_mixer(h, g, layer, w_in, w_out, conv_w, conv_b, w_ax, b_a, b_x, lam, j, casts, *, tm=512):
    b, s, d = h.shape
    tok = pl.BlockSpec((None, tm, d), lambda bi, si: (bi, si, 0))
    gates = _resident((None, N_LRU_BLOCKS, LRU_BLOCK, 2 * LRU_BLOCK), (j, 0, 0, 0))
    lead = (LRU_CONV_WIDTH - 1) * SUBLANES
    slab_rows = SUBLANES * (tm // SUBLANES + SUBLANES)
    (out,), cast = _call(
        functools.partial(_lru_kernel, tm=tm),
        name="rglru_mixer", grid=(b, s // tm),
        in_specs=[tok, _layer(g, layer), _resident(w_in.shape), _layer(conv_w, j), _layer(conv_b, j),
                  gates, _layer(b_a, j), _layer(b_x, j), _layer(lam, j), _resident(w_out.shape)],
        out_specs=[tok], out_shape=[jax.ShapeDtypeStruct(h.shape, F32)],
        args=(h, g, w_in, conv_w, conv_b, w_ax, b_a, b_x, lam, w_out),
        scratch_shapes=[pltpu.VMEM((N_LRU_BLOCKS, slab_rows, LRU_BLOCK), F32),
                        pltpu.VMEM((N_LRU_BLOCKS, lead + tm, LRU_BLOCK), F32),
                        pltpu.VMEM((N_LRU_BLOCKS, lead, LRU_BLOCK), F32),
                        pltpu.VMEM((1, D_RNN), F32), pltpu.VMEM((tm, D_RNN), BF16)],
        semantics=("parallel", "arbitrary"), casts=casts)
    return out, cast


def _rope_table_kernel(pos_ref, invf_ref, sign_ref, cos_ref, sin_ref):
    pack = HEAD_DIM // ROPE_DIM
    rows = pos_ref.shape[0]
    pos = pos_ref[...].astype(F32)
    lane = lax.broadcasted_iota(jnp.int32, (rows, HEAD_DIM), 1)
    packed = pos[:, pack - 1:pack]
    for k in range(pack - 2, -1, -1):
        packed = jnp.where(lane < (k + 1) * ROPE_DIM, pos[:, k:k + 1], packed)
    ang = packed * invf_ref[...]
    cos = jnp.cos(ang)
    sin = sign_ref[...] * jnp.sin(ang)
    for k in range(pack):
        shift = (HEAD_DIM - k * ROPE_DIM) % HEAD_DIM
        cos_k = pltpu.roll(cos, shift, 1) if shift else cos
        sin_k = pltpu.roll(sin, shift, 1) if shift else sin
        cos_ref[pl.ds(k, rows, stride=pack), :] = jnp.where(lane < ROPE_DIM, cos_k, 1.0)
        sin_ref[pl.ds(k, rows, stride=pack), :] = jnp.where(lane < ROPE_DIM, sin_k, 0.0)


def _rope_tables(positions, casts, *, ts=1024):
    b, s = positions.shape
    pack = HEAD_DIM // ROPE_DIM
    inv_freq = ROPE_THETA ** (-2.0 * jnp.arange(ROPE_HALF, dtype=F32) / ROPE_DIM)
    invf = jnp.tile(inv_freq, 2 * pack)[None]
    sign = jnp.tile(jnp.concatenate([-jnp.ones((ROPE_HALF,), F32), jnp.ones((ROPE_HALF,), F32)]), pack)[None]
    out = jax.ShapeDtypeStruct((b, s, HEAD_DIM), F32)
    tab = pl.BlockSpec((None, ts, HEAD_DIM), lambda bi, si: (bi, si, 0))
    return _call(
        _rope_table_kernel, name="rope_tables", grid=(b, s // ts),
        in_specs=[pl.BlockSpec((None, ts // pack, pack), lambda bi, si: (bi, si, 0)),
                  _resident((1, HEAD_DIM)), _resident((1, HEAD_DIM))],
        out_specs=[tab, tab], out_shape=[out, out],
        args=(positions.reshape(b, s // pack, pack), invf, sign),
        semantics=("parallel", "parallel"), casts=casts)


def _dilation_pitch(dil):
    return dil + 4 if dil % 8 == 0 else dil


def _qkv_kernel(h_ref, cos_ref, sin_ref, gn_ref, wq_ref, wk_ref, wv_ref, q_ref, k_ref, v_ref,
                *regroup_scratch, dil, tm):
    sub = tm // dil
    xn = _rms(h_ref[...], gn_ref[...])
    if dil == 1:
        cos, sin = cos_ref[...], sin_ref[...]
        xn = xn.astype(BF16)
    else:
        slab_ref, xp_ref = regroup_scratch
        pitch = slab_ref.shape[1] // sub
        regroup = lambda ref: jnp.concatenate(
            [ref[pl.ds(r, sub, stride=dil), :] for r in range(dil)], axis=0)
        cos, sin = regroup(cos_ref), regroup(sin_ref)
        for c in range(D_MODEL // HEAD_DIM):
            cols = slice(c * HEAD_DIM, (c + 1) * HEAD_DIM)
            for i in range(sub if pitch != dil else 1):
                rows = dil if pitch != dil else tm
                slab_ref[c, i * pitch:i * pitch + rows, :] = xn[i * dil:i * dil + rows, cols]
            for r in range(dil):
                xp_ref[r * sub:(r + 1) * sub, cols] = (
                    slab_ref[c, pl.ds(r, sub, stride=pitch), :].astype(BF16))
        xn = xp_ref[...]

    def store(out_ref, cols, val):
        for r in range(dil):
            out_ref[r, :, cols] = val[r * sub:(r + 1) * sub, :]

    low_half = lax.broadcasted_iota(jnp.int32, cos.shape, 1) < ROPE_HALF
    for w_ref, out_ref, scale in ((wq_ref, q_ref, SCORE_SCALE), (wk_ref, k_ref, 1.0)):
        z = _dot(xn, w_ref[...])
        cos_s, sin_s = (cos, sin) if scale == 1.0 else (cos * scale, sin * scale)
        for hd in range(N_HEADS):
            cols = slice(hd * HEAD_DIM, (hd + 1) * HEAD_DIM)
            x = z[:, cols]
            partner = jnp.where(low_half, pltpu.roll(x, HEAD_DIM - ROPE_HALF, 1),
                                pltpu.roll(x, ROPE_HALF, 1))
            store(out_ref, cols, (x * cos_s + partner * sin_s).astype(BF16))
    store(v_ref, slice(None), _dot(xn, wv_ref[...]).astype(BF16))


def _qkv(h, cos_t, sin_t, g, layer, w_qkv, group, dil, casts, *, tm=1024):
    b, s, d = h.shape
    n_groups = len(DILATED_PATTERNS)
    tok = lambda width: pl.BlockSpec((None, tm, width), lambda bi, ti: (bi, ti, 0))
    w_spec = lambda which: _resident((d, d), (0, which * n_groups + group))
    out = jax.ShapeDtypeStruct((b, dil, s // dil, d), BF16)
    out_spec = pl.BlockSpec((None, dil, tm // dil, d), lambda bi, ti: (bi, 0, ti, 0))
    pitch = _dilation_pitch(dil)
    regroup_scratch = [] if dil == 1 else [
        pltpu.VMEM((d // HEAD_DIM, (tm // dil) * pitch, HEAD_DIM), F32), pltpu.VMEM((tm, d), BF16)]
    return _call(
        functools.partial(_qkv_kernel, dil=dil, tm=tm),
        name=f"qkv_dil{dil}", grid=(b, s // tm),
        in_specs=[tok(d), tok(HEAD_DIM), tok(HEAD_DIM), _layer(g, layer), w_spec(0), w_spec(1),
                  w_spec(2)],
        out_specs=[out_spec] * 3, out_shape=[out] * 3,
        args=(h, cos_t, sin_t, g, w_qkv, w_qkv, w_qkv),
        scratch_shapes=regroup_scratch, semantics=("parallel", "parallel"), casts=casts)


def _attn_kernel(q_ref, k_ref, v_ref, o_ref, lse_ref, kbuf_ref, vbuf_ref, m_ref, l_ref,
                 *, n_seq, n_qblocks, whole_sequence):
    step = pl.program_id(1)
    blk = ATTN_BLOCK
    tq = n_qblocks * blk
    vcols = lambda hd: slice(2 * hd * HEAD_DIM, (2 * hd + 1) * HEAD_DIM)

    @pl.when(_first_grid_step())
    def _():
        kbuf_ref[...] = jnp.zeros(kbuf_ref.shape, BF16)
        vbuf_ref[...] = jnp.ones(vbuf_ref.shape, BF16)

    qi = lax.broadcasted_iota(jnp.int32, (blk, 2 * blk), 0)
    kj = lax.broadcasted_iota(jnp.int32, (blk, 2 * blk), 1)
    rel = blk + qi - kj
    band = (rel >= 0) & (rel <= blk)
    after_start = False if whole_sequence else step > 0
    band_first = band & ((kj >= blk) | after_start)

    for sq in range(n_seq):
        kbuf_ref[sq, blk:blk + tq, :] = k_ref[sq]
        for hd in range(N_HEADS):
            vbuf_ref[sq, blk:blk + tq, vcols(hd)] = v_ref[sq, :, hd * HEAD_DIM:(hd + 1) * HEAD_DIM]
        for qb in range(n_qblocks):
            rows = slice(qb * blk, (qb + 1) * blk)
            keys = slice(qb * blk, (qb + 2) * blk)
            valid = band_first if qb == 0 else band
            for hd in range(N_HEADS):
                cols = slice(hd * HEAD_DIM, (hd + 1) * HEAD_DIM)
                stat = slice(hd * LSE_LANES_PER_HEAD, (hd + 1) * LSE_LANES_PER_HEAD)
                s2 = lax.dot_general(q_ref[sq, rows, cols], kbuf_ref[sq, keys, cols],
                                     (((1,), (1,)), ((), ())), preferred_element_type=F32)
                s2 = jnp.where(valid, s2, MASK_VALUE)
                m2 = jnp.max(s2, axis=-1, keepdims=True)
                p = jnp.exp2(s2 - m2)
                ol = _dot(p.astype(BF16),
                          vbuf_ref[sq, keys, 2 * hd * HEAD_DIM:2 * (hd + 1) * HEAD_DIM])
                l = ol[:, HEAD_DIM:]
                o_ref[sq, rows, cols] = (ol[:, :HEAD_DIM] * (1.0 / l)).astype(BF16)
                m_ref[sq, rows, stat] = jnp.broadcast_to(m2, (blk, LSE_LANES_PER_HEAD))
                l_ref[sq, rows, stat] = l[:, stat]
        if not whole_sequence:
            kbuf_ref[sq, 0:blk, :] = kbuf_ref[sq, tq:tq + blk, :]
            vbuf_ref[sq, 0:blk, :] = vbuf_ref[sq, tq:tq + blk, :]
    lse_ref[...] = (m_ref[...] + jnp.log2(l_ref[...])) * math.log(2.0)


def _band_attention(q, k, v, *, units_per_step=64):
    n, sub, d = q.shape
    blocks_per_step = units_per_step // N_HEADS
    n_qblocks = min(blocks_per_step, sub // ATTN_BLOCK)
    n_seq = blocks_per_step // n_qblocks
    tq = n_qblocks * ATTN_BLOCK
    tok = lambda width: pl.BlockSpec((n_seq, tq, width), lambda ni, ti: (ni, ti, 0))
    return pl.pallas_call(
        functools.partial(_attn_kernel, n_seq=n_seq, n_qblocks=n_qblocks, whole_sequence=(tq == sub)),
        grid=(n // n_seq, sub // tq),
        in_specs=[tok(d)] * 3,
        out_specs=[tok(d), tok(HEAD_DIM)],
        out_shape=[jax.ShapeDtypeStruct((n, sub, d), BF16),
                   jax.ShapeDtypeStruct((n, sub, HEAD_DIM), F32)],
        scratch_shapes=[pltpu.VMEM((n_seq, ATTN_BLOCK + tq, d), BF16),
                        pltpu.VMEM((n_seq, ATTN_BLOCK + tq, 2 * d), BF16),
                        pltpu.VMEM((n_seq, tq, HEAD_DIM), F32), pltpu.VMEM((n_seq, tq, HEAD_DIM), F32)],
        compiler_params=_params(("arbitrary", "arbitrary")),
        name=f"band_attn_len{sub}",
    )(q, k, v)


def _combine_kernel(h_ref, o0_ref, l0_ref, o1_ref, l1_ref, o2_ref, l2_ref, wo_ref, out_ref,
                    lse1_ref, lse2_ref, slab1_ref, slab2_ref, mix_ref, *, tm):
    def natural_lse(l_ref, nat_ref, dil):
        for r in range(dil):
            nat_ref[pl.ds(r, tm // dil, stride=dil), :] = l_ref[r]
        return nat_ref[...]

    def natural_out(o_ref, slab_ref, dil, hd):
        sub = tm // dil
        pitch = slab_ref.shape[1] // sub
        cols = slice(hd * HEAD_DIM, (hd + 1) * HEAD_DIM)
        for r in range(dil):
            slab_ref[hd, pl.ds(r, sub, stride=pitch), :] = o_ref[r, :, cols].astype(F32)
        if pitch == dil:
            return slab_ref[hd]
        return jnp.concatenate([slab_ref[hd, i * pitch:i * pitch + dil, :] for i in range(sub)], axis=0)

    dil1, dil2 = DILATED_PATTERNS[1][1], DILATED_PATTERNS[2][1]
    lses = (l0_ref[0], natural_lse(l1_ref, lse1_ref, dil1), natural_lse(l2_ref, lse2_ref, dil2))
    m = jnp.maximum(jnp.maximum(lses[0], lses[1]), lses[2])
    es = [jnp.exp(l - m) for l in lses]
    inv = 1.0 / (es[0] + es[1] + es[2])
    w1, w2 = es[1] * inv, es[2] * inv
    for hd in range(N_HEADS):
        cols = slice(hd * HEAD_DIM, (hd + 1) * HEAD_DIM)
        lane = slice(hd * LSE_LANES_PER_HEAD, hd * LSE_LANES_PER_HEAD + 1)
        o0 = o0_ref[0, :, cols].astype(F32)
        mixed = o0
        for w, o in ((w1, natural_out(o1_ref, slab1_ref, dil1, hd)),
                     (w2, natural_out(o2_ref, slab2_ref, dil2, hd))):
            mixed = mixed + jnp.broadcast_to(w[:, lane], (tm, HEAD_DIM)) * (o - o0)
        mix_ref[:, cols] = mixed.astype(BF16)
    out_ref[...] = h_ref[...] + _dot(mix_ref[...], wo_ref[...])


def _combine(h, outs, lses, w_o, *, tm=1024):
    b, s, d = h.shape
    tok = pl.BlockSpec((None, tm, d), lambda bi, ti: (bi, ti, 0))
    specs, slabs = [], []
    for (_, dil) in DILATED_PATTERNS:
        for width in (d, HEAD_DIM):
            specs.append(pl.BlockSpec((None, dil, tm // dil, width), lambda bi, ti: (bi, 0, ti, 0)))
        if dil > 1:
            slabs.append(pltpu.VMEM((N_HEADS, (tm // dil) * _dilation_pitch(dil), HEAD_DIM), F32))
    args = [a for pair in zip(outs, lses) for a in pair]
    return pl.pallas_call(
        functools.partial(_combine_kernel, tm=tm),
        grid=(b, s // tm),
        in_specs=[tok] + specs + [_resident(w_o.shape)],
        out_specs=tok,
        out_shape=jax.ShapeDtypeStruct(h.shape, F32),
        scratch_shapes=[pltpu.VMEM((tm, HEAD_DIM), F32), pltpu.VMEM((tm, HEAD_DIM), F32)] + slabs
        + [pltpu.VMEM((tm, d), BF16)],
        compiler_params=_params(("parallel", "parallel")),
        name="attn_combine",
    )(h, *args, w_o)


def _attention_mixer(h, cos_t, sin_t, g, layer, w_qkv, w_o, casts):
    b, s, d = h.shape
    outs, lses, cast = [], [], ()
    for group, (window, dil) in enumerate(DILATED_PATTERNS):
        assert window // dil == ATTN_BLOCK
        (q, k, v), group_cast = _qkv(h, cos_t, sin_t, g, layer, w_qkv, group, dil,
                                     casts if group == 0 else ())
        cast = cast or group_cast
        sub = s // dil
        o_g, l_g = _band_attention(q.reshape(b * dil, sub, d), k.reshape(b * dil, sub, d),
                                   v.reshape(b * dil, sub, d))
        outs.append(o_g.reshape(b, dil, sub, d))
        lses.append(l_g.reshape(b, dil, sub, HEAD_DIM))
    return _combine(h, outs, lses, w_o), cast


def kernel(x, p, positions, norm_mix, norm_mlp, norm_ple, norm_final, sc_w_in, sc_w_conv, sc_w_out,
           attn_w_qkv, attn_w_o, lru_w_in, lru_conv_w, lru_conv_b, lru_w_a, lru_b_a, lru_w_x,
           lru_b_x, lru_lambda, lru_w_out, mlp_w_up, mlp_w_down, ple_w_gate, ple_w_proj):
    norm_mix, norm_mlp, norm_ple = _rows(norm_mix), _rows(norm_mlp), _rows(norm_ple)
    lru_w_ax = (0.5 * jnp.concatenate([lru_w_a, lru_w_x], axis=-1)).astype(BF16)
    lru_conv_b, lru_b_a, lru_b_x, lru_lambda = (_rows(lru_conv_b), _rows(0.5 * lru_b_a),
                                                _rows(0.5 * lru_b_x), _rows(lru_lambda))

    mixer_weights = ((sc_w_in, sc_w_out), (attn_w_qkv, attn_w_o), (lru_w_in, lru_w_out))
    plan = []
    for i in range(DEPTH):
        kind, j = i % N_MIXERS, i // N_MIXERS
        plan.append([(w, j) for w in mixer_weights[kind]])
        plan.append([(mlp_w_up, i), (mlp_w_down, i), (ple_w_gate, i)])
    plan.append([])
    proj_stack = ple_w_proj.reshape(1, DEPTH * PLE_DIM, D_MODEL)
    (cos_t, sin_t), ready = _rope_tables(positions, plan[0] + [(proj_stack, 0)])
    *ready, w_proj_all = ready

    h = x
    for i in range(DEPTH):
        kind, j = i % N_MIXERS, i // N_MIXERS
        if kind == 0:
            h, ready = _conv_mixer(h, norm_mix, i, *ready, sc_w_conv, j, plan[2 * i + 1])
        elif kind == 1:
            h, ready = _attention_mixer(h, cos_t, sin_t, norm_mix, i, *ready, plan[2 * i + 1])
        else:
            h, ready = _lru_mixer(h, norm_mix, i, *ready, lru_conv_w, lru_conv_b, lru_w_ax, lru_b_a,
                                  lru_b_x, lru_lambda, j, plan[2 * i + 1])
        h, ready = _post(h, p, norm_mlp, norm_ple, norm_final[None], *ready, w_proj_all, i,
                         plan[2 * i + 2])
    return h
```

```python
import functools
import math

import jax
import jax.numpy as jnp
from jax import lax
from jax.experimental import pallas as pl
from jax.experimental.pallas import tpu as pltpu

D_MODEL = 1024
DEPTH = 4
N_MIXERS = 3
PLE_DIM = 256
D_FF = 4 * D_MODEL
RMS_EPS = 1e-6
SC_WIDTH = 3
HEAD_DIM = 128
N_HEADS = D_MODEL // HEAD_DIM
DILATED_PATTERNS = ((128, 1), (512, 4), (2048, 16))
ROPE_THETA = 500000.0
ROPE_DIM = HEAD_DIM // 4
ROPE_HALF = ROPE_DIM // 2
D_RNN = 1280
N_LRU_BLOCKS = 10
LRU_BLOCK = D_RNN // N_LRU_BLOCKS
LRU_CONV_WIDTH = 4
LRU_C = 8.0

ATTN_BLOCK = 128
LSE_LANES_PER_HEAD = HEAD_DIM // N_HEADS
SUBLANES = 8
HALO = SUBLANES
MASK_VALUE = -1e30
SCORE_SCALE = HEAD_DIM ** -0.5 * math.log2(math.e)
VMEM_LIMIT_BYTES = 56 * 1024 * 1024
BF16_SUBLANES = 16

F32 = jnp.float32
BF16 = jnp.bfloat16


def _rms(x, g):
    return x * lax.rsqrt(jnp.mean(x * x, axis=-1, keepdims=True) + RMS_EPS) * g


def _dot(a, b):
    return jnp.dot(a, b, preferred_element_type=F32)


def _resident(shape, index=None):
    index = (0,) * len(shape) if index is None else index
    return pl.BlockSpec(shape, lambda *_: index, pipeline_mode=pl.Buffered(1))


def _layer(stacked, layer):
    return _resident((None,) + stacked.shape[1:], (layer, 0, 0))


def _rows(stacked):
    return stacked[:, None, :]


def _params(semantics):
    return pltpu.CompilerParams(dimension_semantics=semantics, vmem_limit_bytes=VMEM_LIMIT_BYTES)


def _call(body, *, name, grid, in_specs, out_specs, out_shape, args, scratch_shapes=(), semantics,
          casts=()):
    n_in, n_out, n_casts = len(in_specs), len(out_specs), len(casts)
    n_steps = math.prod(grid)

    def step_of(*idx):
        step = 0
        for i, n in zip(idx, grid):
            step = step * n + i
        return step

    in_specs, out_specs, out_shape, args = list(in_specs), list(out_specs), list(out_shape), list(args)
    for stack, layer in casts:
        _, rows, cols = stack.shape
        chunk = rows // n_steps
        assert rows % n_steps == 0 and chunk % BF16_SUBLANES == 0
        in_specs.append(pl.BlockSpec((None, chunk, cols), lambda *idx, layer=layer: (layer, step_of(*idx), 0)))
        out_specs.append(pl.BlockSpec((chunk, cols), lambda *idx: (step_of(*idx), 0)))
        out_shape.append(jax.ShapeDtypeStruct((rows, cols), BF16))
        args.append(stack)

    def kernel(*refs):
        main_in, rest = refs[:n_in], refs[n_in:]
        cast_in, rest = rest[:n_casts], rest[n_casts:]
        main_out, rest = rest[:n_out], rest[n_out:]
        cast_out, scratch = rest[:n_casts], rest[n_casts:]
        for src, dst in zip(cast_in, cast_out):
            dst[...] = src[...].astype(BF16)
        body(*main_in, *main_out, *scratch)

    outs = pl.pallas_call(
        kernel, grid=grid, in_specs=in_specs, out_specs=out_specs, out_shape=out_shape,
        scratch_shapes=list(scratch_shapes), compiler_params=_params(semantics), name=name)(*args)
    return outs[:n_out], outs[n_out:]


def _first_grid_step():
    return (pl.program_id(0) == 0) & (pl.program_id(1) == 0)


def _post_kernel(h_ref, p_ref, gm_ref, gp_ref, gf_ref, wup_ref, wdn_ref, wg_ref, wp_ref, o_ref,
                 *, final, tf):
    h = h_ref[...]
    xn = _rms(h, gm_ref[...]).astype(BF16)
    acc = h
    for f in range(D_FF // tf):
        u = _dot(xn, wup_ref[:, f * tf:(f + 1) * tf])
        a = jnp.square(jnp.maximum(u, 0.0)).astype(BF16)
        acc = acc + _dot(a, wdn_ref[f * tf:(f + 1) * tf, :])
    xg = _rms(acc, gp_ref[...]).astype(BF16)
    gate = jax.nn.sigmoid(_dot(xg, wg_ref[...]))
    proj = _dot(p_ref[...].astype(BF16), wp_ref[...])
    out = acc + gate * proj
    if final:
        out = _rms(out, gf_ref[...])
    o_ref[...] = out


def _post(h, p, g_mlp, g_ple, g_final, w_up, w_down, w_gate, w_proj, layer, casts, *, tm=1024, tf=512):
    b, s, d = h.shape
    tok = pl.BlockSpec((None, tm, d), lambda bi, si: (bi, si, 0))
    p_spec = pl.BlockSpec((None, None, tm, PLE_DIM), lambda bi, si: (layer, bi, si, 0))
    (out,), cast = _call(
        functools.partial(_post_kernel, final=(layer == DEPTH - 1), tf=tf),
        name="mlp_ple", grid=(b, s // tm),
        in_specs=[tok, p_spec, _layer(g_mlp, layer), _layer(g_ple, layer), _resident((1, d)),
                  _resident(w_up.shape), _resident(w_down.shape), _resident(w_gate.shape),
                  _resident(w_proj.shape)],
        out_specs=[tok], out_shape=[jax.ShapeDtypeStruct(h.shape, F32)],
        args=(h, p, g_mlp, g_ple, g_final, w_up, w_down, w_gate, w_proj),
        semantics=("parallel", "parallel"), casts=casts)
    return out, cast


def _causal_taps(buf_ref, x, w, tm):
    k_width = w.shape[0]
    buf_ref[HALO:HALO + tm, :] = x
    out = w[k_width - 1:k_width] * x
    for back in range(1, k_width):
        out = out + w[k_width - 1 - back:k_width - back] * buf_ref[HALO - back:HALO - back + tm, :]
    buf_ref[0:HALO, :] = buf_ref[tm:tm + HALO, :]
    return out


def _conv_mixer_kernel(h_ref, gn_ref, win_ref, wconv_ref, wout_ref, o_ref, buf_ref, *, tm):
    @pl.when(pl.program_id(1) == 0)
    def _():
        buf_ref[0:HALO, :] = jnp.zeros((HALO, D_MODEL), F32)

    h = h_ref[...]
    xn = _rms(h, gn_ref[...]).astype(BF16)
    gate_b = _dot(xn, win_ref[:, 0:D_MODEL])
    gate_c = _dot(xn, win_ref[:, D_MODEL:2 * D_MODEL])
    xin = _dot(xn, win_ref[:, 2 * D_MODEL:3 * D_MODEL])
    conv = _causal_taps(buf_ref, gate_c * xin, wconv_ref[...], tm)
    y = (gate_b * conv).astype(BF16)
    o_ref[...] = h + _dot(y, wout_ref[...])


def _conv_mixer(h, g, layer, w_in, w_out, w_conv, j, casts, *, tm=1024):
    b, s, d = h.shape
    tok = pl.BlockSpec((None, tm, d), lambda bi, si: (bi, si, 0))
    (out,), cast = _call(
        functools.partial(_conv_mixer_kernel, tm=tm),
        name="conv_mixer", grid=(b, s // tm),
        in_specs=[tok, _layer(g, layer), _resident(w_in.shape), _layer(w_conv, j), _resident(w_out.shape)],
        out_specs=[tok], out_shape=[jax.ShapeDtypeStruct(h.shape, F32)],
        args=(h, g, w_in, w_conv, w_out),
        scratch_shapes=[pltpu.VMEM((HALO + tm, d), F32)],
        semantics=("parallel", "arbitrary"), casts=casts)
    return out, cast


def _sublane_scan(a, u, row):
    for k in (1, 2, 4):
        keep = row >= k
        a_back = jnp.where(keep, pltpu.roll(a, k, 0), 1.0)
        u_back = jnp.where(keep, pltpu.roll(u, k, 0), 0.0)
        u = a * u_back + u
        a = a * a_back
    return a, u


def _lru_kernel(h_ref, gn_ref, win_ref, cw_ref, cb_ref, wax_ref, ba_ref, bx_ref, lam_ref, wout_ref,
                o_ref, slab_ref, xs_ref, tail_ref, carry_ref, y_ref, *, tm):
    chunk = tm // SUBLANES
    pitch = chunk + SUBLANES
    n_back = LRU_CONV_WIDTH - 1
    lead = n_back * SUBLANES
    vreg = lambda i: slice(i * SUBLANES, (i + 1) * SUBLANES)

    @pl.when(pl.program_id(1) == 0)
    def _():
        tail_ref[...] = jnp.zeros(tail_ref.shape, F32)
        carry_ref[...] = jnp.zeros(carry_ref.shape, F32)

    h = h_ref[...]
    xn = _rms(h, gn_ref[...]).astype(BF16)
    taps = cw_ref[...]
    bias = cb_ref[...]
    half_log_decay = -0.5 * LRU_C * jax.nn.softplus(-lam_ref[...])
    h_prev = carry_ref[...]
    row = lax.broadcasted_iota(jnp.int32, (SUBLANES, LRU_BLOCK), 0)
    block_cols = lambda n: slice(n * LRU_BLOCK, (n + 1) * LRU_BLOCK)

    def conv_and_gate_matmul(n, x_nat):
        cols = block_cols(n)
        for s in range(SUBLANES):
            slab_ref[n, s * pitch:s * pitch + chunk, :] = x_nat[s * chunk:(s + 1) * chunk, :]
        for i in range(chunk):
            xs_ref[n, lead + i * SUBLANES:lead + (i + 1) * SUBLANES, :] = (
                slab_ref[n, pl.ds(i, SUBLANES, stride=pitch), :])
        for k in range(1, n_back + 1):
            cur = xs_ref[n, lead + (chunk - k) * SUBLANES:lead + (chunk - k + 1) * SUBLANES, :]
            prev = tail_ref[n, vreg(n_back - k), :]
            xs_ref[n, lead - k * SUBLANES:lead - (k - 1) * SUBLANES, :] = pltpu.roll(
                jnp.where(row == SUBLANES - 1, prev, cur), 1, 0)
        tail_ref[n] = xs_ref[n, lead + (chunk - n_back) * SUBLANES:lead + chunk * SUBLANES, :]
        xr = bias[:, cols]
        for k in range(LRU_CONV_WIDTH):
            xr = xr + taps[n_back - k:n_back - k + 1, cols] * xs_ref[
                n, lead - k * SUBLANES:lead - k * SUBLANES + tm, :]
        return xr, _dot(xr.astype(BF16), wax_ref[n])

    def recurrence(n, xr, ri, gate):
        cols = block_cols(n)
        t_r = jnp.tanh(ri[:, 0:LRU_BLOCK] + ba_ref[:, cols])
        t_i = jnp.tanh(ri[:, LRU_BLOCK:2 * LRU_BLOCK] + bx_ref[:, cols])
        log_a = (t_r + 1.0) * half_log_decay[:, cols]
        a = jnp.exp(log_a)
        z = -jnp.tanh(log_a) * (a * a + 1.0)
        u = (0.5 * jnp.exp2(0.5 * jnp.log2(z))) * ((t_i + 1.0) * xr)
        h_loc = jnp.zeros((SUBLANES, LRU_BLOCK), F32)
        a_tot = jnp.ones((SUBLANES, LRU_BLOCK), F32)
        for i in range(chunk):
            h_loc = a[vreg(i)] * h_loc + u[vreg(i)]
            a_tot = a_tot * a[vreg(i)]
        a_cum, h_cum = _sublane_scan(a_tot, h_loc, row)
        ends = a_cum * h_prev[:, cols] + h_cum
        h_cur = jnp.where(row == 0, h_prev[:, cols], pltpu.roll(ends, 1, 0))
        for i in range(chunk):
            h_cur = a[vreg(i)] * h_cur + u[vreg(i)]
            slab_ref[n, pl.ds(i, SUBLANES, stride=pitch), :] = h_cur
        for s in range(SUBLANES):
            rows = slice(s * chunk, (s + 1) * chunk)
            hs = slab_ref[n, s * pitch:s * pitch + chunk, :]
            y_ref[rows, cols] = (hs * jax.nn.gelu(gate[rows, :])).astype(BF16)
        return ends[SUBLANES - 1:SUBLANES, :]

    x_all = _dot(xn, win_ref[:, D_RNN:2 * D_RNN])
    gate_all = _dot(xn, win_ref[:, 0:D_RNN])
    carries = []
    for n in range(N_LRU_BLOCKS):
        xr, ri = conv_and_gate_matmul(n, x_all[:, block_cols(n)])
        carries.append(recurrence(n, xr, ri, gate_all[:, block_cols(n)]))

    carry_ref[...] = jnp.concatenate(carries, axis=1)
    o_ref[...] = h + _dot(y_ref[...], wout_ref[...])


def _lru_mixer(h, g, layer, w_in, w_out, conv_w, conv_b, w_ax, b_a, b_x, lam, j, casts, *, tm=512):
    b, s, d = h.shape
    tok = pl.BlockSpec((None, tm, d), lambda bi, si: (bi, si, 0))
    gates = _resident((None, N_LRU_BLOCKS, LRU_BLOCK, 2 * LRU_BLOCK), (j, 0, 0, 0))
    lead = (LRU_CONV_WIDTH - 1) * SUBLANES
    slab_rows = SUBLANES * (tm // SUBLANES + SUBLANES)
    (out,), cast = _call(
        functools.partial(_lru_kernel, tm=tm),
        name="rglru_mixer", grid=(b, s // tm),
        in_specs=[tok, _layer(g, layer), _resident(w_in.shape), _layer(conv_w, j), _layer(conv_b, j),
                  gates, _layer(b_a, j), _layer(b_x, j), _layer(lam, j), _resident(w_out.shape)],
        out_specs=[tok], out_shape=[jax.ShapeDtypeStruct(h.shape, F32)],
        args=(h, g, w_in, conv_w, conv_b, w_ax, b_a, b_x, lam, w_out),
        scratch_shapes=[pltpu.VMEM((N_LRU_BLOCKS, slab_rows, LRU_BLOCK), F32),
                        pltpu.VMEM((N_LRU_BLOCKS, lead + tm, LRU_BLOCK), F32),
                        pltpu.VMEM((N_LRU_BLOCKS, lead, LRU_BLOCK), F32),
                        pltpu.VMEM((1, D_RNN), F32), pltpu.VMEM((tm, D_RNN), BF16)],
        semantics=("parallel", "arbitrary"), casts=casts)
    return out, cast


def _rope_table_kernel(pos_ref, invf_ref, sign_ref, cos_ref, sin_ref):
    pack = HEAD_DIM // ROPE_DIM
    rows = pos_ref.shape[0]
    pos = pos_ref[...].astype(F32)
    lane = lax.broadcasted_iota(jnp.int32, (rows, HEAD_DIM), 1)
    packed = pos[:, pack - 1:pack]
    for k in range(pack - 2, -1, -1):
        packed = jnp.where(lane < (k + 1) * ROPE_DIM, pos[:, k:k + 1], packed)
    ang = packed * invf_ref[...]
    cos = jnp.cos(ang)
    sin = sign_ref[...] * jnp.sin(ang)
    for k in range(pack):
        shift = (HEAD_DIM - k * ROPE_DIM) % HEAD_DIM
        cos_k = pltpu.roll(cos, shift, 1) if shift else cos
        sin_k = pltpu.roll(sin, shift, 1) if shift else sin
        cos_ref[pl.ds(k, rows, stride=pack), :] = jnp.where(lane < ROPE_DIM, cos_k, 1.0)
        sin_ref[pl.ds(k, rows, stride=pack), :] = jnp.where(lane < ROPE_DIM, sin_k, 0.0)


def _rope_tables(positions, casts, *, ts=1024):
    b, s = positions.shape
    pack = HEAD_DIM // ROPE_DIM
    inv_freq = ROPE_THETA ** (-2.0 * jnp.arange(ROPE_HALF, dtype=F32) / ROPE_DIM)
    invf = jnp.tile(inv_freq, 2 * pack)[None]
    sign = jnp.tile(jnp.concatenate([-jnp.ones((ROPE_HALF,), F32), jnp.ones((ROPE_HALF,), F32)]), pack)[None]
    out = jax.ShapeDtypeStruct((b, s, HEAD_DIM), F32)
    tab = pl.BlockSpec((None, ts, HEAD_DIM), lambda bi, si: (bi, si, 0))
    return _call(
        _rope_table_kernel, name="rope_tables", grid=(b, s // ts),
        in_specs=[pl.BlockSpec((None, ts // pack, pack), lambda bi, si: (bi, si, 0)),
                  _resident((1, HEAD_DIM)), _resident((1, HEAD_DIM))],
        out_specs=[tab, tab], out_shape=[out, out],
        args=(positions.reshape(b, s // pack, pack), invf, sign),
        semantics=("parallel", "parallel"), casts=casts)


def _dilation_pitch(dil):
    return dil + 4 if dil % 8 == 0 else dil


def _qkv_kernel(h_ref, cos_ref, sin_ref, gn_ref, wq_ref, wk_ref, wv_ref, q_ref, k_ref, v_ref,
                *regroup_scratch, dil, tm):
    sub = tm // dil
    xn = _rms(h_ref[...], gn_ref[...])
    if dil == 1:
        cos, sin = cos_ref[...], sin_ref[...]
        xn = xn.astype(BF16)
    else:
        slab_ref, xp_ref = regroup_scratch
        pitch = slab_ref.shape[1] // sub
        regroup = lambda ref: jnp.concatenate(
            [ref[pl.ds(r, sub, stride=dil), :] for r in range(dil)], axis=0)
        cos, sin = regroup(cos_ref), regroup(sin_ref)
        for c in range(D_MODEL // HEAD_DIM):
            cols = slice(c * HEAD_DIM, (c + 1) * HEAD_DIM)
            for i in range(sub if pitch != dil else 1):
                rows = dil if pitch != dil else tm
                slab_ref[c, i * pitch:i * pitch + rows, :] = xn[i * dil:i * dil + rows, cols]
            for r in range(dil):
                xp_ref[r * sub:(r + 1) * sub, cols] = (
                    slab_ref[c, pl.ds(r, sub, stride=pitch), :].astype(BF16))
        xn = xp_ref[...]

    def store(out_ref, cols, val):
        for r in range(dil):
            out_ref[r, :, cols] = val[r * sub:(r + 1) * sub, :]

    low_half = lax.broadcasted_iota(jnp.int32, cos.shape, 1) < ROPE_HALF
    for w_ref, out_ref, scale in ((wq_ref, q_ref, SCORE_SCALE), (wk_ref, k_ref, 1.0)):
        z = _dot(xn, w_ref[...])
        cos_s, sin_s = (cos, sin) if scale == 1.0 else (cos * scale, sin * scale)
        for hd in range(N_HEADS):
            cols = slice(hd * HEAD_DIM, (hd + 1) * HEAD_DIM)
            x = z[:, cols]
            partner = jnp.where(low_half, pltpu.roll(x, HEAD_DIM - ROPE_HALF, 1),
                                pltpu.roll(x, ROPE_HALF, 1))
            store(out_ref, cols, (x * cos_s + partner * sin_s).astype(BF16))
    store(v_ref, slice(None), _dot(xn, wv_ref[...]).astype(BF16))


def _qkv(h, cos_t, sin_t, g, layer, w_qkv, group, dil, casts, *, tm=1024):
    b, s, d = h.shape
    n_groups = len(DILATED_PATTERNS)
    tok = lambda width: pl.BlockSpec((None, tm, width), lambda bi, ti: (bi, ti, 0))
    w_spec = lambda which: _resident((d, d), (0, which * n_groups + group))
    out = jax.ShapeDtypeStruct((b, dil, s // dil, d), BF16)
    out_spec = pl.BlockSpec((None, dil, tm // dil, d), lambda bi, ti: (bi, 0, ti, 0))
    pitch = _dilation_pitch(dil)
    regroup_scratch = [] if dil == 1 else [
        pltpu.VMEM((d // HEAD_DIM, (tm // dil) * pitch, HEAD_DIM), F32), pltpu.VMEM((tm, d), BF16)]
    return _call(
        functools.partial(_qkv_kernel, dil=dil, tm=tm),
        name=f"qkv_dil{dil}", grid=(b, s // tm),
        in_specs=[tok(d), tok(HEAD_DIM), tok(HEAD_DIM), _layer(g, layer), w_spec(0), w_spec(1),
                  w_spec(2)],
        out_specs=[out_spec] * 3, out_shape=[out] * 3,
        args=(h, cos_t, sin_t, g, w_qkv, w_qkv, w_qkv),
        scratch_shapes=regroup_scratch, semantics=("parallel", "parallel"), casts=casts)


def _attn_kernel(q_ref, k_ref, v_ref, o_ref, lse_ref, kbuf_ref, vbuf_ref, m_ref, l_ref,
                 *, n_seq, n_qblocks, whole_sequence):
    step = pl.program_id(1)
    blk = ATTN_BLOCK
    tq = n_qblocks * blk
    vcols = lambda hd: slice(2 * hd * HEAD_DIM, (2 * hd + 1) * HEAD_DIM)

    @pl.when(_first_grid_step())
    def _():
        kbuf_ref[...] = jnp.zeros(kbuf_ref.shape, BF16)
        vbuf_ref[...] = jnp.ones(vbuf_ref.shape, BF16)

    qi = lax.broadcasted_iota(jnp.int32, (blk, 2 * blk), 0)
    kj = lax.broadcasted_iota(jnp.int32, (blk, 2 * blk), 1)
    rel = blk + qi - kj
    band = (rel >= 0) & (rel <= blk)
    after_start = False if whole_sequence else step > 0
    band_first = band & ((kj >= blk) | after_start)

    for sq in range(n_seq):
        kbuf_ref[sq, blk:blk + tq, :] = k_ref[sq]
        for hd in range(N_HEADS):
            vbuf_ref[sq, blk:blk + tq, vcols(hd)] = v_ref[sq, :, hd * HEAD_DIM:(hd + 1) * HEAD_DIM]
        for qb in range(n_qblocks):
            rows = slice(qb * blk, (qb + 1) * blk)
            keys = slice(qb * blk, (qb + 2) * blk)
            valid = band_first if qb == 0 else band
            for hd in range(N_HEADS):
                cols = slice(hd * HEAD_DIM, (hd + 1) * HEAD_DIM)
                stat = slice(hd * LSE_LANES_PER_HEAD, (hd + 1) * LSE_LANES_PER_HEAD)
                s2 = lax.dot_general(q_ref[sq, rows, cols], kbuf_ref[sq, keys, cols],
                                     (((1,), (1,)), ((), ())), preferred_element_type=F32)
                s2 = jnp.where(valid, s2, MASK_VALUE)
                m2 = jnp.max(s2, axis=-1, keepdims=True)
                p = jnp.exp2(s2 - m2)
                ol = _dot(p.astype(BF16),
                          vbuf_ref[sq, keys, 2 * hd * HEAD_DIM:2 * (hd + 1) * HEAD_DIM])
                l = ol[:, HEAD_DIM:]
                o_ref[sq, rows, cols] = (ol[:, :HEAD_DIM] * (1.0 / l)).astype(BF16)
                m_ref[sq, rows, stat] = jnp.broadcast_to(m2, (blk, LSE_LANES_PER_HEAD))
                l_ref[sq, rows, stat] = l[:, stat]
        if not whole_sequence:
            kbuf_ref[sq, 0:blk, :] = kbuf_ref[sq, tq:tq + blk, :]
            vbuf_ref[sq, 0:blk, :] = vbuf_ref[sq, tq:tq + blk, :]
    lse_ref[...] = (m_ref[...] + jnp.log2(l_ref[...])) * math.log(2.0)


def _band_attention(q, k, v, *, units_per_step=64):
    n, sub, d = q.shape
    blocks_per_step = units_per_step // N_HEADS
    n_qblocks = min(blocks_per_step, sub // ATTN_BLOCK)
    n_seq = blocks_per_step // n_qblocks
    tq = n_qblocks * ATTN_BLOCK
    tok = lambda width: pl.BlockSpec((n_seq, tq, width), lambda ni, ti: (ni, ti, 0))
    return pl.pallas_call(
        functools.partial(_attn_kernel, n_seq=n_seq, n_qblocks=n_qblocks, whole_sequence=(tq == sub)),
        grid=(n // n_seq, sub // tq),
        in_specs=[tok(d)] * 3,
        out_specs=[tok(d), tok(HEAD_DIM)],
        out_shape=[jax.ShapeDtypeStruct((n, sub, d), BF16),
                   jax.ShapeDtypeStruct((n, sub, HEAD_DIM), F32)],
        scratch_shapes=[pltpu.VMEM((n_seq, ATTN_BLOCK + tq, d), BF16),
                        pltpu.VMEM((n_seq, ATTN_BLOCK + tq, 2 * d), BF16),
                        pltpu.VMEM((n_seq, tq, HEAD_DIM), F32), pltpu.VMEM((n_seq, tq, HEAD_DIM), F32)],
        compiler_params=_params(("arbitrary", "arbitrary")),
        name=f"band_attn_len{sub}",
    )(q, k, v)


def _combine_kernel(h_ref, o0_ref, l0_ref, o1_ref, l1_ref, o2_ref, l2_ref, wo_ref, out_ref,
                    lse1_ref, lse2_ref, slab1_ref, slab2_ref, mix_ref, *, tm):
    def natural_lse(l_ref, nat_ref, dil):
        for r in range(dil):
            nat_ref[pl.ds(r, tm // dil, stride=dil), :] = l_ref[r]
        return nat_ref[...]

    def natural_out(o_ref, slab_ref, dil, hd):
        sub = tm // dil
        pitch = slab_ref.shape[1] // sub
        cols = slice(hd * HEAD_DIM, (hd + 1) * HEAD_DIM)
        for r in range(dil):
            slab_ref[hd, pl.ds(r, sub, stride=pitch), :] = o_ref[r, :, cols].astype(F32)
        if pitch == dil:
            return slab_ref[hd]
        return jnp.concatenate([slab_ref[hd, i * pitch:i * pitch + dil, :] for i in range(sub)], axis=0)

    dil1, dil2 = DILATED_PATTERNS[1][1], DILATED_PATTERNS[2][1]
    lses = (l0_ref[0], natural_lse(l1_ref, lse1_ref, dil1), natural_lse(l2_ref, lse2_ref, dil2))
    m = jnp.maximum(jnp.maximum(lses[0], lses[1]), lses[2])
    es = [jnp.exp(l - m) for l in lses]
    inv = 1.0 / (es[0] + es[1] + es[2])
    w1, w2 = es[1] * inv, es[2] * inv
    for hd in range(N_HEADS):
        cols = slice(hd * HEAD_DIM, (hd + 1) * HEAD_DIM)
        lane = slice(hd * LSE_LANES_PER_HEAD, hd * LSE_LANES_PER_HEAD + 1)
        o0 = o0_ref[0, :, cols].astype(F32)
        mixed = o0
        for w, o in ((w1, natural_out(o1_ref, slab1_ref, dil1, hd)),
                     (w2, natural_out(o2_ref, slab2_ref, dil2, hd))):
            mixed = mixed + jnp.broadcast_to(w[:, lane], (tm, HEAD_DIM)) * (o - o0)
        mix_ref[:, cols] = mixed.astype(BF16)
    out_ref[...] = h_ref[...] + _dot(mix_ref[...], wo_ref[...])


def _combine(h, outs, lses, w_o, *, tm=1024):
    b, s, d = h.shape
    tok = pl.BlockSpec((None, tm, d), lambda bi, ti: (bi, ti, 0))
    specs, slabs = [], []
    for (_, dil) in DILATED_PATTERNS:
        for width in (d, HEAD_DIM):
            specs.append(pl.BlockSpec((None, dil, tm // dil, width), lambda bi, ti: (bi, 0, ti, 0)))
        if dil > 1:
            slabs.append(pltpu.VMEM((N_HEADS, (tm // dil) * _dilation_pitch(dil), HEAD_DIM), F32))
    args = [a for pair in zip(outs, lses) for a in pair]
    return pl.pallas_call(
        functools.partial(_combine_kernel, tm=tm),
        grid=(b, s // tm),
        in_specs=[tok] + specs + [_resident(w_o.shape)],
        out_specs=tok,
        out_shape=jax.ShapeDtypeStruct(h.shape, F32),
        scratch_shapes=[pltpu.VMEM((tm, HEAD_DIM), F32), pltpu.VMEM((tm, HEAD_DIM), F32)] + slabs
        + [pltpu.VMEM((tm, d), BF16)],
        compiler_params=_params(("parallel", "parallel")),
        name="attn_combine",
    )(h, *args, w_o)


def _attention_mixer(h, cos_t, sin_t, g, layer, w_qkv, w_o, casts):
    b, s, d = h.shape
    outs, lses, cast = [], [], ()
    for group, (window, dil) in enumerate(DILATED_PATTERNS):
        assert window // dil == ATTN_BLOCK
        (q, k, v), group_cast = _qkv(h, cos_t, sin_t, g, layer, w_qkv, group, dil,
                                     casts if group == 0 else ())
        cast = cast or group_cast
        sub = s // dil
        o_g, l_g = _band_attention(q.reshape(b * dil, sub, d), k.reshape(b * dil, sub, d),
                                   v.reshape(b * dil, sub, d))
        outs.append(o_g.reshape(b, dil, sub, d))
        lses.append(l_g.reshape(b, dil, sub, HEAD_DIM))
    return _combine(h, outs, lses, w_o), cast


def kernel(x, p, positions, norm_mix, norm_mlp, norm_ple, norm_final, sc_w_in, sc_w_conv, sc_w_out,
           attn_w_qkv, attn_w_o, lru_w_in, lru_conv_w, lru_conv_b, lru_w_a, lru_b_a, lru_w_x,
           lru_b_x, lru_lambda, lru_w_out, mlp_w_up, mlp_w_down, ple_w_gate, ple_w_proj):
    norm_mix, norm_mlp, norm_ple = _rows(norm_mix), _rows(norm_mlp), _rows(norm_ple)
    lru_w_ax = (0.5 * jnp.concatenate([lru_w_a, lru_w_x], axis=-1)).astype(BF16)
    lru_conv_b, lru_b_a, lru_b_x, lru_lambda = (_rows(lru_conv_b), _rows(0.5 * lru_b_a),
                                                _rows(0.5 * lru_b_x), _rows(lru_lambda))

    mixer_weights = ((sc_w_in, sc_w_out), (attn_w_qkv, attn_w_o), (lru_w_in, lru_w_out))
    plan = []
    for i in range(DEPTH):
        kind, j = i % N_MIXERS, i // N_MIXERS
        plan.append([(w, j) for w in mixer_weights[kind]])
        plan.append([(mlp_w_up, i), (mlp_w_down, i), (ple_w_gate, i)])
    plan.append([])
    (cos_t, sin_t), ready = _rope_tables(positions, plan[0] + [(ple_w_proj, i) for i in range(DEPTH)])
    ready, w_projs = ready[:-DEPTH], ready[-DEPTH:]

    h = x
    for i in range(DEPTH):
        kind, j = i % N_MIXERS, i // N_MIXERS
        if kind == 0:
            h, ready = _conv_mixer(h, norm_mix, i, *ready, sc_w_conv, j, plan[2 * i + 1])
        elif kind == 1:
            h, ready = _attention_mixer(h, cos_t, sin_t, norm_mix, i, *ready, plan[2 * i + 1])
        else:
            h, ready = _lru_mixer(h, norm_mix, i, *ready, lru_conv_w, lru_conv_b, lru_w_ax, lru_b_a,
                                  lru_b_x, lru_lambda, j, plan[2 * i + 1])
        h, ready = _post(h, p, norm_mlp, norm_ple, norm_final[None], *ready, w_projs[i], i,
                         plan[2 * i + 2])
    return h
```

```python
import functools
import math

import jax
import jax.numpy as jnp
from jax import lax
from jax.experimental import pallas as pl
from jax.experimental.pallas import tpu as pltpu

D_MODEL = 1024
DEPTH = 4
N_MIXERS = 3
PLE_DIM = 256
D_FF = 4 * D_MODEL
RMS_EPS = 1e-6
SC_WIDTH = 3
HEAD_DIM = 128
N_HEADS = D_MODEL // HEAD_DIM
DILATED_PATTERNS = ((128, 1), (512, 4), (2048, 16))
ROPE_THETA = 500000.0
ROPE_DIM = HEAD_DIM // 4
ROPE_HALF = ROPE_DIM // 2
D_RNN = 1280
N_LRU_BLOCKS = 10
LRU_BLOCK = D_RNN // N_LRU_BLOCKS
LRU_CONV_WIDTH = 4
LRU_C = 8.0

ATTN_BLOCK = 128
LSE_LANES_PER_HEAD = HEAD_DIM // N_HEADS
SUBLANES = 8
HALO = SUBLANES
MASK_VALUE = -1e30
SCORE_SCALE = HEAD_DIM ** -0.5 * math.log2(math.e)
VMEM_LIMIT_BYTES = 56 * 1024 * 1024
BF16_SUBLANES = 16

F32 = jnp.float32
BF16 = jnp.bfloat16


def _rms(x, g):
    return x * lax.rsqrt(jnp.mean(x * x, axis=-1, keepdims=True) + RMS_EPS) * g


def _dot(a, b):
    return jnp.dot(a, b, preferred_element_type=F32)


def _resident(shape, index=None):
    index = (0,) * len(shape) if index is None else index
    return pl.BlockSpec(shape, lambda *_: index, pipeline_mode=pl.Buffered(1))


def _layer(stacked, layer):
    return _resident((None,) + stacked.shape[1:], (layer, 0, 0))


def _rows(stacked):
    return stacked[:, None, :]


def _params(semantics):
    return pltpu.CompilerParams(dimension_semantics=semantics, vmem_limit_bytes=VMEM_LIMIT_BYTES)


def _call(body, *, name, grid, in_specs, out_specs, out_shape, args, scratch_shapes=(), semantics,
          casts=()):
    n_in, n_out, n_casts = len(in_specs), len(out_specs), len(casts)
    n_steps = math.prod(grid)

    def step_of(*idx):
        step = 0
        for i, n in zip(idx, grid):
            step = step * n + i
        return step

    in_specs, out_specs, out_shape, args = list(in_specs), list(out_specs), list(out_shape), list(args)
    for stack, layer in casts:
        _, rows, cols = stack.shape
        chunk = rows // n_steps
        assert rows % n_steps == 0 and chunk % BF16_SUBLANES == 0
        in_specs.append(pl.BlockSpec((None, chunk, cols), lambda *idx, layer=layer: (layer, step_of(*idx), 0)))
        out_specs.append(pl.BlockSpec((chunk, cols), lambda *idx: (step_of(*idx), 0)))
        out_shape.append(jax.ShapeDtypeStruct((rows, cols), BF16))
        args.append(stack)

    def kernel(*refs):
        main_in, rest = refs[:n_in], refs[n_in:]
        cast_in, rest = rest[:n_casts], rest[n_casts:]
        main_out, rest = rest[:n_out], rest[n_out:]
        cast_out, scratch = rest[:n_casts], rest[n_casts:]
        for src, dst in zip(cast_in, cast_out):
            dst[...] = src[...].astype(BF16)
        body(*main_in, *main_out, *scratch)

    outs = pl.pallas_call(
        kernel, grid=grid, in_specs=in_specs, out_specs=out_specs, out_shape=out_shape,
        scratch_shapes=list(scratch_shapes), compiler_params=_params(semantics), name=name)(*args)
    return outs[:n_out], outs[n_out:]


def _first_grid_step():
    return (pl.program_id(0) == 0) & (pl.program_id(1) == 0)


def _post_kernel(h_ref, p_ref, gm_ref, gp_ref, gf_ref, wup_ref, wdn_ref, wg_ref, wp_ref, o_ref,
                 *, final, tf):
    h = h_ref[...]
    xn = _rms(h, gm_ref[...]).astype(BF16)
    acc = h
    for f in range(D_FF // tf):
        u = _dot(xn, wup_ref[:, f * tf:(f + 1) * tf])
        a = jnp.square(jnp.maximum(u, 0.0)).astype(BF16)
        acc = acc + _dot(a, wdn_ref[f * tf:(f + 1) * tf, :])
    xg = _rms(acc, gp_ref[...]).astype(BF16)
    gate = jax.nn.sigmoid(_dot(xg, wg_ref[...]))
    proj = _dot(p_ref[...].astype(BF16), wp_ref[...])
    out = acc + gate * proj
    if final:
        out = _rms(out, gf_ref[...])
    o_ref[...] = out


def _post(h, p, g_mlp, g_ple, g_final, w_up, w_down, w_gate, w_proj, layer, casts, *, tm=1024, tf=512):
    b, s, d = h.shape
    tok = pl.BlockSpec((None, tm, d), lambda bi, si: (bi, si, 0))
    p_spec = pl.BlockSpec((None, None, tm, PLE_DIM), lambda bi, si: (layer, bi, si, 0))
    (out,), cast = _call(
        functools.partial(_post_kernel, final=(layer == DEPTH - 1), tf=tf),
        name="mlp_ple", grid=(b, s // tm),
        in_specs=[tok, p_spec, _layer(g_mlp, layer), _layer(g_ple, layer), _resident((1, d)),
                  _resident(w_up.shape), _resident(w_down.shape), _resident(w_gate.shape),
                  _resident(w_proj.shape)],
        out_specs=[tok], out_shape=[jax.ShapeDtypeStruct(h.shape, F32)],
        args=(h, p, g_mlp, g_ple, g_final, w_up, w_down, w_gate, w_proj),
        semantics=("parallel", "parallel"), casts=casts)
    return out, cast


def _causal_taps(buf_ref, x, w, tm):
    k_width = w.shape[0]
    buf_ref[HALO:HALO + tm, :] = x
    out = w[k_width - 1:k_width] * x
    for back in range(1, k_width):
        out = out + w[k_width - 1 - back:k_width - back] * buf_ref[HALO - back:HALO - back + tm, :]
    buf_ref[0:HALO, :] = buf_ref[tm:tm + HALO, :]
    return out


def _conv_mixer_kernel(h_ref, gn_ref, win_ref, wconv_ref, wout_ref, o_ref, buf_ref, *, tm):
    @pl.when(pl.program_id(1) == 0)
    def _():
        buf_ref[0:HALO, :] = jnp.zeros((HALO, D_MODEL), F32)

    h = h_ref[...]
    xn = _rms(h, gn_ref[...]).astype(BF16)
    gate_b = _dot(xn, win_ref[:, 0:D_MODEL])
    gate_c = _dot(xn, win_ref[:, D_MODEL:2 * D_MODEL])
    xin = _dot(xn, win_ref[:, 2 * D_MODEL:3 * D_MODEL])
    conv = _causal_taps(buf_ref, gate_c * xin, wconv_ref[...], tm)
    y = (gate_b * conv).astype(BF16)
    o_ref[...] = h + _dot(y, wout_ref[...])


def _conv_mixer(h, g, layer, w_in, w_out, w_conv, j, casts, *, tm=1024):
    b, s, d = h.shape
    tok = pl.BlockSpec((None, tm, d), lambda bi, si: (bi, si, 0))
    (out,), cast = _call(
        functools.partial(_conv_mixer_kernel, tm=tm),
        name="conv_mixer", grid=(b, s // tm),
        in_specs=[tok, _layer(g, layer), _resident(w_in.shape), _layer(w_conv, j), _resident(w_out.shape)],
        out_specs=[tok], out_shape=[jax.ShapeDtypeStruct(h.shape, F32)],
        args=(h, g, w_in, w_conv, w_out),
        scratch_shapes=[pltpu.VMEM((HALO + tm, d), F32)],
        semantics=("parallel", "arbitrary"), casts=casts)
    return out, cast


def _sublane_scan(a, u, row):
    for k in (1, 2, 4):
        keep = row >= k
        a_back = jnp.where(keep, pltpu.roll(a, k, 0), 1.0)
        u_back = jnp.where(keep, pltpu.roll(u, k, 0), 0.0)
        u = a * u_back + u
        a = a * a_back
    return a, u


def _lru_kernel(h_ref, gn_ref, win_ref, cw_ref, cb_ref, wax_ref, ba_ref, bx_ref, lam_ref, wout_ref,
                o_ref, slab_ref, xs_ref, tail_ref, carry_ref, y_ref, *, tm):
    chunk = tm // SUBLANES
    pitch = chunk + SUBLANES
    n_back = LRU_CONV_WIDTH - 1
    lead = n_back * SUBLANES
    vreg = lambda i: slice(i * SUBLANES, (i + 1) * SUBLANES)

    @pl.when(pl.program_id(1) == 0)
    def _():
        tail_ref[...] = jnp.zeros(tail_ref.shape, F32)
        carry_ref[...] = jnp.zeros(carry_ref.shape, F32)

    taps = cw_ref[...]
    bias = cb_ref[...]
    half_log_decay = -0.5 * LRU_C * jax.nn.softplus(-lam_ref[...])
    row = lax.broadcasted_iota(jnp.int32, (SUBLANES, LRU_BLOCK), 0)
    block_cols = lambda n: slice(n * LRU_BLOCK, (n + 1) * LRU_BLOCK)

    def conv_and_gate_matmul(t, n, x_nat):
        cols, slot = block_cols(n), t * N_LRU_BLOCKS + n
        for s in range(SUBLANES):
            slab_ref[slot, s * pitch:s * pitch + chunk, :] = x_nat[s * chunk:(s + 1) * chunk, :]
        for i in range(chunk):
            xs_ref[slot, lead + i * SUBLANES:lead + (i + 1) * SUBLANES, :] = (
                slab_ref[slot, pl.ds(i, SUBLANES, stride=pitch), :])
        for k in range(1, n_back + 1):
            cur = xs_ref[slot, lead + (chunk - k) * SUBLANES:lead + (chunk - k + 1) * SUBLANES, :]
            prev = tail_ref[slot, vreg(n_back - k), :]
            xs_ref[slot, lead - k * SUBLANES:lead - (k - 1) * SUBLANES, :] = pltpu.roll(
                jnp.where(row == SUBLANES - 1, prev, cur), 1, 0)
        tail_ref[slot] = xs_ref[slot, lead + (chunk - n_back) * SUBLANES:lead + chunk * SUBLANES, :]
        xr = bias[:, cols]
        for k in range(LRU_CONV_WIDTH):
            xr = xr + taps[n_back - k:n_back - k + 1, cols] * xs_ref[
                slot, lead - k * SUBLANES:lead - k * SUBLANES + tm, :]
        return xr, _dot(xr.astype(BF16), wax_ref[n])

    def recurrence(t, n, xr, ri, gate):
        cols, slot = block_cols(n), t * N_LRU_BLOCKS + n
        h_prev = carry_ref[t:t + 1, cols]
        t_r = jnp.tanh(ri[:, 0:LRU_BLOCK] + ba_ref[:, cols])
        t_i = jnp.tanh(ri[:, LRU_BLOCK:2 * LRU_BLOCK] + bx_ref[:, cols])
        log_a = (t_r + 1.0) * half_log_decay[:, cols]
        a = jnp.exp(log_a)
        z = -jnp.tanh(log_a) * (a * a + 1.0)
        u = (0.5 * jnp.exp2(0.5 * jnp.log2(z))) * ((t_i + 1.0) * xr)
        h_loc = jnp.zeros((SUBLANES, LRU_BLOCK), F32)
        a_tot = jnp.ones((SUBLANES, LRU_BLOCK), F32)
        for i in range(chunk):
            h_loc = a[vreg(i)] * h_loc + u[vreg(i)]
            a_tot = a_tot * a[vreg(i)]
        a_cum, h_cum = _sublane_scan(a_tot, h_loc, row)
        ends = a_cum * h_prev + h_cum
        h_cur = jnp.where(row == 0, h_prev, pltpu.roll(ends, 1, 0))
        for i in range(chunk):
            h_cur = a[vreg(i)] * h_cur + u[vreg(i)]
            slab_ref[slot, pl.ds(i, SUBLANES, stride=pitch), :] = h_cur
        for s in range(SUBLANES):
            rows = slice(s * chunk, (s + 1) * chunk)
            hs = slab_ref[slot, s * pitch:s * pitch + chunk, :]
            y_ref[t, rows, cols] = (hs * jax.nn.gelu(gate[rows, :])).astype(BF16)
        carry_ref[t:t + 1, cols] = ends[SUBLANES - 1:SUBLANES, :]

    h0, h1 = h_ref[0], h_ref[1]
    xn0 = _rms(h0, gn_ref[...]).astype(BF16)
    xn1 = _rms(h1, gn_ref[...]).astype(BF16)
    pair_width = 2 * LRU_BLOCK
    n_pairs = N_LRU_BLOCKS // 2
    half = lambda n: slice((n % 2) * LRU_BLOCK, (n % 2 + 1) * LRU_BLOCK)

    in_piece = lambda xn, lo: _dot(xn, win_ref[:, lo:lo + pair_width])
    x0 = _dot(xn0, win_ref[:, D_RNN:2 * D_RNN])
    piece_lo = [D_RNN + p * pair_width for p in range(n_pairs)] + [p * pair_width for p in range(n_pairs)]
    pieces, gate0 = [], []
    for n in range(N_LRU_BLOCKS):
        xr, ri = conv_and_gate_matmul(0, n, x0[:, block_cols(n)])
        if n % 2 == 0:
            gate0.append(in_piece(xn0, (n // 2) * pair_width))
        pieces.append(in_piece(xn1, piece_lo[n]))
        recurrence(0, n, xr, ri, gate0[n // 2][:, half(n)])
    out0 = h0
    for n in range(N_LRU_BLOCKS):
        xr, ri = conv_and_gate_matmul(1, n, pieces[n // 2][:, half(n)])
        if n % 2 == 0:
            pair_cols = slice((n // 2) * pair_width, (n // 2 + 1) * pair_width)
            out0 = out0 + _dot(y_ref[0, :, pair_cols], wout_ref[pair_cols, :])
        recurrence(1, n, xr, ri, pieces[n_pairs + n // 2][:, half(n)])
    o_ref[0] = out0
    o_ref[1] = h1 + _dot(y_ref[1], wout_ref[...])


def _lru_mixer(h, g, layer, w_in, w_out, conv_w, conv_b, w_ax, b_a, b_x, lam, j, casts, *, tm=512):
    b, s, d = h.shape
    rows_per_step = 2
    tok = pl.BlockSpec((rows_per_step, tm, d), lambda bi, si: (bi, si, 0))
    gates = _resident((None, N_LRU_BLOCKS, LRU_BLOCK, 2 * LRU_BLOCK), (j, 0, 0, 0))
    lead = (LRU_CONV_WIDTH - 1) * SUBLANES
    slab_rows = SUBLANES * (tm // SUBLANES + SUBLANES)
    slots = rows_per_step * N_LRU_BLOCKS
    (out,), cast = _call(
        functools.partial(_lru_kernel, tm=tm),
        name="rglru_mixer", grid=(b // rows_per_step, s // tm),
        in_specs=[tok, _layer(g, layer), _resident(w_in.shape), _layer(conv_w, j), _layer(conv_b, j),
                  gates, _layer(b_a, j), _layer(b_x, j), _layer(lam, j), _resident(w_out.shape)],
        out_specs=[tok], out_shape=[jax.ShapeDtypeStruct(h.shape, F32)],
        args=(h, g, w_in, conv_w, conv_b, w_ax, b_a, b_x, lam, w_out),
        scratch_shapes=[pltpu.VMEM((slots, slab_rows, LRU_BLOCK), F32),
                        pltpu.VMEM((slots, lead + tm, LRU_BLOCK), F32),
                        pltpu.VMEM((slots, lead, LRU_BLOCK), F32),
                        pltpu.VMEM((rows_per_step, D_RNN), F32),
                        pltpu.VMEM((rows_per_step, tm, D_RNN), BF16)],
        semantics=("parallel", "arbitrary"), casts=casts)
    return out, cast


def _rope_table_kernel(pos_ref, invf_ref, sign_ref, cos_ref, sin_ref):
    pack = HEAD_DIM // ROPE_DIM
    rows = pos_ref.shape[0]
    pos = pos_ref[...].astype(F32)
    lane = lax.broadcasted_iota(jnp.int32, (rows, HEAD_DIM), 1)
    packed = pos[:, pack - 1:pack]
    for k in range(pack - 2, -1, -1):
        packed = jnp.where(lane < (k + 1) * ROPE_DIM, pos[:, k:k + 1], packed)
    ang = packed * invf_ref[...]
    cos = jnp.cos(ang)
    sin = sign_ref[...] * jnp.sin(ang)
    for k in range(pack):
        shift = (HEAD_DIM - k * ROPE_DIM) % HEAD_DIM
        cos_k = pltpu.roll(cos, shift, 1) if shift else cos
        sin_k = pltpu.roll(sin, shift, 1) if shift else sin
        cos_ref[pl.ds(k, rows, stride=pack), :] = jnp.where(lane < ROPE_DIM, cos_k, 1.0)
        sin_ref[pl.ds(k, rows, stride=pack), :] = jnp.where(lane < ROPE_DIM, sin_k, 0.0)


def _rope_tables(positions, casts, *, ts=1024):
    b, s = positions.shape
    pack = HEAD_DIM // ROPE_DIM
    inv_freq = ROPE_THETA ** (-2.0 * jnp.arange(ROPE_HALF, dtype=F32) / ROPE_DIM)
    invf = jnp.tile(inv_freq, 2 * pack)[None]
    sign = jnp.tile(jnp.concatenate([-jnp.ones((ROPE_HALF,), F32), jnp.ones((ROPE_HALF,), F32)]), pack)[None]
    out = jax.ShapeDtypeStruct((b, s, HEAD_DIM), F32)
    tab = pl.BlockSpec((None, ts, HEAD_DIM), lambda bi, si: (bi, si, 0))
    return _call(
        _rope_table_kernel, name="rope_tables", grid=(b, s // ts),
        in_specs=[pl.BlockSpec((None, ts // pack, pack), lambda bi, si: (bi, si, 0)),
                  _resident((1, HEAD_DIM)), _resident((1, HEAD_DIM))],
        out_specs=[tab, tab], out_shape=[out, out],
        args=(positions.reshape(b, s // pack, pack), invf, sign),
        semantics=("parallel", "parallel"), casts=casts)


def _dilation_pitch(dil):
    return dil + 4 if dil % 8 == 0 else dil


def _qkv_kernel(h_ref, cos_ref, sin_ref, gn_ref, wq_ref, wk_ref, wv_ref, q_ref, k_ref, v_ref,
                *regroup_scratch, dil, tm):
    sub = tm // dil
    xn = _rms(h_ref[...], gn_ref[...])
    if dil == 1:
        cos, sin = cos_ref[...], sin_ref[...]
        xn = xn.astype(BF16)
    else:
        slab_ref, xp_ref = regroup_scratch
        pitch = slab_ref.shape[1] // sub
        regroup = lambda ref: jnp.concatenate(
            [ref[pl.ds(r, sub, stride=dil), :] for r in range(dil)], axis=0)
        cos, sin = regroup(cos_ref), regroup(sin_ref)
        for c in range(D_MODEL // HEAD_DIM):
            cols = slice(c * HEAD_DIM, (c + 1) * HEAD_DIM)
            for i in range(sub if pitch != dil else 1):
                rows = dil if pitch != dil else tm
                slab_ref[c, i * pitch:i * pitch + rows, :] = xn[i * dil:i * dil + rows, cols]
            for r in range(dil):
                xp_ref[r * sub:(r + 1) * sub, cols] = (
                    slab_ref[c, pl.ds(r, sub, stride=pitch), :].astype(BF16))
        xn = xp_ref[...]

    def store(out_ref, cols, val):
        for r in range(dil):
            out_ref[r, :, cols] = val[r * sub:(r + 1) * sub, :]

    low_half = lax.broadcasted_iota(jnp.int32, cos.shape, 1) < ROPE_HALF
    for w_ref, out_ref, scale in ((wq_ref, q_ref, SCORE_SCALE), (wk_ref, k_ref, 1.0)):
        z = _dot(xn, w_ref[...])
        cos_s, sin_s = (cos, sin) if scale == 1.0 else (cos * scale, sin * scale)
        for hd in range(N_HEADS):
            cols = slice(hd * HEAD_DIM, (hd + 1) * HEAD_DIM)
            x = z[:, cols]
            partner = jnp.where(low_half, pltpu.roll(x, HEAD_DIM - ROPE_HALF, 1),
                                pltpu.roll(x, ROPE_HALF, 1))
            store(out_ref, cols, (x * cos_s + partner * sin_s).astype(BF16))
    store(v_ref, slice(None), _dot(xn, wv_ref[...]).astype(BF16))


def _qkv(h, cos_t, sin_t, g, layer, w_qkv, group, dil, casts, *, tm=1024):
    b, s, d = h.shape
    n_groups = len(DILATED_PATTERNS)
    tok = lambda width: pl.BlockSpec((None, tm, width), lambda bi, ti: (bi, ti, 0))
    w_spec = lambda which: _resident((d, d), (0, which * n_groups + group))
    out = jax.ShapeDtypeStruct((b, dil, s // dil, d), BF16)
    out_spec = pl.BlockSpec((None, dil, tm // dil, d), lambda bi, ti: (bi, 0, ti, 0))
    pitch = _dilation_pitch(dil)
    regroup_scratch = [] if dil == 1 else [
        pltpu.VMEM((d // HEAD_DIM, (tm // dil) * pitch, HEAD_DIM), F32), pltpu.VMEM((tm, d), BF16)]
    return _call(
        functools.partial(_qkv_kernel, dil=dil, tm=tm),
        name=f"qkv_dil{dil}", grid=(b, s // tm),
        in_specs=[tok(d), tok(HEAD_DIM), tok(HEAD_DIM), _layer(g, layer), w_spec(0), w_spec(1),
                  w_spec(2)],
        out_specs=[out_spec] * 3, out_shape=[out] * 3,
        args=(h, cos_t, sin_t, g, w_qkv, w_qkv, w_qkv),
        scratch_shapes=regroup_scratch, semantics=("parallel", "parallel"), casts=casts)


def _attn_kernel(q_ref, k_ref, v_ref, o_ref, lse_ref, kbuf_ref, vbuf_ref, m_ref, l_ref,
                 *, n_seq, n_qblocks, whole_sequence):
    step = pl.program_id(1)
    blk = ATTN_BLOCK
    tq = n_qblocks * blk
    vcols = lambda hd: slice(2 * hd * HEAD_DIM, (2 * hd + 1) * HEAD_DIM)

    @pl.when(_first_grid_step())
    def _():
        kbuf_ref[...] = jnp.zeros(kbuf_ref.shape, BF16)
        vbuf_ref[...] = jnp.ones(vbuf_ref.shape, BF16)

    qi = lax.broadcasted_iota(jnp.int32, (blk, 2 * blk), 0)
    kj = lax.broadcasted_iota(jnp.int32, (blk, 2 * blk), 1)
    rel = blk + qi - kj
    band = (rel >= 0) & (rel <= blk)
    after_start = False if whole_sequence else step > 0
    band_first = band & ((kj >= blk) | after_start)

    for sq in range(n_seq):
        kbuf_ref[sq, blk:blk + tq, :] = k_ref[sq]
        for hd in range(N_HEADS):
            vbuf_ref[sq, blk:blk + tq, vcols(hd)] = v_ref[sq, :, hd * HEAD_DIM:(hd + 1) * HEAD_DIM]
        for qb in range(n_qblocks):
            rows = slice(qb * blk, (qb + 1) * blk)
            keys = slice(qb * blk, (qb + 2) * blk)
            valid = band_first if qb == 0 else band
            for hd in range(N_HEADS):
                cols = slice(hd * HEAD_DIM, (hd + 1) * HEAD_DIM)
                stat = slice(hd * LSE_LANES_PER_HEAD, (hd + 1) * LSE_LANES_PER_HEAD)
                s2 = lax.dot_general(q_ref[sq, rows, cols], kbuf_ref[sq, keys, cols],
                                     (((1,), (1,)), ((), ())), preferred_element_type=F32)
                s2 = jnp.where(valid, s2, MASK_VALUE)
                m2 = jnp.max(s2, axis=-1, keepdims=True)
                p = jnp.exp2(s2 - m2)
                ol = _dot(p.astype(BF16),
                          vbuf_ref[sq, keys, 2 * hd * HEAD_DIM:2 * (hd + 1) * HEAD_DIM])
                l = ol[:, HEAD_DIM:]
                o_ref[sq, rows, cols] = (ol[:, :HEAD_DIM] * (1.0 / l)).astype(BF16)
                m_ref[sq, rows, stat] = jnp.broadcast_to(m2, (blk, LSE_LANES_PER_HEAD))
                l_ref[sq, rows, stat] = l[:, stat]
        if not whole_sequence:
            kbuf_ref[sq, 0:blk, :] = kbuf_ref[sq, tq:tq + blk, :]
            vbuf_ref[sq, 0:blk, :] = vbuf_ref[sq, tq:tq + blk, :]
    lse_ref[...] = (m_ref[...] + jnp.log2(l_ref[...])) * math.log(2.0)


def _band_attention(q, k, v, *, units_per_step=64):
    n, sub, d = q.shape
    blocks_per_step = units_per_step // N_HEADS
    n_qblocks = min(blocks_per_step, sub // ATTN_BLOCK)
    n_seq = blocks_per_step // n_qblocks
    tq = n_qblocks * ATTN_BLOCK
    tok = lambda width: pl.BlockSpec((n_seq, tq, width), lambda ni, ti: (ni, ti, 0))
    return pl.pallas_call(
        functools.partial(_attn_kernel, n_seq=n_seq, n_qblocks=n_qblocks, whole_sequence=(tq == sub)),
        grid=(n // n_seq, sub // tq),
        in_specs=[tok(d)] * 3,
        out_specs=[tok(d), tok(HEAD_DIM)],
        out_shape=[jax.ShapeDtypeStruct((n, sub, d), BF16),
                   jax.ShapeDtypeStruct((n, sub, HEAD_DIM), F32)],
        scratch_shapes=[pltpu.VMEM((n_seq, ATTN_BLOCK + tq, d), BF16),
                        pltpu.VMEM((n_seq, ATTN_BLOCK + tq, 2 * d), BF16),
                        pltpu.VMEM((n_seq, tq, HEAD_DIM), F32), pltpu.VMEM((n_seq, tq, HEAD_DIM), F32)],
        compiler_params=_params(("arbitrary", "arbitrary")),
        name=f"band_attn_len{sub}",
    )(q, k, v)


def _combine_kernel(h_ref, o0_ref, l0_ref, o1_ref, l1_ref, o2_ref, l2_ref, wo_ref, out_ref,
                    lse1_ref, lse2_ref, slab1_ref, slab2_ref, mix_ref, *, tm):
    def natural_lse(l_ref, nat_ref, dil):
        for r in range(dil):
            nat_ref[pl.ds(r, tm // dil, stride=dil), :] = l_ref[r]
        return nat_ref[...]

    def natural_out(o_ref, slab_ref, dil, hd):
        sub = tm // dil
        pitch = slab_ref.shape[1] // sub
        cols = slice(hd * HEAD_DIM, (hd + 1) * HEAD_DIM)
        for r in range(dil):
            slab_ref[hd, pl.ds(r, sub, stride=pitch), :] = o_ref[r, :, cols].astype(F32)
        if pitch == dil:
            return slab_ref[hd]
        return jnp.concatenate([slab_ref[hd, i * pitch:i * pitch + dil, :] for i in range(sub)], axis=0)

    dil1, dil2 = DILATED_PATTERNS[1][1], DILATED_PATTERNS[2][1]
    lses = (l0_ref[0], natural_lse(l1_ref, lse1_ref, dil1), natural_lse(l2_ref, lse2_ref, dil2))
    m = jnp.maximum(jnp.maximum(lses[0], lses[1]), lses[2])
    es = [jnp.exp(l - m) for l in lses]
    inv = 1.0 / (es[0] + es[1] + es[2])
    w1, w2 = es[1] * inv, es[2] * inv
    for hd in range(N_HEADS):
        cols = slice(hd * HEAD_DIM, (hd + 1) * HEAD_DIM)
        lane = slice(hd * LSE_LANES_PER_HEAD, hd * LSE_LANES_PER_HEAD + 1)
        o0 = o0_ref[0, :, cols].astype(F32)
        mixed = o0
        for w, o in ((w1, natural_out(o1_ref, slab1_ref, dil1, hd)),
                     (w2, natural_out(o2_ref, slab2_ref, dil2, hd))):
            mixed = mixed + jnp.broadcast_to(w[:, lane], (tm, HEAD_DIM)) * (o - o0)
        mix_ref[:, cols] = mixed.astype(BF16)
    out_ref[...] = h_ref[...] + _dot(mix_ref[...], wo_ref[...])


def _combine(h, outs, lses, w_o, *, tm=1024):
    b, s, d = h.shape
    tok = pl.BlockSpec((None, tm, d), lambda bi, ti: (bi, ti, 0))
    specs, slabs = [], []
    for (_, dil) in DILATED_PATTERNS:
        for width in (d, HEAD_DIM):
            specs.append(pl.BlockSpec((None, dil, tm // dil, width), lambda bi, ti: (bi, 0, ti, 0)))
        if dil > 1:
            slabs.append(pltpu.VMEM((N_HEADS, (tm // dil) * _dilation_pitch(dil), HEAD_DIM), F32))
    args = [a for pair in zip(outs, lses) for a in pair]
    return pl.pallas_call(
        functools.partial(_combine_kernel, tm=tm),
        grid=(b, s // tm),
        in_specs=[tok] + specs + [_resident(w_o.shape)],
        out_specs=tok,
        out_shape=jax.ShapeDtypeStruct(h.shape, F32),
        scratch_shapes=[pltpu.VMEM((tm, HEAD_DIM), F32), pltpu.VMEM((tm, HEAD_DIM), F32)] + slabs
        + [pltpu.VMEM((tm, d), BF16)],
        compiler_params=_params(("parallel", "parallel")),
        name="attn_combine",
    )(h, *args, w_o)


def _attention_mixer(h, cos_t, sin_t, g, layer, w_qkv, w_o, casts):
    b, s, d = h.shape
    outs, lses, cast = [], [], ()
    for group, (window, dil) in enumerate(DILATED_PATTERNS):
        assert window // dil == ATTN_BLOCK
        (q, k, v), group_cast = _qkv(h, cos_t, sin_t, g, layer, w_qkv, group, dil,
                                     casts if group == 0 else ())
        cast = cast or group_cast
        sub = s // dil
        o_g, l_g = _band_attention(q.reshape(b * dil, sub, d), k.reshape(b * dil, sub, d),
                                   v.reshape(b * dil, sub, d))
        outs.append(o_g.reshape(b, dil, sub, d))
        lses.append(l_g.reshape(b, dil, sub, HEAD_DIM))
    return _combine(h, outs, lses, w_o), cast


def kernel(x, p, positions, norm_mix, norm_mlp, norm_ple, norm_final, sc_w_in, sc_w_conv, sc_w_out,
           attn_w_qkv, attn_w_o, lru_w_in, lru_conv_w, lru_conv_b, lru_w_a, lru_b_a, lru_w_x,
           lru_b_x, lru_lambda, lru_w_out, mlp_w_up, mlp_w_down, ple_w_gate, ple_w_proj):
    norm_mix, norm_mlp, norm_ple = _rows(norm_mix), _rows(norm_mlp), _rows(norm_ple)
    lru_w_ax = (0.5 * jnp.concatenate([lru_w_a, lru_w_x], axis=-1)).astype(BF16)
    lru_conv_b, lru_b_a, lru_b_x, lru_lambda = (_rows(lru_conv_b), _rows(0.5 * lru_b_a),
                                                _rows(0.5 * lru_b_x), _rows(lru_lambda))

    mixer_weights = ((sc_w_in, sc_w_out), (attn_w_qkv, attn_w_o), (lru_w_in, lru_w_out))
    plan = []
    for i in range(DEPTH):
        kind, j = i % N_MIXERS, i // N_MIXERS
        plan.append([(w, j) for w in mixer_weights[kind]])
        plan.append([(mlp_w_up, i), (mlp_w_down, i), (ple_w_gate, i)])
    plan.append([])
    (cos_t, sin_t), ready = _rope_tables(positions, plan[0] + [(ple_w_proj, i) for i in range(DEPTH)])
    ready, w_projs = ready[:-DEPTH], ready[-DEPTH:]

    h = x
    for i in range(DEPTH):
        kind, j = i % N_MIXERS, i // N_MIXERS
        if kind == 0:
            h, ready = _conv_mixer(h, norm_mix, i, *ready, sc_w_conv, j, plan[2 * i + 1])
        elif kind == 1:
            h, ready = _attention_mixer(h, cos_t, sin_t, norm_mix, i, *ready, plan[2 * i + 1])
        else:
            h, ready = _lru_mixer(h, norm_mix, i, *ready, lru_conv_w, lru_conv_b, lru_w_ax, lru_b_a,
                                  lru_b_x, lru_lambda, j, plan[2 * i + 1])
        h, ready = _post(h, p, norm_mlp, norm_ple, norm_final[None], *ready, w_projs[i], i,
                         plan[2 * i + 2])
    return h
```

```python
import functools
import math

import jax
import jax.numpy as jnp
from jax import lax
from jax.experimental import pallas as pl
from jax.experimental.pallas import tpu as pltpu

D_MODEL = 1024
DEPTH = 4
N_MIXERS = 3
PLE_DIM = 256
D_FF = 4 * D_MODEL
RMS_EPS = 1e-6
SC_WIDTH = 3
HEAD_DIM = 128
N_HEADS = D_MODEL // HEAD_DIM
DILATED_PATTERNS = ((128, 1), (512, 4), (2048, 16))
ROPE_THETA = 500000.0
ROPE_DIM = HEAD_DIM // 4
ROPE_HALF = ROPE_DIM // 2
D_RNN = 1280
N_LRU_BLOCKS = 10
LRU_BLOCK = D_RNN // N_LRU_BLOCKS
LRU_CONV_WIDTH = 4
LRU_C = 8.0

ATTN_BLOCK = 128
LSE_LANES_PER_HEAD = HEAD_DIM // N_HEADS
SUBLANES = 8
HALO = SUBLANES
MASK_VALUE = -1e30
SCORE_SCALE = HEAD_DIM ** -0.5 * math.log2(math.e)
VMEM_LIMIT_BYTES = 56 * 1024 * 1024
BF16_SUBLANES = 16

F32 = jnp.float32
BF16 = jnp.bfloat16


def _rms(x, g):
    return x * lax.rsqrt(jnp.mean(x * x, axis=-1, keepdims=True) + RMS_EPS) * g


def _dot(a, b):
    return jnp.dot(a, b, preferred_element_type=F32)


def _resident(shape, index=None):
    index = (0,) * len(shape) if index is None else index
    return pl.BlockSpec(shape, lambda *_: index, pipeline_mode=pl.Buffered(1))


def _layer(stacked, layer):
    return _resident((None,) + stacked.shape[1:], (layer, 0, 0))


def _rows(stacked):
    return stacked[:, None, :]


def _params(semantics):
    return pltpu.CompilerParams(dimension_semantics=semantics, vmem_limit_bytes=VMEM_LIMIT_BYTES)


def _call(body, *, name, grid, in_specs, out_specs, out_shape, args, scratch_shapes=(), semantics,
          casts=()):
    n_in, n_out, n_casts = len(in_specs), len(out_specs), len(casts)
    n_steps = math.prod(grid)

    def step_of(*idx):
        step = 0
        for i, n in zip(idx, grid):
            step = step * n + i
        return step

    in_specs, out_specs, out_shape, args = list(in_specs), list(out_specs), list(out_shape), list(args)
    for stack, layer in casts:
        _, rows, cols = stack.shape
        chunk = rows // n_steps
        assert rows % n_steps == 0 and chunk % BF16_SUBLANES == 0
        in_specs.append(pl.BlockSpec((None, chunk, cols), lambda *idx, layer=layer: (layer, step_of(*idx), 0)))
        out_specs.append(pl.BlockSpec((chunk, cols), lambda *idx: (step_of(*idx), 0)))
        out_shape.append(jax.ShapeDtypeStruct((rows, cols), BF16))
        args.append(stack)

    def kernel(*refs):
        main_in, rest = refs[:n_in], refs[n_in:]
        cast_in, rest = rest[:n_casts], rest[n_casts:]
        main_out, rest = rest[:n_out], rest[n_out:]
        cast_out, scratch = rest[:n_casts], rest[n_casts:]
        for src, dst in zip(cast_in, cast_out):
            dst[...] = src[...].astype(BF16)
        body(*main_in, *main_out, *scratch)

    outs = pl.pallas_call(
        kernel, grid=grid, in_specs=in_specs, out_specs=out_specs, out_shape=out_shape,
        scratch_shapes=list(scratch_shapes), compiler_params=_params(semantics), name=name)(*args)
    return outs[:n_out], outs[n_out:]


def _first_grid_step():
    return (pl.program_id(0) == 0) & (pl.program_id(1) == 0)


def _post_kernel(h_ref, p_ref, gm_ref, gp_ref, gf_ref, wup_ref, wdn_ref, wg_ref, wp_ref, o_ref,
                 *, final, tf):
    h = h_ref[...]
    xn = _rms(h, gm_ref[...]).astype(BF16)
    acc = h
    for f in range(D_FF // tf):
        u = _dot(xn, wup_ref[:, f * tf:(f + 1) * tf])
        a = jnp.square(jnp.maximum(u, 0.0)).astype(BF16)
        acc = acc + _dot(a, wdn_ref[f * tf:(f + 1) * tf, :])
    xg = _rms(acc, gp_ref[...]).astype(BF16)
    gate = jax.nn.sigmoid(_dot(xg, wg_ref[...]))
    proj = _dot(p_ref[...].astype(BF16), wp_ref[...])
    out = acc + gate * proj
    if final:
        out = _rms(out, gf_ref[...])
    o_ref[...] = out


def _post(h, p, g_mlp, g_ple, g_final, w_up, w_down, w_gate, w_proj, layer, casts, *, tm=1024, tf=512):
    b, s, d = h.shape
    tok = pl.BlockSpec((None, tm, d), lambda bi, si: (bi, si, 0))
    p_spec = pl.BlockSpec((None, None, tm, PLE_DIM), lambda bi, si: (layer, bi, si, 0))
    (out,), cast = _call(
        functools.partial(_post_kernel, final=(layer == DEPTH - 1), tf=tf),
        name="mlp_ple", grid=(b, s // tm),
        in_specs=[tok, p_spec, _layer(g_mlp, layer), _layer(g_ple, layer), _resident((1, d)),
                  _resident(w_up.shape), _resident(w_down.shape), _resident(w_gate.shape),
                  _resident(w_proj.shape)],
        out_specs=[tok], out_shape=[jax.ShapeDtypeStruct(h.shape, F32)],
        args=(h, p, g_mlp, g_ple, g_final, w_up, w_down, w_gate, w_proj),
        semantics=("parallel", "parallel"), casts=casts)
    return out, cast


def _causal_taps(buf_ref, x, w, tm):
    k_width = w.shape[0]
    buf_ref[HALO:HALO + tm, :] = x
    out = w[k_width - 1:k_width] * x
    for back in range(1, k_width):
        out = out + w[k_width - 1 - back:k_width - back] * buf_ref[HALO - back:HALO - back + tm, :]
    buf_ref[0:HALO, :] = buf_ref[tm:tm + HALO, :]
    return out


def _conv_mixer_kernel(h_ref, gn_ref, win_ref, wconv_ref, wout_ref, o_ref, buf_ref, *, tm):
    @pl.when(pl.program_id(1) == 0)
    def _():
        buf_ref[0:HALO, :] = jnp.zeros((HALO, D_MODEL), F32)

    h = h_ref[...]
    xn = _rms(h, gn_ref[...]).astype(BF16)
    gate_b = _dot(xn, win_ref[:, 0:D_MODEL])
    gate_c = _dot(xn, win_ref[:, D_MODEL:2 * D_MODEL])
    xin = _dot(xn, win_ref[:, 2 * D_MODEL:3 * D_MODEL])
    conv = _causal_taps(buf_ref, gate_c * xin, wconv_ref[...], tm)
    y = (gate_b * conv).astype(BF16)
    o_ref[...] = h + _dot(y, wout_ref[...])


def _conv_mixer(h, g, layer, w_in, w_out, w_conv, j, casts, *, tm=1024):
    b, s, d = h.shape
    tok = pl.BlockSpec((None, tm, d), lambda bi, si: (bi, si, 0))
    (out,), cast = _call(
        functools.partial(_conv_mixer_kernel, tm=tm),
        name="conv_mixer", grid=(b, s // tm),
        in_specs=[tok, _layer(g, layer), _resident(w_in.shape), _layer(w_conv, j), _resident(w_out.shape)],
        out_specs=[tok], out_shape=[jax.ShapeDtypeStruct(h.shape, F32)],
        args=(h, g, w_in, w_conv, w_out),
        scratch_shapes=[pltpu.VMEM((HALO + tm, d), F32)],
        semantics=("parallel", "arbitrary"), casts=casts)
    return out, cast


def _sublane_scan(a, u, row):
    for k in (1, 2, 4):
        keep = row >= k
        a_back = jnp.where(keep, pltpu.roll(a, k, 0), 1.0)
        u_back = jnp.where(keep, pltpu.roll(u, k, 0), 0.0)
        u = a * u_back + u
        a = a * a_back
    return a, u


def _lru_kernel(h_ref, gn_ref, win_ref, cw_ref, cb_ref, wax_ref, ba_ref, bx_ref, lam_ref, wout_ref,
                o_ref, slab_ref, xs_ref, tail_ref, carry_ref, y_ref, *, tm):
    chunk = tm // SUBLANES
    pitch = chunk + SUBLANES
    n_back = LRU_CONV_WIDTH - 1
    lead = n_back * SUBLANES
    vreg = lambda i: slice(i * SUBLANES, (i + 1) * SUBLANES)

    @pl.when(pl.program_id(1) == 0)
    def _():
        tail_ref[...] = jnp.zeros(tail_ref.shape, F32)
        carry_ref[...] = jnp.zeros(carry_ref.shape, F32)

    h = h_ref[...]
    xn = _rms(h, gn_ref[...]).astype(BF16)
    taps = cw_ref[...]
    bias = cb_ref[...]
    half_log_decay = -0.5 * LRU_C * jax.nn.softplus(-lam_ref[...])
    h_prev = carry_ref[...]
    row = lax.broadcasted_iota(jnp.int32, (SUBLANES, LRU_BLOCK), 0)
    block_cols = lambda n: slice(n * LRU_BLOCK, (n + 1) * LRU_BLOCK)

    def conv_and_gate_matmul(n, x_nat):
        cols = block_cols(n)
        for s in range(SUBLANES):
            slab_ref[n, s * pitch:s * pitch + chunk, :] = x_nat[s * chunk:(s + 1) * chunk, :]
        for i in range(chunk):
            xs_ref[n, lead + i * SUBLANES:lead + (i + 1) * SUBLANES, :] = (
                slab_ref[n, pl.ds(i, SUBLANES, stride=pitch), :])
        for k in range(1, n_back + 1):
            cur = xs_ref[n, lead + (chunk - k) * SUBLANES:lead + (chunk - k + 1) * SUBLANES, :]
            prev = tail_ref[n, vreg(n_back - k), :]
            xs_ref[n, lead - k * SUBLANES:lead - (k - 1) * SUBLANES, :] = pltpu.roll(
                jnp.where(row == SUBLANES - 1, prev, cur), 1, 0)
        tail_ref[n] = xs_ref[n, lead + (chunk - n_back) * SUBLANES:lead + chunk * SUBLANES, :]
        xr = bias[:, cols]
        for k in range(LRU_CONV_WIDTH):
            xr = xr + taps[n_back - k:n_back - k + 1, cols] * xs_ref[
                n, lead - k * SUBLANES:lead - k * SUBLANES + tm, :]
        return xr, _dot(xr.astype(BF16), wax_ref[n])

    def recurrence(n, xr, ri, gate):
        cols = block_cols(n)
        t_r = jnp.tanh(ri[:, 0:LRU_BLOCK] + ba_ref[:, cols])
        t_i = jnp.tanh(ri[:, LRU_BLOCK:2 * LRU_BLOCK] + bx_ref[:, cols])
        log_a = (t_r + 1.0) * half_log_decay[:, cols]
        a = jnp.exp(log_a)
        z = -jnp.tanh(log_a) * (a * a + 1.0)
        u = (0.5 * jnp.exp2(0.5 * jnp.log2(z))) * ((t_i + 1.0) * xr)
        h_loc = jnp.zeros((SUBLANES, LRU_BLOCK), F32)
        a_tot = jnp.ones((SUBLANES, LRU_BLOCK), F32)
        for i in range(chunk):
            h_loc = a[vreg(i)] * h_loc + u[vreg(i)]
            a_tot = a_tot * a[vreg(i)]
        a_cum, h_cum = _sublane_scan(a_tot, h_loc, row)
        ends = a_cum * h_prev[:, cols] + h_cum
        h_cur = jnp.where(row == 0, h_prev[:, cols], pltpu.roll(ends, 1, 0))
        for i in range(chunk):
            h_cur = a[vreg(i)] * h_cur + u[vreg(i)]
            slab_ref[n, pl.ds(i, SUBLANES, stride=pitch), :] = h_cur
        for s in range(SUBLANES):
            rows = slice(s * chunk, (s + 1) * chunk)
            hs = slab_ref[n, s * pitch:s * pitch + chunk, :]
            y_ref[rows, cols] = (hs * jax.nn.gelu(gate[rows, :])).astype(BF16)
        return ends[SUBLANES - 1:SUBLANES, :]

    x_all = _dot(xn, win_ref[:, D_RNN:2 * D_RNN])
    gate_all = _dot(xn, win_ref[:, 0:D_RNN])
    carries = []
    for n in range(N_LRU_BLOCKS):
        xr, ri = conv_and_gate_matmul(n, x_all[:, block_cols(n)])
        carries.append(recurrence(n, xr, ri, gate_all[:, block_cols(n)]))

    carry_ref[...] = jnp.concatenate(carries, axis=1)
    o_ref[...] = h + _dot(y_ref[...], wout_ref[...])


def _lru_mixer(h, g, layer, w_in, w_out, conv_w, conv_b, w_ax, b_a, b_x, lam, j, casts, *, tm=512):
    b, s, d = h.shape
    tok = pl.BlockSpec((None, tm, d), lambda bi, si: (bi, si, 0))
    gates = _resident((None, N_LRU_BLOCKS, LRU_BLOCK, 2 * LRU_BLOCK), (j, 0, 0, 0))
    lead = (LRU_CONV_WIDTH - 1) * SUBLANES
    slab_rows = SUBLANES * (tm // SUBLANES + SUBLANES)
    (out,), cast = _call(
        functools.partial(_lru_kernel, tm=tm),
        name="rglru_mixer", grid=(b, s // tm),
        in_specs=[tok, _layer(g, layer), _resident(w_in.shape), _layer(conv_w, j), _layer(conv_b, j),
                  gates, _layer(b_a, j), _layer(b_x, j), _layer(lam, j), _resident(w_out.shape)],
        out_specs=[tok], out_shape=[jax.ShapeDtypeStruct(h.shape, F32)],
        args=(h, g, w_in, conv_w, conv_b, w_ax, b_a, b_x, lam, w_out),
        scratch_shapes=[pltpu.VMEM((N_LRU_BLOCKS, slab_rows, LRU_BLOCK), F32),
                        pltpu.VMEM((N_LRU_BLOCKS, lead + tm, LRU_BLOCK), F32),
                        pltpu.VMEM((N_LRU_BLOCKS, lead, LRU_BLOCK), F32),
                        pltpu.VMEM((1, D_RNN), F32), pltpu.VMEM((tm, D_RNN), BF16)],
        semantics=("parallel", "arbitrary"), casts=casts)
    return out, cast


def _rope_table_kernel(pos_ref, invf_ref, sign_ref, cos_ref, sin_ref):
    pack = HEAD_DIM // ROPE_DIM
    rows = pos_ref.shape[0]
    pos = pos_ref[...].astype(F32)
    lane = lax.broadcasted_iota(jnp.int32, (rows, HEAD_DIM), 1)
    packed = pos[:, pack - 1:pack]
    for k in range(pack - 2, -1, -1):
        packed = jnp.where(lane < (k + 1) * ROPE_DIM, pos[:, k:k + 1], packed)
    ang = packed * invf_ref[...]
    cos = jnp.cos(ang)
    sin = sign_ref[...] * jnp.sin(ang)
    for k in range(pack):
        shift = (HEAD_DIM - k * ROPE_DIM) % HEAD_DIM
        cos_k = pltpu.roll(cos, shift, 1) if shift else cos
        sin_k = pltpu.roll(sin, shift, 1) if shift else sin
        cos_ref[pl.ds(k, rows, stride=pack), :] = jnp.where(lane < ROPE_DIM, cos_k, 1.0)
        sin_ref[pl.ds(k, rows, stride=pack), :] = jnp.where(lane < ROPE_DIM, sin_k, 0.0)


def _rope_tables(positions, casts, *, ts=1024):
    b, s = positions.shape
    pack = HEAD_DIM // ROPE_DIM
    inv_freq = ROPE_THETA ** (-2.0 * jnp.arange(ROPE_HALF, dtype=F32) / ROPE_DIM)
    invf = jnp.tile(inv_freq, 2 * pack)[None]
    sign = jnp.tile(jnp.concatenate([-jnp.ones((ROPE_HALF,), F32), jnp.ones((ROPE_HALF,), F32)]), pack)[None]
    out = jax.ShapeDtypeStruct((b, s, HEAD_DIM), F32)
    tab = pl.BlockSpec((None, ts, HEAD_DIM), lambda bi, si: (bi, si, 0))
    return _call(
        _rope_table_kernel, name="rope_tables", grid=(b, s // ts),
        in_specs=[pl.BlockSpec((None, ts // pack, pack), lambda bi, si: (bi, si, 0)),
                  _resident((1, HEAD_DIM)), _resident((1, HEAD_DIM))],
        out_specs=[tab, tab], out_shape=[out, out],
        args=(positions.reshape(b, s // pack, pack), invf, sign),
        semantics=("parallel", "parallel"), casts=casts)


def _dilation_pitch(dil):
    return dil + 4 if dil % 8 == 0 else dil


def _qkv_kernel(h_ref, cos_ref, sin_ref, gn_ref, wq_ref, wk_ref, wv_ref, q_ref, k_ref, v_ref,
                *regroup_scratch, dil, tm):
    sub = tm // dil
    xn = _rms(h_ref[...], gn_ref[...])
    if dil == 1:
        cos, sin = cos_ref[...], sin_ref[...]
        xn = xn.astype(BF16)
    else:
        slab_ref, xp_ref = regroup_scratch
        pitch = slab_ref.shape[1] // sub
        regroup = lambda ref: jnp.concatenate(
            [ref[pl.ds(r, sub, stride=dil), :] for r in range(dil)], axis=0)
        cos, sin = regroup(cos_ref), regroup(sin_ref)
        for c in range(D_MODEL // HEAD_DIM):
            cols = slice(c * HEAD_DIM, (c + 1) * HEAD_DIM)
            for i in range(sub if pitch != dil else 1):
                rows = dil if pitch != dil else tm
                slab_ref[c, i * pitch:i * pitch + rows, :] = xn[i * dil:i * dil + rows, cols]
            for r in range(dil):
                xp_ref[r * sub:(r + 1) * sub, cols] = (
                    slab_ref[c, pl.ds(r, sub, stride=pitch), :].astype(BF16))
        xn = xp_ref[...]

    def store(out_ref, cols, val):
        for r in range(dil):
            out_ref[r, :, cols] = val[r * sub:(r + 1) * sub, :]

    low_half = lax.broadcasted_iota(jnp.int32, cos.shape, 1) < ROPE_HALF
    for w_ref, out_ref, scale in ((wq_ref, q_ref, SCORE_SCALE), (wk_ref, k_ref, 1.0)):
        z = _dot(xn, w_ref[...])
        cos_s, sin_s = (cos, sin) if scale == 1.0 else (cos * scale, sin * scale)
        for hd in range(N_HEADS):
            cols = slice(hd * HEAD_DIM, (hd + 1) * HEAD_DIM)
            x = z[:, cols]
            partner = jnp.where(low_half, pltpu.roll(x, HEAD_DIM - ROPE_HALF, 1),
                                pltpu.roll(x, ROPE_HALF, 1))
            store(out_ref, cols, (x * cos_s + partner * sin_s).astype(BF16))
    store(v_ref, slice(None), _dot(xn, wv_ref[...]).astype(BF16))


def _qkv(h, cos_t, sin_t, g, layer, w_qkv, group, dil, casts, *, tm=1024):
    b, s, d = h.shape
    n_groups = len(DILATED_PATTERNS)
    tok = lambda width: pl.BlockSpec((None, tm, width), lambda bi, ti: (bi, ti, 0))
    w_spec = lambda which: _resident((d, d), (0, which * n_groups + group))
    out = jax.ShapeDtypeStruct((b, dil, s // dil, d), BF16)
    out_spec = pl.BlockSpec((None, dil, tm // dil, d), lambda bi, ti: (bi, 0, ti, 0))
    pitch = _dilation_pitch(dil)
    regroup_scratch = [] if dil == 1 else [
        pltpu.VMEM((d // HEAD_DIM, (tm // dil) * pitch, HEAD_DIM), F32), pltpu.VMEM((tm, d), BF16)]
    return _call(
        functools.partial(_qkv_kernel, dil=dil, tm=tm),
        name=f"qkv_dil{dil}", grid=(b, s // tm),
        in_specs=[tok(d), tok(HEAD_DIM), tok(HEAD_DIM), _layer(g, layer), w_spec(0), w_spec(1),
                  w_spec(2)],
        out_specs=[out_spec] * 3, out_shape=[out] * 3,
        args=(h, cos_t, sin_t, g, w_qkv, w_qkv, w_qkv),
        scratch_shapes=regroup_scratch, semantics=("parallel", "parallel"), casts=casts)


def _attn_kernel(q_ref, k_ref, v_ref, o_ref, lse_ref, kbuf_ref, vbuf_ref, m_ref, l_ref,
                 *, n_seq, n_qblocks, whole_sequence):
    step = pl.program_id(1)
    blk = ATTN_BLOCK
    tq = n_qblocks * blk
    vcols = lambda hd: slice(2 * hd * HEAD_DIM, (2 * hd + 1) * HEAD_DIM)

    @pl.when(_first_grid_step())
    def _():
        kbuf_ref[...] = jnp.zeros(kbuf_ref.shape, BF16)
        vbuf_ref[...] = jnp.ones(vbuf_ref.shape, BF16)

    qi = lax.broadcasted_iota(jnp.int32, (blk, 2 * blk), 0)
    kj = lax.broadcasted_iota(jnp.int32, (blk, 2 * blk), 1)
    rel = blk + qi - kj
    band = (rel >= 0) & (rel <= blk)
    after_start = False if whole_sequence else step > 0
    band_first = band & ((kj >= blk) | after_start)

    for sq in range(n_seq):
        kbuf_ref[sq, blk:blk + tq, :] = k_ref[sq]
        for hd in range(N_HEADS):
            vbuf_ref[sq, blk:blk + tq, vcols(hd)] = v_ref[sq, :, hd * HEAD_DIM:(hd + 1) * HEAD_DIM]
        for qb in range(n_qblocks):
            rows = slice(qb * blk, (qb + 1) * blk)
            keys = slice(qb * blk, (qb + 2) * blk)
            valid = band_first if qb == 0 else band
            for hd in range(N_HEADS):
                cols = slice(hd * HEAD_DIM, (hd + 1) * HEAD_DIM)
                stat = slice(hd * LSE_LANES_PER_HEAD, (hd + 1) * LSE_LANES_PER_HEAD)
                s2 = lax.dot_general(q_ref[sq, rows, cols], kbuf_ref[sq, keys, cols],
                                     (((1,), (1,)), ((), ())), preferred_element_type=F32)
                s2 = jnp.where(valid, s2, MASK_VALUE)
                m2 = jnp.max(s2, axis=-1, keepdims=True)
                p = jnp.exp2(s2 - m2)
                ol = _dot(p.astype(BF16),
                          vbuf_ref[sq, keys, 2 * hd * HEAD_DIM:2 * (hd + 1) * HEAD_DIM])
                l = ol[:, HEAD_DIM:]
                o_ref[sq, rows, cols] = (ol[:, :HEAD_DIM] * (1.0 / l)).astype(BF16)
                m_ref[sq, rows, stat] = jnp.broadcast_to(m2, (blk, LSE_LANES_PER_HEAD))
                l_ref[sq, rows, stat] = l[:, stat]
        if not whole_sequence:
            kbuf_ref[sq, 0:blk, :] = kbuf_ref[sq, tq:tq + blk, :]
            vbuf_ref[sq, 0:blk, :] = vbuf_ref[sq, tq:tq + blk, :]
    lse_ref[...] = (m_ref[...] + jnp.log2(l_ref[...])) * math.log(2.0)


def _band_attention(q, k, v, *, units_per_step=128):
    n, sub, d = q.shape
    blocks_per_step = units_per_step // N_HEADS
    n_qblocks = min(blocks_per_step, sub // ATTN_BLOCK)
    n_seq = blocks_per_step // n_qblocks
    tq = n_qblocks * ATTN_BLOCK
    tok = lambda width: pl.BlockSpec((n_seq, tq, width), lambda ni, ti: (ni, ti, 0))
    return pl.pallas_call(
        functools.partial(_attn_kernel, n_seq=n_seq, n_qblocks=n_qblocks, whole_sequence=(tq == sub)),
        grid=(n // n_seq, sub // tq),
        in_specs=[tok(d)] * 3,
        out_specs=[tok(d), tok(HEAD_DIM)],
        out_shape=[jax.ShapeDtypeStruct((n, sub, d), BF16),
                   jax.ShapeDtypeStruct((n, sub, HEAD_DIM), F32)],
        scratch_shapes=[pltpu.VMEM((n_seq, ATTN_BLOCK + tq, d), BF16),
                        pltpu.VMEM((n_seq, ATTN_BLOCK + tq, 2 * d), BF16),
                        pltpu.VMEM((n_seq, tq, HEAD_DIM), F32), pltpu.VMEM((n_seq, tq, HEAD_DIM), F32)],
        compiler_params=_params(("arbitrary", "arbitrary")),
        name=f"band_attn_len{sub}",
    )(q, k, v)


def _combine_kernel(h_ref, o0_ref, l0_ref, o1_ref, l1_ref, o2_ref, l2_ref, wo_ref, out_ref,
                    lse1_ref, lse2_ref, slab1_ref, slab2_ref, mix_ref, *, tm):
    def natural_lse(l_ref, nat_ref, dil):
        for r in range(dil):
            nat_ref[pl.ds(r, tm // dil, stride=dil), :] = l_ref[r]
        return nat_ref[...]

    def natural_out(o_ref, slab_ref, dil, hd):
        sub = tm // dil
        pitch = slab_ref.shape[1] // sub
        cols = slice(hd * HEAD_DIM, (hd + 1) * HEAD_DIM)
        for r in range(dil):
            slab_ref[hd, pl.ds(r, sub, stride=pitch), :] = o_ref[r, :, cols].astype(F32)
        if pitch == dil:
            return slab_ref[hd]
        return jnp.concatenate([slab_ref[hd, i * pitch:i * pitch + dil, :] for i in range(sub)], axis=0)

    dil1, dil2 = DILATED_PATTERNS[1][1], DILATED_PATTERNS[2][1]
    lses = (l0_ref[0], natural_lse(l1_ref, lse1_ref, dil1), natural_lse(l2_ref, lse2_ref, dil2))
    m = jnp.maximum(jnp.maximum(lses[0], lses[1]), lses[2])
    es = [jnp.exp(l - m) for l in lses]
    inv = 1.0 / (es[0] + es[1] + es[2])
    w1, w2 = es[1] * inv, es[2] * inv
    for hd in range(N_HEADS):
        cols = slice(hd * HEAD_DIM, (hd + 1) * HEAD_DIM)
        lane = slice(hd * LSE_LANES_PER_HEAD, hd * LSE_LANES_PER_HEAD + 1)
        o0 = o0_ref[0, :, cols].astype(F32)
        mixed = o0
        for w, o in ((w1, natural_out(o1_ref, slab1_ref, dil1, hd)),
                     (w2, natural_out(o2_ref, slab2_ref, dil2, hd))):
            mixed = mixed + jnp.broadcast_to(w[:, lane], (tm, HEAD_DIM)) * (o - o0)
        mix_ref[:, cols] = mixed.astype(BF16)
    out_ref[...] = h_ref[...] + _dot(mix_ref[...], wo_ref[...])


def _combine(h, outs, lses, w_o, *, tm=1024):
    b, s, d = h.shape
    tok = pl.BlockSpec((None, tm, d), lambda bi, ti: (bi, ti, 0))
    specs, slabs = [], []
    for (_, dil) in DILATED_PATTERNS:
        for width in (d, HEAD_DIM):
            specs.append(pl.BlockSpec((None, dil, tm // dil, width), lambda bi, ti: (bi, 0, ti, 0)))
        if dil > 1:
            slabs.append(pltpu.VMEM((N_HEADS, (tm // dil) * _dilation_pitch(dil), HEAD_DIM), F32))
    args = [a for pair in zip(outs, lses) for a in pair]
    return pl.pallas_call(
        functools.partial(_combine_kernel, tm=tm),
        grid=(b, s // tm),
        in_specs=[tok] + specs + [_resident(w_o.shape)],
        out_specs=tok,
        out_shape=jax.ShapeDtypeStruct(h.shape, F32),
        scratch_shapes=[pltpu.VMEM((tm, HEAD_DIM), F32), pltpu.VMEM((tm, HEAD_DIM), F32)] + slabs
        + [pltpu.VMEM((tm, d), BF16)],
        compiler_params=_params(("parallel", "parallel")),
        name="attn_combine",
    )(h, *args, w_o)


def _attention_mixer(h, cos_t, sin_t, g, layer, w_qkv, w_o, casts):
    b, s, d = h.shape
    outs, lses, cast = [], [], ()
    for group, (window, dil) in enumerate(DILATED_PATTERNS):
        assert window // dil == ATTN_BLOCK
        (q, k, v), group_cast = _qkv(h, cos_t, sin_t, g, layer, w_qkv, group, dil,
                                     casts if group == 0 else ())
        cast = cast or group_cast
        sub = s // dil
        o_g, l_g = _band_attention(q.reshape(b * dil, sub, d), k.reshape(b * dil, sub, d),
                                   v.reshape(b * dil, sub, d))
        outs.append(o_g.reshape(b, dil, sub, d))
        lses.append(l_g.reshape(b, dil, sub, HEAD_DIM))
    return _combine(h, outs, lses, w_o), cast


def kernel(x, p, positions, norm_mix, norm_mlp, norm_ple, norm_final, sc_w_in, sc_w_conv, sc_w_out,
           attn_w_qkv, attn_w_o, lru_w_in, lru_conv_w, lru_conv_b, lru_w_a, lru_b_a, lru_w_x,
           lru_b_x, lru_lambda, lru_w_out, mlp_w_up, mlp_w_down, ple_w_gate, ple_w_proj):
    norm_mix, norm_mlp, norm_ple = _rows(norm_mix), _rows(norm_mlp), _rows(norm_ple)
    lru_w_ax = (0.5 * jnp.concatenate([lru_w_a, lru_w_x], axis=-1)).astype(BF16)
    lru_conv_b, lru_b_a, lru_b_x, lru_lambda = (_rows(lru_conv_b), _rows(0.5 * lru_b_a),
                                                _rows(0.5 * lru_b_x), _rows(lru_lambda))

    mixer_weights = ((sc_w_in, sc_w_out), (attn_w_qkv, attn_w_o), (lru_w_in, lru_w_out))
    plan = []
    for i in range(DEPTH):
        kind, j = i % N_MIXERS, i // N_MIXERS
        plan.append([(w, j) for w in mixer_weights[kind]])
        plan.append([(mlp_w_up, i), (mlp_w_down, i), (ple_w_gate, i)])
    plan.append([])
    (cos_t, sin_t), ready = _rope_tables(positions, plan[0] + [(ple_w_proj, i) for i in range(DEPTH)])
    ready, w_projs = ready[:-DEPTH], ready[-DEPTH:]

    h = x
    for i in range(DEPTH):
        kind, j = i % N_MIXERS, i // N_MIXERS
        if kind == 0:
            h, ready = _conv_mixer(h, norm_mix, i, *ready, sc_w_conv, j, plan[2 * i + 1])
        elif kind == 1:
            h, ready = _attention_mixer(h, cos_t, sin_t, norm_mix, i, *ready, plan[2 * i + 1])
        else:
            h, ready = _lru_mixer(h, norm_mix, i, *ready, lru_conv_w, lru_conv_b, lru_w_ax, lru_b_a,
                                  lru_b_x, lru_lambda, j, plan[2 * i + 1])
        h, ready = _post(h, p, norm_mlp, norm_ple, norm_final[None], *ready, w_projs[i], i,
                         plan[2 * i + 2])
    return h
```
